```python
import math
import jax, jax.numpy as jnp
from jax import lax
import numpy as np

D_MODEL = 1024
BATCH = 8
SEQ = 4096
DEPTH = 1

ATT_HEADS = 8
ATT_HEAD_DIM = 64
ATT_WIDTH = ATT_HEADS * ATT_HEAD_DIM
IDX_HEADS = 8
IDX_HEAD_DIM = 64
TOPK_MAX = 256
Q_BLOCK = 128
MLSTM_HEADS = 4
MLSTM_QK_DIM = 64
MLSTM_V_DIM = 128
MLSTM_WIDTH = MLSTM_HEADS * MLSTM_V_DIM
MLSTM_CHUNK = 64
CONV_WIDTH = 4
N_BUCKETS = 32
MAX_DISTANCE = 128
N_EXPERTS = 32
TOP_K = 4
D_FF = 1024
SWIGLU_ALPHA = 1.702
SWIGLU_LIMIT = 7.0
MOE_BLOCK = 256
LN_EPS = 1e-5
DEEPNORM_ALPHA = (2 * DEPTH) ** 0.25
DEEPNORM_BETA = (8 * DEPTH) ** -0.25

SPLIT_SIZES = (
    ATT_WIDTH, ATT_WIDTH, ATT_WIDTH,
    IDX_HEADS * IDX_HEAD_DIM, IDX_HEAD_DIM, IDX_HEADS,
    MLSTM_HEADS * MLSTM_QK_DIM, MLSTM_HEADS * MLSTM_QK_DIM, MLSTM_WIDTH,
    MLSTM_HEADS, MLSTM_HEADS, MLSTM_WIDTH,
    D_MODEL, D_MODEL,
)
IN_WIDTH = sum(SPLIT_SIZES)

kernel_name = "hybrid_dsa_mlstm_moe_deepnorm"


def layer_norm(x, g, b):
    xf = x.astype(jnp.float32)
    mu = jnp.mean(xf, axis=-1, keepdims=True)
    var = jnp.mean(jnp.square(xf - mu), axis=-1, keepdims=True)
    y = (xf - mu) * lax.rsqrt(var + LN_EPS)
    return (y * g.astype(jnp.float32) + b.astype(jnp.float32)).astype(x.dtype)


def t5_bucket(rel):
    n = jnp.maximum(rel, 0)
    max_exact = N_BUCKETS // 2
    n_f = jnp.maximum(n, 1).astype(jnp.float32)
    large = max_exact + (jnp.log(n_f / max_exact) / math.log(MAX_DISTANCE / max_exact)
                         * (N_BUCKETS - max_exact)).astype(jnp.int32)
    large = jnp.minimum(large, N_BUCKETS - 1)
    return jnp.where(n < max_exact, n, large)


def dsa_attention(q, k, v, q_idx, k_idx, w_idx, rel_bias):
    B, S, H, Dh = q.shape
    n_sel = min(TOPK_MAX, S // 4)
    n_blk = S // Q_BLOCK
    key_pos = jnp.arange(S, dtype=jnp.int32)

    def to_blocks(a):
        return jnp.moveaxis(a.reshape(B, n_blk, Q_BLOCK, *a.shape[2:]), 1, 0)

    def block(args):
        q_b, qi_b, wi_b, start = args
        t_pos = start + jnp.arange(Q_BLOCK, dtype=jnp.int32)
        dots = jnp.einsum('bqhd,bsd->bqhs', qi_b, k_idx).astype(jnp.float32) * IDX_HEAD_DIM ** -0.5
        score = jnp.einsum('bqh,bqhs->bqs', wi_b.astype(jnp.float32) * IDX_HEADS ** -0.5,
                           jax.nn.relu(dots))
        score = jnp.where(key_pos[None, None, :] <= t_pos[None, :, None], score, -jnp.inf)
        _, sel = lax.top_k(score, n_sel)
        k_g = jax.vmap(lambda kk, ii: kk[ii])(k, sel)
        v_g = jax.vmap(lambda vv, ii: vv[ii])(v, sel)
        rel = t_pos[None, :, None] - sel
        logits = jnp.einsum('bqhd,bqjhd->bqhj', q_b, k_g).astype(jnp.float32) * ATT_HEAD_DIM ** -0.5
        bias = jnp.moveaxis(rel_bias[:, t5_bucket(rel)], 0, 2).astype(jnp.float32)
        logits = jnp.where((rel >= 0)[:, :, None, :], logits + bias, -jnp.inf)
        p = jax.nn.softmax(logits, axis=-1).astype(v.dtype)
        return jnp.einsum('bqhj,bqjhd->bqhd', p, v_g)

    starts = jnp.arange(n_blk, dtype=jnp.int32) * Q_BLOCK
    out = lax.map(block, (to_blocks(q), to_blocks(q_idx), to_blocks(w_idx), starts))
    return jnp.moveaxis(out, 0, 1).reshape(B, S, H * Dh)


def causal_depthwise_conv(x, w, b):
    C = x.shape[-1]
    y = lax.conv_general_dilated(x, w[:, None, :].astype(x.dtype), window_strides=(1,),
                                 padding=[(CONV_WIDTH - 1, 0)],
                                 dimension_numbers=('NWC', 'WIO', 'NWC'),
                                 feature_group_count=C)
    return y + b


def mlstm_chunkwise(q, k, v, i_pre, f_pre):
    B, S, H, Dk = q.shape
    Dv = v.shape[-1]
    L = MLSTM_CHUNK
    NC = S // L
    q = q.reshape(B, NC, L, H, Dk)
    k = k.reshape(B, NC, L, H, Dk)
    v = v.reshape(B, NC, L, H, Dv)
    log_f = jax.nn.log_sigmoid(f_pre).reshape(B, NC, L, H)
    log_i = i_pre.reshape(B, NC, L, H)
    b = jnp.cumsum(log_f, axis=2)
    b_end = b[:, :, -1, :]
    a = b_end[:, :, None, :] - b + log_i
    m_loc = jnp.max(a, axis=2)
    w_loc = jnp.exp(a - m_loc[:, :, None, :])
    c_loc = jnp.einsum('bclh,bclhk,bclhv->bchkv', w_loc, k, v)
    n_loc = jnp.einsum('bclh,bclhk->bchk', w_loc, k)

    def step(carry, inp):
        c, n, m = carry
        cl, nl, ml, be = inp
        m_new = jnp.maximum(be + m, ml)
        decay = jnp.exp(be + m - m_new)
        inject = jnp.exp(ml - m_new)
        c_new = decay[..., None, None] * c + inject[..., None, None] * cl
        n_new = decay[..., None] * n + inject[..., None] * nl
        return (c_new, n_new, m_new), (c, n, m)

    init = (jnp.zeros((B, H, Dk, Dv), jnp.float32), jnp.zeros((B, H, Dk), jnp.float32),
            jnp.zeros((B, H), jnp.float32))
    xs = (jnp.moveaxis(c_loc, 1, 0), jnp.moveaxis(n_loc, 1, 0),
          jnp.moveaxis(m_loc, 1, 0), jnp.moveaxis(b_end, 1, 0))
    _, (c0, n0, m0) = lax.scan(step, init, xs)
    c0 = jnp.moveaxis(c0, 0, 1)
    n0 = jnp.moveaxis(n0, 0, 1)
    m0 = jnp.moveaxis(m0, 0, 1)

    d = b[:, :, :, None, :] - b[:, :, None, :, :] + log_i[:, :, None, :, :]
    causal = jnp.tril(jnp.ones((L, L), dtype=bool))
    d = jnp.where(causal[None, None, :, :, None], d, -jnp.inf)
    m_inter = b + m0[:, :, None, :]
    m_t = jnp.maximum(m_inter, jnp.max(d, axis=3))
    p = jnp.einsum('bcthk,bcshk->bctsh', q, k) * jnp.exp(d - m_t[:, :, :, None, :])
    scale_inter = jnp.exp(m_inter - m_t)
    num = (jnp.einsum('bctsh,bcshv->bcthv', p, v)
           + scale_inter[..., None] * jnp.einsum('bcthk,bchkv->bcthv', q, c0))
    den = jnp.sum(p, axis=3) + scale_inter * jnp.einsum('bcthk,bchk->bcth', q, n0)
    h = num / jnp.maximum(jnp.abs(den), jnp.exp(-m_t))[..., None]
    return h.reshape(B, S, H, Dv)


def token_mixers(x, w_in, conv_w, conv_b, i_bias, f_bias, norm_g,
                 w_branch_attn, w_branch_mlstm, w_out, rel_bias):
    B, S, _ = x.shape
    proj = x @ w_in
    split_points = np.cumsum(SPLIT_SIZES)[:-1].tolist()
    (q_a, k_a, v_a, q_i, k_i, w_i, q_m, k_m, v_m, i_m, f_m, o_m, g_a, g_m) = jnp.split(
        proj, split_points, axis=-1)

    y_att = dsa_attention(q_a.reshape(B, S, ATT_HEADS, ATT_HEAD_DIM),
                          k_a.reshape(B, S, ATT_HEADS, ATT_HEAD_DIM),
                          v_a.reshape(B, S, ATT_HEADS, ATT_HEAD_DIM),
                          q_i.reshape(B, S, IDX_HEADS, IDX_HEAD_DIM), k_i, w_i, rel_bias)

    qk_m = jax.nn.silu(causal_depthwise_conv(jnp.concatenate([q_m, k_m], axis=-1), conv_w, conv_b))
    q_m, k_m = jnp.split(qk_m, 2, axis=-1)
    h_m = mlstm_chunkwise(
        q_m.reshape(B, S, MLSTM_HEADS, MLSTM_QK_DIM).astype(jnp.float32),
        k_m.reshape(B, S, MLSTM_HEADS, MLSTM_QK_DIM).astype(jnp.float32) * MLSTM_QK_DIM ** -0.5,
        v_m.reshape(B, S, MLSTM_HEADS, MLSTM_V_DIM).astype(jnp.float32),
        (i_m + i_bias).astype(jnp.float32),
        (f_m + f_bias).astype(jnp.float32))
    mu = jnp.mean(h_m, axis=-1, keepdims=True)
    var = jnp.mean(jnp.square(h_m - mu), axis=-1, keepdims=True)
    h_m = (h_m - mu) * lax.rsqrt(var + LN_EPS) * norm_g.reshape(MLSTM_HEADS, MLSTM_V_DIM).astype(jnp.float32)
    h_m = h_m * jax.nn.sigmoid(o_m.reshape(B, S, MLSTM_HEADS, MLSTM_V_DIM).astype(jnp.float32))
    y_m = h_m.reshape(B, S, MLSTM_WIDTH).astype(x.dtype)

    mix = jax.nn.sigmoid(g_a) * (y_att @ w_branch_attn) + jax.nn.sigmoid(g_m) * (y_m @ w_branch_mlstm)
    return mix @ w_out


def moe_ffn(h, w_router, b_router, w_gate_up, b_gate_up, w_down, b_down):
    B, S, D = h.shape
    T = B * S
    xt = h.reshape(T, D)
    logits = (xt @ w_router + b_router).astype(jnp.float32)
    top_vals, top_idx = lax.top_k(logits, TOP_K)
    gates = jax.nn.softmax(top_vals, axis=-1)
    P = T * TOP_K
    e_flat = top_idx.reshape(P)
    tok_flat = jnp.arange(P, dtype=jnp.int32) // TOP_K
    g_flat = gates.reshape(P)
    order = jnp.argsort(e_flat)
    e_sorted = e_flat[order]
    counts = jnp.bincount(e_flat, length=N_EXPERTS)
    start = jnp.cumsum(counts) - counts
    padded = ((counts + MOE_BLOCK - 1) // MOE_BLOCK) * MOE_BLOCK
    pend = jnp.cumsum(padded)
    pstart = pend - padded
    dest = pstart[e_sorted] + (jnp.arange(P, dtype=jnp.int32) - start[e_sorted])
    cap = ((P + MOE_BLOCK - 1) // MOE_BLOCK) * MOE_BLOCK + N_EXPERTS * MOE_BLOCK
    n_blocks = cap // MOE_BLOCK
    buf_tok = jnp.zeros((cap,), jnp.int32).at[dest].set(tok_flat[order])
    buf_gate = jnp.zeros((cap,), jnp.float32).at[dest].set(g_flat[order])
    block_expert = jnp.searchsorted(pend, jnp.arange(n_blocks, dtype=jnp.int32) * MOE_BLOCK, side='right')
    block_expert = jnp.minimum(block_expert, N_EXPERTS - 1)

    def expert_block(args):
        tok, e = args
        xb = xt[tok]
        gu = xb @ w_gate_up[e] + b_gate_up[e]
        gate, up = gu[:, :D_FF], gu[:, D_FF:]
        gate = jnp.minimum(gate, SWIGLU_LIMIT)
        up = jnp.clip(up, -SWIGLU_LIMIT, SWIGLU_LIMIT)
        act = (up + 1) * (gate * jax.nn.sigmoid(SWIGLU_ALPHA * gate))
        return act @ w_down[e] + b_down[e]

    y_blocks = lax.map(expert_block, (buf_tok.reshape(n_blocks, MOE_BLOCK), block_expert))
    y_buf = y_blocks.reshape(cap, D) * buf_gate[:, None].astype(h.dtype)
    y = jnp.zeros((T, D), h.dtype).at[buf_tok].add(y_buf)
    return y.reshape(B, S, D)


def setup_inputs(seed: int = 0) -> dict:
    key = jax.random.key(seed)
    ks = jax.random.split(key, 21)
    nrm = jax.random.normal
    qk_cols = 2 * MLSTM_HEADS * MLSTM_QK_DIM
    return {
        "x": nrm(ks[0], (BATCH, SEQ, D_MODEL), jnp.float32),
        "w_in": nrm(ks[1], (DEPTH, D_MODEL, IN_WIDTH), jnp.float32) * D_MODEL ** -0.5,
        "conv_w": nrm(ks[2], (DEPTH, CONV_WIDTH, qk_cols), jnp.float32) * CONV_WIDTH ** -0.5,
        "conv_b": 0.02 * nrm(ks[3], (DEPTH, qk_cols), jnp.float32),
        "mlstm_i_bias": -2.0 + 0.1 * nrm(ks[4], (DEPTH, MLSTM_HEADS), jnp.float32),
        "mlstm_f_bias": jnp.linspace(3.0, 6.0, MLSTM_HEADS, dtype=jnp.float32)
                        + 0.1 * nrm(ks[5], (DEPTH, MLSTM_HEADS), jnp.float32),
        "mlstm_norm_g": 1.0 + 0.02 * nrm(ks[6], (DEPTH, MLSTM_WIDTH), jnp.float32),
        "w_branch_attn": nrm(ks[7], (DEPTH, ATT_WIDTH, D_MODEL), jnp.float32) * ATT_WIDTH ** -0.5,
        "w_branch_mlstm": nrm(ks[8], (DEPTH, MLSTM_WIDTH, D_MODEL), jnp.float32) * MLSTM_WIDTH ** -0.5,
        "w_out": nrm(ks[9], (DEPTH, D_MODEL, D_MODEL), jnp.float32) * (D_MODEL ** -0.5 * DEEPNORM_BETA),
        "ln1_g": 1.0 + 0.02 * nrm(ks[10], (DEPTH, D_MODEL), jnp.float32),
        "ln1_b": 0.02 * nrm(ks[11], (DEPTH, D_MODEL), jnp.float32),
        "w_router": nrm(ks[12], (DEPTH, D_MODEL, N_EXPERTS), jnp.float32) * D_MODEL ** -0.5,
        "b_router": 0.01 * nrm(ks[13], (DEPTH, N_EXPERTS), jnp.float32),
        "w_gate_up": nrm(ks[14], (DEPTH, N_EXPERTS, D_MODEL, 2 * D_FF), jnp.float32) * D_MODEL ** -0.5,
        "b_gate_up": 0.02 * nrm(ks[15], (DEPTH, N_EXPERTS, 2 * D_FF), jnp.float32),
        "w_down": nrm(ks[16], (DEPTH, N_EXPERTS, D_FF, D_MODEL), jnp.float32) * (D_FF ** -0.5 * DEEPNORM_BETA),
        "b_down": 0.02 * nrm(ks[17], (DEPTH, N_EXPERTS, D_MODEL), jnp.float32),
        "ln2_g": 1.0 + 0.02 * nrm(ks[18], (DEPTH, D_MODEL), jnp.float32),
        "ln2_b": 0.02 * nrm(ks[19], (DEPTH, D_MODEL), jnp.float32),
        "rel_bias": 0.5 * nrm(ks[20], (ATT_HEADS, N_BUCKETS), jnp.float32),
    }


def reference(x, w_in, conv_w, conv_b, mlstm_i_bias, mlstm_f_bias, mlstm_norm_g,
              w_branch_attn, w_branch_mlstm, w_out, ln1_g, ln1_b, w_router, b_router,
              w_gate_up, b_gate_up, w_down, b_down, ln2_g, ln2_b, rel_bias):
    for l in range(DEPTH):
        y = token_mixers(x, w_in[l], conv_w[l], conv_b[l], mlstm_i_bias[l], mlstm_f_bias[l],
                         mlstm_norm_g[l], w_branch_attn[l], w_branch_mlstm[l], w_out[l], rel_bias)
        x = layer_norm(DEEPNORM_ALPHA * x + y, ln1_g[l], ln1_b[l])
        y = moe_ffn(x, w_router[l], b_router[l], w_gate_up[l], b_gate_up[l], w_down[l], b_down[l])
        x = layer_norm(DEEPNORM_ALPHA * x + y, ln2_g[l], ln2_b[l])
    return x
```

```python
import functools
import math

import jax
import jax.numpy as jnp
from jax import lax
from jax.experimental import pallas as pl
from jax.experimental.pallas import tpu as pltpu

ATT_HEADS = 8
ATT_HEAD_DIM = 64
ATT_WIDTH = ATT_HEADS * ATT_HEAD_DIM
IDX_HEADS = 8
IDX_HEAD_DIM = 64
TOPK_MAX = 256
MLSTM_HEADS = 4
MLSTM_QK_DIM = 64
MLSTM_V_DIM = 128
MLSTM_WIDTH = MLSTM_HEADS * MLSTM_V_DIM
CONV_WIDTH = 4
N_BUCKETS = 32
MAX_DISTANCE = 128
N_EXPERTS = 32
TOP_K = 4
SWIGLU_ALPHA = 1.702
SWIGLU_LIMIT = 7.0
LN_EPS = 1e-5
DEPTH = 1
DEEPNORM_ALPHA = (2 * DEPTH) ** 0.25

LANES = 128
SUBLANES = 8
VMEM_LIMIT_BYTES = 56 * 1024 * 1024

MXU_DTYPE = jnp.bfloat16

INT_MIN = -(2 ** 31)
NEG_BIG = -1e30

QK_M = MLSTM_HEADS * MLSTM_QK_DIM
COLS_A = 3 * ATT_WIDTH + IDX_HEADS * IDX_HEAD_DIM
COLS_S = LANES
COLS_M = 2 * QK_M + 2 * MLSTM_WIDTH
S_KI = 0
S_WI = IDX_HEAD_DIM
S_IM = S_WI + IDX_HEADS
S_FM = S_IM + MLSTM_HEADS


def _cparams(sem):
    return pltpu.CompilerParams(dimension_semantics=sem, vmem_limit_bytes=VMEM_LIMIT_BYTES)


def _proj_kernel(x_ref, w_ref, oa_ref, os_ref, om_ref, og_ref, *, d_model):
    xb = x_ref[...].astype(MXU_DTYPE)
    step = 512

    def mm(lo, hi):
        return jnp.dot(xb, w_ref[:, lo:hi], preferred_element_type=jnp.float32)

    base = 0
    for j in range(0, COLS_A, step):
        oa_ref[:, j:j + step] = mm(base + j, base + j + step).astype(MXU_DTYPE)
    base += COLS_A
    os_ref[...] = mm(base, base + COLS_S)
    base += COLS_S
    for j in range(0, COLS_M, step):
        om_ref[:, j:j + step] = mm(base + j, base + j + step)
    base += COLS_M
    for j in range(0, 2 * d_model, step):
        og_ref[:, j:j + step] = mm(base + j, base + j + step)


def _project(x2, w_p, tm):
    T, D = x2.shape
    n_all = w_p.shape[1]
    return pl.pallas_call(
        functools.partial(_proj_kernel, d_model=D),
        grid=(T // tm,),
        in_specs=[pl.BlockSpec((tm, D), lambda i: (i, 0)),
                  pl.BlockSpec((D, n_all), lambda i: (0, 0))],
        out_specs=[pl.BlockSpec((tm, COLS_A), lambda i: (i, 0)),
                   pl.BlockSpec((tm, COLS_S), lambda i: (i, 0)),
                   pl.BlockSpec((tm, COLS_M), lambda i: (i, 0)),
                   pl.BlockSpec((tm, 2 * D), lambda i: (i, 0))],
        out_shape=[jax.ShapeDtypeStruct((T, COLS_A), MXU_DTYPE),
                   jax.ShapeDtypeStruct((T, COLS_S), jnp.float32),
                   jax.ShapeDtypeStruct((T, COLS_M), jnp.float32),
                   jax.ShapeDtypeStruct((T, 2 * D), jnp.float32)],
        compiler_params=_cparams(("parallel",)),
    )(x2, w_p)


def _bias_kernel(rb_ref, o_ref, *, tq):
    h = pl.program_id(0)
    t = lax.broadcasted_iota(jnp.int32, (tq, tq), 0)
    s = lax.broadcasted_iota(jnp.int32, (tq, tq), 1)
    max_exact = N_BUCKETS // 2
    for d in range(3):
        n = jnp.maximum(t - s + d * tq, 0)
        n_f = jnp.maximum(n, 1).astype(jnp.float32)
        large = max_exact + (jnp.log(n_f / max_exact) / math.log(MAX_DISTANCE / max_exact)
                             * (N_BUCKETS - max_exact)).astype(jnp.int32)
        large = jnp.minimum(large, N_BUCKETS - 1)
        bucket = jnp.where(n < max_exact, n, large)
        acc = jnp.zeros((tq, tq), jnp.float32)
        for k in range(N_BUCKETS):
            acc = jnp.where(bucket == k, rb_ref[h, k], acc)
        o_ref[0, d] = acc


def _bias_tiles(rel_bias, tq):
    assert tq + 1 >= MAX_DISTANCE
    H = rel_bias.shape[0]
    return pl.pallas_call(
        functools.partial(_bias_kernel, tq=tq),
        grid=(H,),
        in_specs=[pl.BlockSpec(memory_space=pltpu.SMEM)],
        out_specs=pl.BlockSpec((1, 3, tq, tq), lambda h: (h, 0, 0, 0)),
        out_shape=jax.ShapeDtypeStruct((H, 3, tq, tq), jnp.float32),
        compiler_params=_cparams(("parallel",)),
    )(rel_bias)


def _dsa_kernel(qa_ref, ka_ref, va_ref, qi_ref, ks_ref, wq_ref, bias_ref, o_ref,
                keys_ref, msk_ref, acc_ref, m_ref, l_ref, *, tq, n_sel):
    i = pl.program_id(1)
    nch = i + 1
    nt = (((1,), (1,)), ((), ()))
    t_pos = i * tq + lax.broadcasted_iota(jnp.int32, (tq, 1), 0)
    s_loc = lax.broadcasted_iota(jnp.int32, (1, tq), 1)

    ws = wq_ref[:, S_WI:S_WI + IDX_HEADS] * (IDX_HEADS ** -0.5)
    qi = qi_ref[...]

    def score_body(c, carry):
        off = pl.multiple_of(c * tq, tq)
        kic = ks_ref[pl.ds(off, tq), S_KI:S_KI + IDX_HEAD_DIM].astype(MXU_DTYPE)
        sc = jnp.zeros((tq, tq), jnp.float32)
        for h in range(IDX_HEADS):
            d = lax.dot_general(qi[:, h * IDX_HEAD_DIM:(h + 1) * IDX_HEAD_DIM], kic, nt,
                                preferred_element_type=jnp.float32)
            sc = sc + ws[:, h:h + 1] * jnp.maximum(d, 0.0)
        sc = sc + 0.0
        b = pltpu.bitcast(sc, jnp.int32)
        sk = b ^ ((b >> 31) & jnp.int32(0x7FFFFFFF))
        sk = jnp.where(off + s_loc <= t_pos, sk, jnp.int32(INT_MIN))
        keys_ref[c] = sk
        return carry

    lax.fori_loop(0, nch, score_body, 0)

    def count(pred_fn):
        def cnt_body(c, acc):
            hit = pred_fn(keys_ref[c])
            for j in range(tq // LANES):
                acc = acc + jnp.where(hit[:, j * LANES:(j + 1) * LANES], 1.0, 0.0)
            return acc
        acc = lax.fori_loop(0, nch, cnt_body, jnp.zeros((tq, LANES), jnp.float32))
        return jnp.sum(acc, axis=1, keepdims=True)

    def bit_body(it, res):
        cand = res ^ lax.shift_left(jnp.int32(1), 31 - it)
        cnt = count(lambda kc: kc >= cand)
        return jnp.where(cnt >= n_sel, cand, res)

    thr = lax.fori_loop(0, 32, bit_body, jnp.full((tq, 1), INT_MIN, jnp.int32))
    n_gt = count(lambda kc: kc > thr)
    need = jnp.where(thr == INT_MIN, 0.0, n_sel - n_gt)

    tri = jnp.where(lax.broadcasted_iota(jnp.int32, (tq, tq), 0)
                    <= lax.broadcasted_iota(jnp.int32, (tq, tq), 1), 1.0, 0.0).astype(MXU_DTYPE)

    def mask_body(c, run):
        kc = keys_ref[c]
        eq = kc == thr
        eqf = jnp.where(eq, 1.0, 0.0)
        prefix = jnp.dot(eqf.astype(MXU_DTYPE), tri, preferred_element_type=jnp.float32) + run
        msk_ref[c] = jnp.where(kc > thr, 0.0,
                               jnp.where(eq, jnp.where(prefix <= need, 0.0, NEG_BIG), NEG_BIG))
        return run + jnp.sum(eqf, axis=1, keepdims=True)

    lax.fori_loop(0, nch, mask_body, jnp.zeros((tq, 1), jnp.float32))

    m_ref[...] = jnp.full(m_ref.shape, NEG_BIG, jnp.float32)
    l_ref[...] = jnp.zeros(l_ref.shape, jnp.float32)
    acc_ref[...] = jnp.zeros(acc_ref.shape, jnp.float32)
    scale = ATT_HEAD_DIM ** -0.5

    def att_body(c, carry):
        off = pl.multiple_of(c * tq, tq)
        mk = msk_ref[c]
        bsel = jnp.minimum(i - c, 2)
        for h in range(ATT_HEADS):
            lo, hi = h * ATT_HEAD_DIM, (h + 1) * ATT_HEAD_DIM
            s = lax.dot_general(qa_ref[:, lo:hi], ka_ref[pl.ds(off, tq), lo:hi], nt,
                                preferred_element_type=jnp.float32)
            s = s * scale + bias_ref[h, bsel] + mk
            m_prev = m_ref[h]
            m_new = jnp.maximum(m_prev, jnp.max(s, axis=1, keepdims=True))
            alpha = jnp.exp(m_prev - m_new)
            p = jnp.exp(s - m_new)
            l_ref[h] = alpha * l_ref[h] + jnp.sum(p, axis=1, keepdims=True)
            acc_ref[h] = alpha * acc_ref[h] + jnp.dot(
                p.astype(MXU_DTYPE), va_ref[pl.ds(off, tq), lo:hi],
                preferred_element_type=jnp.float32)
            m_ref[h] = m_new
        return carry

    lax.fori_loop(0, nch, att_body, 0)
    for h in range(ATT_HEADS):
        o_ref[:, h * ATT_HEAD_DIM:(h + 1) * ATT_HEAD_DIM] = (acc_ref[h] / l_ref[h]).astype(o_ref.dtype)


def _dsa_attention(oa, osm, bias_t, B, S, tq):
    nq = S // tq
    n_sel = min(TOPK_MAX, S // 4)
    W = ATT_WIDTH
    return pl.pallas_call(
        functools.partial(_dsa_kernel, tq=tq, n_sel=n_sel),
        grid=(B, nq),
        in_specs=[pl.BlockSpec((tq, W), lambda b, i: (b * nq + i, 0)),
                  pl.BlockSpec((S, W), lambda b, i: (b, 1)),
                  pl.BlockSpec((S, W), lambda b, i: (b, 2)),
                  pl.BlockSpec((tq, W), lambda b, i: (b * nq + i, 3)),
                  pl.BlockSpec((S, COLS_S), lambda b, i: (b, 0)),
                  pl.BlockSpec((tq, COLS_S), lambda b, i: (b * nq + i, 0)),
                  pl.BlockSpec((ATT_HEADS, 3, tq, tq), lambda b, i: (0, 0, 0, 0))],
        out_specs=pl.BlockSpec((tq, W), lambda b, i: (b * nq + i, 0)),
        out_shape=jax.ShapeDtypeStruct((B * S, W), MXU_DTYPE),
        scratch_shapes=[pltpu.VMEM((nq, tq, tq), jnp.int32),
                        pltpu.VMEM((nq, tq, tq), jnp.float32),
                        pltpu.VMEM((ATT_HEADS, tq, ATT_HEAD_DIM), jnp.float32),
                        pltpu.VMEM((ATT_HEADS, tq, 1), jnp.float32),
                        pltpu.VMEM((ATT_HEADS, tq, 1), jnp.float32)],
        compiler_params=_cparams(("parallel", "arbitrary")),
    )(oa, oa, oa, oa, osm, osm, bias_t)


def _mlstm_kernel(qk_ref, v_ref, og_ref, s_ref, cw_ref, cb_ref, gb_ref, ng_ref, y_ref,
                  xbuf, cst, mst, *, L):
    c = pl.program_id(1)
    H, DK, DV = MLSTM_HEADS, MLSTM_QK_DIM, MLSTM_V_DIM
    tail = SUBLANES

    @pl.when(c == 0)
    def _():
        xbuf[0:tail, :] = jnp.zeros((tail, 2 * QK_M), jnp.float32)
        cst[...] = jnp.zeros(cst.shape, jnp.float32)
        mst[...] = jnp.zeros(mst.shape, jnp.float32)

    @pl.when(c > 0)
    def _():
        xbuf[0:tail, :] = xbuf[L:L + tail, :]

    xbuf[tail:tail + L, :] = qk_ref[...]
    conv = cb_ref[...] + jnp.zeros((L, 2 * QK_M), jnp.float32)
    for j in range(CONV_WIDTH):
        conv = conv + cw_ref[j:j + 1, :] * xbuf[pl.ds(tail - (CONV_WIDTH - 1) + j, L), :]
    qk = conv * jax.nn.sigmoid(conv)
    q = qk[:, :QK_M]
    kT = (qk[:, QK_M:] * (DK ** -0.5)).T

    g = s_ref[...] + gb_ref[...]
    logf = jnp.minimum(g, 0.0) - jnp.log(1.0 + jnp.exp(-jnp.abs(g)))
    row = lax.broadcasted_iota(jnp.int32, (L, L), 0)
    col = lax.broadcasted_iota(jnp.int32, (L, L), 1)
    causal = row >= col
    bcum = jnp.dot(jnp.where(causal, 1.0, 0.0), logf, precision=lax.Precision.HIGHEST,
                   preferred_element_type=jnp.float32)
    gT = g.T
    bT = bcum.T
    ones_col = jnp.where(lax.broadcasted_iota(jnp.int32, (L, LANES), 1) == 0, 1.0, 0.0
                         ).astype(MXU_DTYPE)

    for h in range(H):
        bc = bcum[:, S_FM + h:S_FM + h + 1]
        br = bT[S_FM + h:S_FM + h + 1, :]
        lir = gT[S_IM + h:S_IM + h + 1, :]
        m0 = mst[h:h + 1, 0:1]
        d = jnp.where(causal, bc - br + lir, -jnp.inf)
        m_inter = bc + m0
        m_t = jnp.maximum(m_inter, jnp.max(d, axis=1, keepdims=True))
        qh = q[:, h * DK:(h + 1) * DK].astype(MXU_DTYPE)
        kTh = kT[h * DK:(h + 1) * DK, :]
        p = jnp.dot(qh, kTh.astype(MXU_DTYPE), preferred_element_type=jnp.float32) * jnp.exp(d - m_t)
        sc = jnp.exp(m_inter - m_t)
        vh = v_ref[:, h * DV:(h + 1) * DV].astype(MXU_DTYPE)
        caug = cst[h]
        inter = jnp.dot(qh, caug.astype(MXU_DTYPE), preferred_element_type=jnp.float32)
        num = jnp.dot(p.astype(MXU_DTYPE), vh, preferred_element_type=jnp.float32) + sc * inter[:, :DV]
        den = jnp.sum(p, axis=1, keepdims=True) + sc * inter[:, DV:DV + 1]
        hh = num / jnp.maximum(jnp.abs(den), jnp.exp(-m_t))
        b_end = bc[L - 1:L, :]
        a = b_end - br + lir
        m_loc = jnp.max(a, axis=1, keepdims=True)
        m_new = jnp.maximum(b_end + m0, m_loc)
        kTw = (kTh * jnp.exp(a - m_new)).astype(MXU_DTYPE)
        vaug = jnp.concatenate([vh, ones_col], axis=1)
        cst[h] = jnp.exp(b_end + m0 - m_new) * caug + jnp.dot(kTw, vaug, preferred_element_type=jnp.float32)
        mst[h:h + 1, :] = jnp.broadcast_to(m_new, (1, LANES))
        mu = jnp.mean(hh, axis=1, keepdims=True)
        hc = hh - mu
        var = jnp.mean(hc * hc, axis=1, keepdims=True)
        hn = hc * lax.rsqrt(var + LN_EPS) * ng_ref[:, h * DV:(h + 1) * DV]
        y_ref[:, h * DV:(h + 1) * DV] = (hn * jax.nn.sigmoid(og_ref[:, h * DV:(h + 1) * DV])).astype(y_ref.dtype)


def _mlstm(om, osm, conv_w, conv_b, gate_bias, norm_g, B, S, L):
    nc = S // L
    W = MLSTM_WIDTH
    assert 2 * QK_M == W
    return pl.pallas_call(
        functools.partial(_mlstm_kernel, L=L),
        grid=(B, nc),
        in_specs=[pl.BlockSpec((L, W), lambda b, c: (b * nc + c, 0)),
                  pl.BlockSpec((L, W), lambda b, c: (b * nc + c, 1)),
                  pl.BlockSpec((L, W), lambda b, c: (b * nc + c, 2)),
                  pl.BlockSpec((L, COLS_S), lambda b, c: (b * nc + c, 0)),
                  pl.BlockSpec((CONV_WIDTH, W), lambda b, c: (0, 0)),
                  pl.BlockSpec((1, W), lambda b, c: (0, 0)),
                  pl.BlockSpec((1, COLS_S), lambda b, c: (0, 0)),
                  pl.BlockSpec((1, W), lambda b, c: (0, 0))],
        out_specs=pl.BlockSpec((L, W), lambda b, c: (b * nc + c, 0)),
        out_shape=jax.ShapeDtypeStruct((B * S, W), MXU_DTYPE),
        scratch_shapes=[pltpu.VMEM((L + 2 * SUBLANES, W), jnp.float32),
                        pltpu.VMEM((MLSTM_HEADS, MLSTM_QK_DIM, 2 * LANES), jnp.float32),
                        pltpu.VMEM((SUBLANES, LANES), jnp.float32)],
        compiler_params=_cparams(("parallel", "arbitrary")),
    )(om, om, om, osm, conv_w, conv_b, gate_bias, norm_g)


def _layer_norm(z, g, b):
    mu = jnp.mean(z, axis=1, keepdims=True)
    zc = z - mu
    var = jnp.mean(zc * zc, axis=1, keepdims=True)
    return zc * lax.rsqrt(var + LN_EPS) * g + b


def _merge_kernel(ya_ref, ym_ref, g_ref, x_ref, wa_ref, wm_ref, wo_ref, lg_ref, lb_ref,
                  wr_ref, br_ref, x1_ref, xp_ref, idx_ref, gate_ref, *, d_model):
    D = d_model
    mix = (jax.nn.sigmoid(g_ref[:, :D]) * jnp.dot(ya_ref[...], wa_ref[...], preferred_element_type=jnp.float32)
           + jax.nn.sigmoid(g_ref[:, D:]) * jnp.dot(ym_ref[...], wm_ref[...], preferred_element_type=jnp.float32))
    y = jnp.dot(mix.astype(MXU_DTYPE), wo_ref[...], preferred_element_type=jnp.float32)
    x1 = _layer_norm(DEEPNORM_ALPHA * x_ref[...] + y, lg_ref[...], lb_ref[...])
    x1_ref[...] = x1
    xb = x1.astype(MXU_DTYPE)
    bits = pltpu.bitcast(x1.astype(jnp.bfloat16).astype(jnp.float32), jnp.uint32)
    xp_ref[...] = (bits[:, :D // 2] & jnp.uint32(0xFFFF0000)) | (bits[:, D // 2:] >> 16)

    logits = jnp.dot(xb, wr_ref[...], preferred_element_type=jnp.float32) + br_ref[...]
    tm = logits.shape[0]
    lane = lax.broadcasted_iota(jnp.int32, (tm, LANES), 1)
    lane_f = lane.astype(jnp.float32)
    vals, idxs = [], []
    for _ in range(TOP_K):
        mx = jnp.max(logits, axis=1, keepdims=True)
        ix = jnp.min(jnp.where(logits == mx, lane_f, float(LANES)), axis=1, keepdims=True)
        vals.append(mx)
        idxs.append(ix)
        logits = jnp.where(lane_f == ix, -jnp.inf, logits)
    es = [jnp.exp(v - vals[0]) for v in vals]
    tot = es[0]
    for e in es[1:]:
        tot = tot + e
    idx_out = jnp.zeros((tm, LANES), jnp.float32)
    gate_out = jnp.zeros((tm, LANES), jnp.float32)
    for k in range(TOP_K):
        idx_out = jnp.where(lane == k, idxs[k], idx_out)
        gate_out = jnp.where(lane == k, es[k] / tot, gate_out)
    idx_ref[...] = idx_out.astype(jnp.int32)
    gate_ref[...] = gate_out


def _merge(y_att, y_m, og, x2, wa, wm, wo, ln_g, ln_b, wr, br, tm):
    T, D = x2.shape
    full = lambda shape: pl.BlockSpec(shape, lambda i: (0, 0))
    return pl.pallas_call(
        functools.partial(_merge_kernel, d_model=D),
        grid=(T // tm,),
        in_specs=[pl.BlockSpec((tm, ATT_WIDTH), lambda i: (i, 0)),
                  pl.BlockSpec((tm, MLSTM_WIDTH), lambda i: (i, 0)),
                  pl.BlockSpec((tm, 2 * D), lambda i: (i, 0)),
                  pl.BlockSpec((tm, D), lambda i: (i, 0)),
                  full(wa.shape), full(wm.shape), full(wo.shape),
                  full((1, D)), full((1, D)), full(wr.shape), full((1, LANES))],
        out_specs=[pl.BlockSpec((tm, D), lambda i: (i, 0)),
                   pl.BlockSpec((tm, D // 2), lambda i: (i, 0)),
                   pl.BlockSpec((tm, LANES), lambda i: (i, 0)),
                   pl.BlockSpec((tm, LANES), lambda i: (i, 0))],
        out_shape=[jax.ShapeDtypeStruct((T, D), jnp.float32),
                   jax.ShapeDtypeStruct((T, D // 2), jnp.uint32),
                   jax.ShapeDtypeStruct((T, LANES), jnp.int32),
                   jax.ShapeDtypeStruct((T, LANES), jnp.float32)],
        compiler_params=_cparams(("parallel",)),
    )(y_att, y_m, og, x2, wa, wm, wo, ln_g, ln_b, wr, br)


def _rank_kernel(idx_ref, rank_ref, cnt_ref, carry_ref):
    @pl.when(pl.program_id(0) == 0)
    def _():
        carry_ref[...] = jnp.zeros(carry_ref.shape, jnp.float32)

    tm = idx_ref.shape[0]
    lane = lax.broadcasted_iota(jnp.int32, (tm, LANES), 1)
    idx = idx_ref[...]
    hots = [lane == idx[:, k:k + 1] for k in range(TOP_K)]
    c = jnp.zeros((tm, LANES), jnp.float32)
    for hot in hots:
        c = c + jnp.where(hot, 1.0, 0.0)
    before = jnp.where(lax.broadcasted_iota(jnp.int32, (tm, tm), 0)
                       > lax.broadcasted_iota(jnp.int32, (tm, tm), 1), 1.0, 0.0).astype(MXU_DTYPE)
    carry = carry_ref[0:1, :]
    prior = jnp.dot(before, c.astype(MXU_DTYPE), preferred_element_type=jnp.float32) + carry
    out = jnp.zeros((tm, LANES), jnp.float32)
    for k, hot in enumerate(hots):
        out = jnp.where(lane == k, jnp.sum(jnp.where(hot, prior, 0.0), axis=1, keepdims=True), out)
    rank_ref[...] = out.astype(jnp.int32)
    total = carry + jnp.sum(c, axis=0, keepdims=True)
    carry_ref[...] = jnp.broadcast_to(total, carry_ref.shape)
    cnt_ref[...] = jnp.broadcast_to(total, cnt_ref.shape)


def _ranks(idx, tm):
    T = idx.shape[0]
    return pl.pallas_call(
        _rank_kernel,
        grid=(T // tm,),
        in_specs=[pl.BlockSpec((tm, LANES), lambda i: (i, 0))],
        out_specs=[pl.BlockSpec((tm, LANES), lambda i: (i, 0)),
                   pl.BlockSpec((SUBLANES, LANES), lambda i: (0, 0))],
        out_shape=[jax.ShapeDtypeStruct((T, LANES), jnp.int32),
                   jax.ShapeDtypeStruct((SUBLANES, LANES), jnp.float32)],
        scratch_shapes=[pltpu.VMEM((SUBLANES, LANES), jnp.float32)],
        compiler_params=_cparams(("arbitrary",)),
    )(idx)


def _row_copy(src_ref, s, dst_ref, d, sem):
    return pltpu.make_async_copy(src_ref.at[pl.ds(s, 1), :], dst_ref.at[pl.ds(d, 1), :], sem)


def _dispatch_kernel(dest_ref, x_ref, xs_in_ref, xs_ref, sem):
    del xs_in_ref
    tm = x_ref.shape[0]

    def start(t, carry):
        for k in range(TOP_K):
            _row_copy(x_ref, t, xs_ref, dest_ref[t * TOP_K + k], sem).start()
        return carry

    lax.fori_loop(0, tm, start, 0)

    def wait(t, carry):
        for k in range(TOP_K):
            _row_copy(x_ref, 0, xs_ref, 0, sem).wait()
        return carry

    lax.fori_loop(0, tm, wait, 0)


def _dispatch(dest_flat, xp, cap, tm):
    T, W = xp.shape
    xs0 = jnp.zeros((cap, W), xp.dtype)
    return pl.pallas_call(
        _dispatch_kernel,
        grid=(T // tm,),
        in_specs=[pl.BlockSpec((tm * TOP_K,), lambda i: (i,), memory_space=pltpu.SMEM),
                  pl.BlockSpec((tm, W), lambda i: (i, 0)),
                  pl.BlockSpec(memory_space=pl.ANY)],
        out_specs=pl.BlockSpec(memory_space=pl.ANY),
        out_shape=jax.ShapeDtypeStruct((cap, W), xp.dtype),
        scratch_shapes=[pltpu.SemaphoreType.DMA],
        input_output_aliases={2: 0},
        compiler_params=_cparams(("arbitrary",)),
    )(dest_flat, xp, xs0)


def _ffn_kernel(be_ref, nu_ref, xs_ref, wgu_ref, bgu_ref, wd_ref, bd_ref, y_ref, wgu_b, wd_b,
                *, d_ff):
    r = pl.program_id(0)
    e = be_ref[r]
    prev = be_ref[jnp.maximum(r - 1, 0)]

    @pl.when((r == 0) | (e != prev))
    def _():
        wgu_b[...] = wgu_ref[0].astype(MXU_DTYPE)
        wd_b[...] = wd_ref[0].astype(MXU_DTYPE)

    @pl.when(r < nu_ref[0])
    def _():
        w = xs_ref[...]
        half = w.shape[1]
        x_hi = pltpu.bitcast(w & jnp.uint32(0xFFFF0000), jnp.float32).astype(MXU_DTYPE)
        x_lo = pltpu.bitcast(w << 16, jnp.float32).astype(MXU_DTYPE)
        step = 512
        acc = bd_ref[0] + jnp.zeros(y_ref.shape, jnp.float32)
        for j in range(0, d_ff, step):
            def gu(lo):
                return (jnp.dot(x_hi, wgu_b[0:half, lo:lo + step], preferred_element_type=jnp.float32)
                        + jnp.dot(x_lo, wgu_b[half:2 * half, lo:lo + step], preferred_element_type=jnp.float32)
                        + bgu_ref[0, :, lo:lo + step])
            gate = jnp.minimum(gu(j), SWIGLU_LIMIT)
            up = jnp.clip(gu(d_ff + j), -SWIGLU_LIMIT, SWIGLU_LIMIT)
            act = (up + 1.0) * (gate * jax.nn.sigmoid(SWIGLU_ALPHA * gate))
            acc = acc + jnp.dot(act.astype(MXU_DTYPE), wd_b[j:j + step, :],
                                preferred_element_type=jnp.float32)
        y_ref[...] = acc

    @pl.when(r >= nu_ref[0])
    def _():
        y_ref[...] = jnp.zeros(y_ref.shape, jnp.float32)


def _expert_ffn(block_expert, n_used, xs, w_gate_up, b_gate_up, w_down, b_down, bm):
    cap, half = xs.shape
    E, D, F2 = w_gate_up.shape
    d_ff = F2 // 2
    grid_spec = pltpu.PrefetchScalarGridSpec(
        num_scalar_prefetch=2,
        grid=(cap // bm,),
        in_specs=[pl.BlockSpec((bm, half), lambda r, be, nu: (r, 0)),
                  pl.BlockSpec((1, D, F2), lambda r, be, nu: (be[r], 0, 0)),
                  pl.BlockSpec((1, 1, F2), lambda r, be, nu: (be[r], 0, 0)),
                  pl.BlockSpec((1, d_ff, D), lambda r, be, nu: (be[r], 0, 0)),
                  pl.BlockSpec((1, 1, D), lambda r, be, nu: (be[r], 0, 0))],
        out_specs=pl.BlockSpec((bm, D), lambda r, be, nu: (r, 0)),
        scratch_shapes=[pltpu.VMEM((D, F2), MXU_DTYPE),
                        pltpu.VMEM((d_ff, D), MXU_DTYPE)],
    )
    return pl.pallas_call(
        functools.partial(_ffn_kernel, d_ff=d_ff),
        grid_spec=grid_spec,
        out_shape=jax.ShapeDtypeStruct((cap, D), jnp.float32),
        compiler_params=_cparams(("arbitrary",)),
    )(block_expert, n_used, xs, w_gate_up, b_gate_up.reshape(E, 1, F2), w_down,
      b_down.reshape(E, 1, D))


def _combine_kernel(dest_ref, gate_ref, x1_ref, yb_ref, lg_ref, lb_ref, o_ref, buf, sem):
    tm = x1_ref.shape[0]

    def start(t, carry):
        for k in range(TOP_K):
            _row_copy(yb_ref, dest_ref[t * TOP_K + k], buf.at[k], t, sem).start()
        return carry

    lax.fori_loop(0, tm, start, 0)

    def wait(t, carry):
        for k in range(TOP_K):
            _row_copy(yb_ref, 0, buf.at[k], 0, sem).wait()
        return carry

    lax.fori_loop(0, tm, wait, 0)
    y = gate_ref[:, 0:1] * buf[0]
    for k in range(1, TOP_K):
        y = y + gate_ref[:, k:k + 1] * buf[k]
    o_ref[...] = _layer_norm(DEEPNORM_ALPHA * x1_ref[...] + y, lg_ref[...], lb_ref[...])


def _combine(dest_flat, gates, x1, ybuf, ln_g, ln_b, tm):
    T, D = x1.shape
    return pl.pallas_call(
        _combine_kernel,
        grid=(T // tm,),
        in_specs=[pl.BlockSpec((tm * TOP_K,), lambda i: (i,), memory_space=pltpu.SMEM),
                  pl.BlockSpec((tm, LANES), lambda i: (i, 0)),
                  pl.BlockSpec((tm, D), lambda i: (i, 0)),
                  pl.BlockSpec(memory_space=pl.ANY),
                  pl.BlockSpec((1, D), lambda i: (0, 0)),
                  pl.BlockSpec((1, D), lambda i: (0, 0))],
        out_specs=pl.BlockSpec((tm, D), lambda i: (i, 0)),
        out_shape=jax.ShapeDtypeStruct((T, D), jnp.float32),
        scratch_shapes=[pltpu.VMEM((TOP_K, tm, D), jnp.float32), pltpu.SemaphoreType.DMA],
        compiler_params=_cparams(("arbitrary",)),
    )(dest_flat, gates, x1, ybuf, ln_g, ln_b)


def _tile(n, pref):
    t = min(n, pref)
    assert n % t == 0
    return t


def _relayout_w_in(w_in, d_model):
    sizes = (ATT_WIDTH, ATT_WIDTH, ATT_WIDTH, IDX_HEADS * IDX_HEAD_DIM, IDX_HEAD_DIM, IDX_HEADS,
             QK_M, QK_M, MLSTM_WIDTH, MLSTM_HEADS, MLSTM_HEADS, MLSTM_WIDTH, d_model, d_model)
    offs = [0]
    for s in sizes:
        offs.append(offs[-1] + s)
    seg = lambda k: w_in[:, offs[k]:offs[k + 1]]
    (q_a, k_a, v_a, q_i, k_i, w_i, q_m, k_m, v_m, i_m, f_m, o_m, g_a, g_m) = [seg(k) for k in range(14)]
    pad = jnp.zeros((w_in.shape[0], COLS_S - (IDX_HEAD_DIM + IDX_HEADS + 2 * MLSTM_HEADS)), w_in.dtype)
    cols = [q_a, k_a, v_a, q_i, k_i, w_i, i_m, f_m, pad, q_m, k_m, v_m, o_m, g_a, g_m]
    return jnp.concatenate(cols, axis=1).astype(MXU_DTYPE)


def _layer(x2, B, S, w_in, conv_w, conv_b, i_bias, f_bias, norm_g, w_branch_attn, w_branch_mlstm,
           w_out, ln1_g, ln1_b, w_router, b_router, w_gate_up, b_gate_up, w_down, b_down,
           ln2_g, ln2_b, rel_bias):
    T, D = x2.shape
    bf = MXU_DTYPE
    tq = _tile(S, 256)
    L = _tile(S, 256)
    tm = _tile(T, 256)

    oa, osm, om, og = _project(x2, _relayout_w_in(w_in, D), tm)
    bias_t = _bias_tiles(rel_bias, tq)
    y_att = _dsa_attention(oa, osm, bias_t, B, S, tq)

    gate_bias = jnp.zeros((1, COLS_S), jnp.float32)
    gate_bias = gate_bias.at[0, S_IM:S_IM + MLSTM_HEADS].set(i_bias)
    gate_bias = gate_bias.at[0, S_FM:S_FM + MLSTM_HEADS].set(f_bias)
    y_m = _mlstm(om, osm, conv_w, conv_b.reshape(1, -1), gate_bias, norm_g.reshape(1, -1), B, S, L)

    wr = jnp.zeros((D, LANES), bf).at[:, :N_EXPERTS].set(w_router.astype(bf))
    br = jnp.full((1, LANES), NEG_BIG, jnp.float32).at[0, :N_EXPERTS].set(b_router)
    x1, xp, idx, gates = _merge(y_att, y_m, og, x2, w_branch_attn.astype(bf), w_branch_mlstm.astype(bf),
                                w_out.astype(bf), ln1_g.reshape(1, D), ln1_b.reshape(1, D), wr, br, tm)

    bm = 512
    rank, cnt = _ranks(idx, _tile(T, 512))
    counts = cnt[0, :N_EXPERTS].astype(jnp.int32)
    padded = ((counts + bm - 1) // bm) * bm
    pend = jnp.cumsum(padded)
    pstart = pend - padded
    cap = ((T * TOP_K + bm - 1) // bm) * bm + N_EXPERTS * bm
    n_blocks = cap // bm
    dest = (pstart[idx[:, :TOP_K]] + rank[:, :TOP_K]).reshape(T * TOP_K)
    block_expert = jnp.minimum(
        jnp.searchsorted(pend, jnp.arange(n_blocks, dtype=jnp.int32) * bm, side='right'),
        N_EXPERTS - 1).astype(jnp.int32)
    n_used = (pend[-1:] // bm).astype(jnp.int32)

    xs = _dispatch(dest, xp, cap, _tile(T, 512))
    ybuf = _expert_ffn(block_expert, n_used, xs, w_gate_up, b_gate_up, w_down, b_down, bm)
    return _combine(dest, gates, x1, ybuf, ln2_g.reshape(1, D), ln2_b.reshape(1, D), tm)


def kernel(x, w_in, conv_w, conv_b, mlstm_i_bias, mlstm_f_bias, mlstm_norm_g, w_branch_attn,
           w_branch_mlstm, w_out, ln1_g, ln1_b, w_router, b_router, w_gate_up, b_gate_up,
           w_down, b_down, ln2_g, ln2_b, rel_bias):
    B, S, D = x.shape
    x2 = x.reshape(B * S, D)
    for l in range(w_in.shape[0]):
        x2 = _layer(x2, B, S, w_in[l], conv_w[l], conv_b[l], mlstm_i_bias[l], mlstm_f_bias[l],
                    mlstm_norm_g[l], w_branch_attn[l], w_branch_mlstm[l], w_out[l], ln1_g[l], ln1_b[l],
                    w_router[l], b_router[l], w_gate_up[l], b_gate_up[l], w_down[l], b_down[l],
                    ln2_g[l], ln2_b[l], rel_bias)
    return x2.reshape(B, S, D)
```

```python
import functools
import math

import jax
import jax.numpy as jnp
from jax import lax
from jax.experimental import pallas as pl
from jax.experimental.pallas import tpu as pltpu

ATT_HEADS = 8
ATT_HEAD_DIM = 64
ATT_WIDTH = ATT_HEADS * ATT_HEAD_DIM
IDX_HEADS = 8
IDX_HEAD_DIM = 64
TOPK_MAX = 256
MLSTM_HEADS = 4
MLSTM_QK_DIM = 64
MLSTM_V_DIM = 128
MLSTM_WIDTH = MLSTM_HEADS * MLSTM_V_DIM
CONV_WIDTH = 4
N_BUCKETS = 32
MAX_DISTANCE = 128
N_EXPERTS = 32
TOP_K = 4
SWIGLU_ALPHA = 1.702
SWIGLU_LIMIT = 7.0
LN_EPS = 1e-5
DEPTH = 1
DEEPNORM_ALPHA = (2 * DEPTH) ** 0.25

LANES = 128
SUBLANES = 8
VMEM_LIMIT_BYTES = 56 * 1024 * 1024

MXU_DTYPE = jnp.bfloat16

INT_MIN = -(2 ** 31)
NEG_BIG = -1e30

QK_M = MLSTM_HEADS * MLSTM_QK_DIM
COLS_A = 3 * ATT_WIDTH + IDX_HEADS * IDX_HEAD_DIM
COLS_S = LANES
COLS_M = 2 * QK_M + 2 * MLSTM_WIDTH
COLS_K = 2 * IDX_HEAD_DIM
S_KI = 0
S_WI = IDX_HEAD_DIM
S_IM = S_WI + IDX_HEADS
S_FM = S_IM + MLSTM_HEADS


def _cparams(sem):
    return pltpu.CompilerParams(dimension_semantics=sem, vmem_limit_bytes=VMEM_LIMIT_BYTES)


def _proj_kernel(x_ref, w_ref, oa_ref, os_ref, om_ref, og_ref, ok_ref, *, d_model):
    xb = x_ref[...].astype(MXU_DTYPE)
    step = 512

    def mm(lo, hi):
        return jnp.dot(xb, w_ref[:, lo:hi], preferred_element_type=jnp.float32)

    base = 0
    for j in range(0, COLS_A, step):
        oa_ref[:, j:j + step] = mm(base + j, base + j + step).astype(MXU_DTYPE)
    base += COLS_A
    os_ref[...] = mm(base, base + COLS_S)
    base += COLS_S
    for j in range(0, COLS_M, step):
        om_ref[:, j:j + step] = mm(base + j, base + j + step)
    base += COLS_M
    for j in range(0, 2 * d_model, step):
        og_ref[:, j:j + step] = mm(base + j, base + j + step)
    base += 2 * d_model
    ok_ref[...] = mm(base, base + COLS_K).astype(MXU_DTYPE)


def _project(x2, w_p, tm):
    T, D = x2.shape
    n_all = w_p.shape[1]
    return pl.pallas_call(
        functools.partial(_proj_kernel, d_model=D),
        grid=(T // tm,),
        in_specs=[pl.BlockSpec((tm, D), lambda i: (i, 0)),
                  pl.BlockSpec((D, n_all), lambda i: (0, 0))],
        out_specs=[pl.BlockSpec((tm, COLS_A), lambda i: (i, 0)),
                   pl.BlockSpec((tm, COLS_S), lambda i: (i, 0)),
                   pl.BlockSpec((tm, COLS_M), lambda i: (i, 0)),
                   pl.BlockSpec((tm, 2 * D), lambda i: (i, 0)),
                   pl.BlockSpec((tm, COLS_K), lambda i: (i, 0))],
        out_shape=[jax.ShapeDtypeStruct((T, COLS_A), MXU_DTYPE),
                   jax.ShapeDtypeStruct((T, COLS_S), jnp.float32),
                   jax.ShapeDtypeStruct((T, COLS_M), jnp.float32),
                   jax.ShapeDtypeStruct((T, 2 * D), jnp.float32),
                   jax.ShapeDtypeStruct((T, COLS_K), MXU_DTYPE)],
        compiler_params=_cparams(("parallel",)),
    )(x2, w_p)


def _bias_kernel(rb_ref, o_ref, *, tq):
    h = pl.program_id(0)
    s = lax.broadcasted_iota(jnp.int32, (tq, tq), 0)
    t = lax.broadcasted_iota(jnp.int32, (tq, tq), 1)
    max_exact = N_BUCKETS // 2
    for d in range(3):
        n = jnp.maximum(t - s + d * tq, 0)
        n_f = jnp.maximum(n, 1).astype(jnp.float32)
        large = max_exact + (jnp.log(n_f / max_exact) / math.log(MAX_DISTANCE / max_exact)
                             * (N_BUCKETS - max_exact)).astype(jnp.int32)
        large = jnp.minimum(large, N_BUCKETS - 1)
        bucket = jnp.where(n < max_exact, n, large)
        acc = jnp.zeros((tq, tq), jnp.float32)
        for k in range(N_BUCKETS):
            acc = jnp.where(bucket == k, rb_ref[h, k], acc)
        o_ref[0, d] = acc


def _bias_tiles(rel_bias, tq):
    assert tq + 1 >= MAX_DISTANCE
    H = rel_bias.shape[0]
    return pl.pallas_call(
        functools.partial(_bias_kernel, tq=tq),
        grid=(H,),
        in_specs=[pl.BlockSpec(memory_space=pltpu.SMEM)],
        out_specs=pl.BlockSpec((1, 3, tq, tq), lambda h: (h, 0, 0, 0)),
        out_shape=jax.ShapeDtypeStruct((H, 3, tq, tq), jnp.float32),
        compiler_params=_cparams(("parallel",)),
    )(rel_bias)


def _dsa_kernel(qa_ref, ka_ref, vt_ref, qi_ref, kk_ref, wq_ref, bias_ref, o_ref,
                keys_ref, msk_ref, qz_ref, qiz_ref, acc_ref, m_ref, l_ref, *, tq, n_sel):
    i = pl.program_id(1)
    nch = i + 1
    t_pos = i * tq + lax.broadcasted_iota(jnp.int32, (1, tq), 1)
    s_loc = lax.broadcasted_iota(jnp.int32, (tq, 1), 0)
    hd = ATT_HEAD_DIM
    n_pairs = ATT_HEADS // 2
    assert 2 * hd == LANES and IDX_HEAD_DIM == hd and IDX_HEADS == ATT_HEADS

    top = lax.broadcasted_iota(jnp.int32, (LANES, tq), 0) < hd

    def pair_operand(blk):
        bt = blk.astype(jnp.float32).T
        return jnp.concatenate([jnp.where(top, bt, 0.0), jnp.where(top, 0.0, bt)],
                               axis=1).astype(MXU_DTYPE)

    for p in range(n_pairs):
        pair = slice(p * LANES, (p + 1) * LANES)
        qz_ref[p] = pair_operand(qa_ref[:, pair] * (hd ** -0.5))
        qiz_ref[p] = pair_operand(qi_ref[:, pair])

    ws = wq_ref[...].T[S_WI:S_WI + IDX_HEADS, :] * (IDX_HEADS ** -0.5)

    def score_body(c, carry):
        off = pl.multiple_of(c * tq, tq)
        kk = kk_ref[pl.ds(off, tq), :]
        sc = jnp.zeros((tq, tq), jnp.float32)
        for p in range(n_pairs):
            d = jnp.dot(kk, qiz_ref[p], preferred_element_type=jnp.float32)
            for j in range(2):
                h = 2 * p + j
                sc = sc + ws[h:h + 1, :] * jnp.maximum(d[:, j * tq:(j + 1) * tq], 0.0)
        sc = sc + 0.0
        b = pltpu.bitcast(sc, jnp.int32)
        sk = b ^ ((b >> 31) & jnp.int32(0x7FFFFFFF))
        keys_ref[c] = jnp.where(off + s_loc <= t_pos, sk, jnp.int32(INT_MIN))
        return carry

    lax.fori_loop(0, nch, score_body, 0)

    def count(pred_fn):
        def cnt_body(c, acc):
            hit = jnp.where(pred_fn(keys_ref[c]), 1.0, 0.0)
            return acc + jnp.sum(hit.reshape(tq // SUBLANES, SUBLANES, tq), axis=0)
        acc = lax.fori_loop(0, nch, cnt_body, jnp.zeros((SUBLANES, tq), jnp.float32))
        return jnp.sum(acc, axis=0, keepdims=True)

    def bit_body(it, res):
        cand = res ^ lax.shift_left(jnp.int32(1), 31 - it)
        cnt = count(lambda kc: kc >= cand)
        return jnp.where(cnt >= n_sel, cand, res)

    thr = lax.fori_loop(0, 32, bit_body, jnp.full((1, tq), INT_MIN, jnp.int32))
    n_gt = count(lambda kc: kc > thr)
    need = jnp.where(thr == INT_MIN, 0.0, n_sel - n_gt)

    tri = jnp.where(lax.broadcasted_iota(jnp.int32, (tq, tq), 0)
                    >= lax.broadcasted_iota(jnp.int32, (tq, tq), 1), 1.0, 0.0).astype(MXU_DTYPE)

    def mask_body(c, run):
        kc = keys_ref[c]
        eq = kc == thr
        eqf = jnp.where(eq, 1.0, 0.0)
        prefix = jnp.dot(tri, eqf.astype(MXU_DTYPE), preferred_element_type=jnp.float32) + run
        msk_ref[c] = jnp.where(kc > thr, 0.0,
                               jnp.where(eq, jnp.where(prefix <= need, 0.0, NEG_BIG), NEG_BIG))
        return run + jnp.sum(eqf, axis=0, keepdims=True)

    lax.fori_loop(0, nch, mask_body, jnp.zeros((1, tq), jnp.float32))

    m_ref[...] = jnp.full(m_ref.shape, NEG_BIG, jnp.float32)
    l_ref[...] = jnp.zeros(l_ref.shape, jnp.float32)
    acc_ref[...] = jnp.zeros(acc_ref.shape, jnp.float32)

    def att_body(c, carry):
        off = pl.multiple_of(c * tq, tq)
        bsel = jnp.minimum(i - c, 2)
        mk = msk_ref[c]
        for p in range(n_pairs):
            pair = slice(p * LANES, (p + 1) * LANES)
            s2 = jnp.dot(ka_ref[pl.ds(off, tq), pair], qz_ref[p], preferred_element_type=jnp.float32)
            ps, alphas = [], []
            for j in range(2):
                h = 2 * p + j
                s = s2[:, j * tq:(j + 1) * tq] + bias_ref[h, bsel] + mk
                m_prev = m_ref[h:h + 1, :]
                m_new = jnp.maximum(m_prev, jnp.max(s, axis=0, keepdims=True))
                alpha = jnp.exp(m_prev - m_new)
                pe = jnp.exp(s - m_new)
                l_ref[h:h + 1, :] = alpha * l_ref[h:h + 1, :] + jnp.sum(pe, axis=0, keepdims=True)
                m_ref[h:h + 1, :] = m_new
                ps.append(pe.astype(MXU_DTYPE))
                alphas.append(alpha)
            pv = jnp.dot(vt_ref[c, pair, :], jnp.concatenate(ps, axis=1),
                         preferred_element_type=jnp.float32)
            for j in range(2):
                h = 2 * p + j
                acc_ref[h] = alphas[j] * acc_ref[h] + pv[j * hd:(j + 1) * hd, j * tq:(j + 1) * tq]
        return carry

    lax.fori_loop(0, nch, att_body, 0)
    y_t = jnp.concatenate([acc_ref[h] / l_ref[h:h + 1, :] for h in range(ATT_HEADS)], axis=0)
    o_ref[...] = y_t.T.astype(o_ref.dtype)


def _dsa_attention(oa, osm, okk, bias_t, B, S, tq):
    nq = S // tq
    n_sel = min(TOPK_MAX, S // 4)
    W = ATT_WIDTH
    v_t = oa[:, 2 * W:3 * W].reshape(B * nq, tq, W).transpose(0, 2, 1)
    return pl.pallas_call(
        functools.partial(_dsa_kernel, tq=tq, n_sel=n_sel),
        grid=(B, nq),
        in_specs=[pl.BlockSpec((tq, W), lambda b, i: (b * nq + i, 0)),
                  pl.BlockSpec((S, W), lambda b, i: (b, 1)),
                  pl.BlockSpec((nq, W, tq), lambda b, i: (b, 0, 0)),
                  pl.BlockSpec((tq, W), lambda b, i: (b * nq + i, 3)),
                  pl.BlockSpec((S, COLS_K), lambda b, i: (b, 0)),
                  pl.BlockSpec((tq, COLS_S), lambda b, i: (b * nq + i, 0)),
                  pl.BlockSpec((ATT_HEADS, 3, tq, tq), lambda b, i: (0, 0, 0, 0))],
        out_specs=pl.BlockSpec((tq, W), lambda b, i: (b * nq + i, 0)),
        out_shape=jax.ShapeDtypeStruct((B * S, W), MXU_DTYPE),
        scratch_shapes=[pltpu.VMEM((nq, tq, tq), jnp.int32),
                        pltpu.VMEM((nq, tq, tq), jnp.float32),
                        pltpu.VMEM((ATT_HEADS // 2, LANES, 2 * tq), MXU_DTYPE),
                        pltpu.VMEM((IDX_HEADS // 2, LANES, 2 * tq), MXU_DTYPE),
                        pltpu.VMEM((ATT_HEADS, ATT_HEAD_DIM, tq), jnp.float32),
                        pltpu.VMEM((ATT_HEADS, tq), jnp.float32),
                        pltpu.VMEM((ATT_HEADS, tq), jnp.float32)],
        compiler_params=_cparams(("parallel", "arbitrary")),
    )(oa, oa, v_t, oa, okk, osm, bias_t)


def _mlstm_kernel(qk_ref, v_ref, og_ref, s_ref, cw_ref, cb_ref, gb_ref, ng_ref, y_ref,
                  xbuf, cst, mst, *, L):
    c = pl.program_id(1)
    H, DK, DV = MLSTM_HEADS, MLSTM_QK_DIM, MLSTM_V_DIM
    tail = SUBLANES

    @pl.when(c == 0)
    def _():
        xbuf[0:tail, :] = jnp.zeros((tail, 2 * QK_M), jnp.float32)
        cst[...] = jnp.zeros(cst.shape, jnp.float32)
        mst[...] = jnp.zeros(mst.shape, jnp.float32)

    @pl.when(c > 0)
    def _():
        xbuf[0:tail, :] = xbuf[L:L + tail, :]

    xbuf[tail:tail + L, :] = qk_ref[...]
    conv = cb_ref[...] + jnp.zeros((L, 2 * QK_M), jnp.float32)
    for j in range(CONV_WIDTH):
        conv = conv + cw_ref[j:j + 1, :] * xbuf[pl.ds(tail - (CONV_WIDTH - 1) + j, L), :]
    qk = conv * jax.nn.sigmoid(conv)
    q = qk[:, :QK_M]
    kT = (qk[:, QK_M:] * (DK ** -0.5)).T

    g = s_ref[...] + gb_ref[...]
    logf = jnp.minimum(g, 0.0) - jnp.log(1.0 + jnp.exp(-jnp.abs(g)))
    row = lax.broadcasted_iota(jnp.int32, (L, L), 0)
    col = lax.broadcasted_iota(jnp.int32, (L, L), 1)
    causal = row >= col
    bcum = jnp.dot(jnp.where(causal, 1.0, 0.0), logf, precision=lax.Precision.HIGHEST,
                   preferred_element_type=jnp.float32)
    gT = g.T
    bT = bcum.T
    ones_col = jnp.where(lax.broadcasted_iota(jnp.int32, (L, LANES), 1) == 0, 1.0, 0.0
                         ).astype(MXU_DTYPE)

    for h in range(H):
        bc = bcum[:, S_FM + h:S_FM + h + 1]
        br = bT[S_FM + h:S_FM + h + 1, :]
        lir = gT[S_IM + h:S_IM + h + 1, :]
        m0 = mst[h:h + 1, 0:1]
        d = jnp.where(causal, bc - br + lir, -jnp.inf)
        m_inter = bc + m0
        m_t = jnp.maximum(m_inter, jnp.max(d, axis=1, keepdims=True))
        qh = q[:, h * DK:(h + 1) * DK].astype(MXU_DTYPE)
        kTh = kT[h * DK:(h + 1) * DK, :]
        p = jnp.dot(qh, kTh.astype(MXU_DTYPE), preferred_element_type=jnp.float32) * jnp.exp(d - m_t)
        sc = jnp.exp(m_inter - m_t)
        vh = v_ref[:, h * DV:(h + 1) * DV].astype(MXU_DTYPE)
        caug = cst[h]
        inter = jnp.dot(qh, caug.astype(MXU_DTYPE), preferred_element_type=jnp.float32)
        num = jnp.dot(p.astype(MXU_DTYPE), vh, preferred_element_type=jnp.float32) + sc * inter[:, :DV]
        den = jnp.sum(p, axis=1, keepdims=True) + sc * inter[:, DV:DV + 1]
        hh = num / jnp.maximum(jnp.abs(den), jnp.exp(-m_t))
        b_end = bc[L - 1:L, :]
        a = b_end - br + lir
        m_loc = jnp.max(a, axis=1, keepdims=True)
        m_new = jnp.maximum(b_end + m0, m_loc)
        kTw = (kTh * jnp.exp(a - m_new)).astype(MXU_DTYPE)
        vaug = jnp.concatenate([vh, ones_col], axis=1)
        cst[h] = jnp.exp(b_end + m0 - m_new) * caug + jnp.dot(kTw, vaug, preferred_element_type=jnp.float32)
        mst[h:h + 1, :] = jnp.broadcast_to(m_new, (1, LANES))
        mu = jnp.mean(hh, axis=1, keepdims=True)
        hc = hh - mu
        var = jnp.mean(hc * hc, axis=1, keepdims=True)
        hn = hc * lax.rsqrt(var + LN_EPS) * ng_ref[:, h * DV:(h + 1) * DV]
        y_ref[:, h * DV:(h + 1) * DV] = (hn * jax.nn.sigmoid(og_ref[:, h * DV:(h + 1) * DV])).astype(y_ref.dtype)


def _mlstm(om, osm, conv_w, conv_b, gate_bias, norm_g, B, S, L):
    nc = S // L
    W = MLSTM_WIDTH
    assert 2 * QK_M == W
    return pl.pallas_call(
        functools.partial(_mlstm_kernel, L=L),
        grid=(B, nc),
        in_specs=[pl.BlockSpec((L, W), lambda b, c: (b * nc + c, 0)),
                  pl.BlockSpec((L, W), lambda b, c: (b * nc + c, 1)),
                  pl.BlockSpec((L, W), lambda b, c: (b * nc + c, 2)),
                  pl.BlockSpec((L, COLS_S), lambda b, c: (b * nc + c, 0)),
                  pl.BlockSpec((CONV_WIDTH, W), lambda b, c: (0, 0)),
                  pl.BlockSpec((1, W), lambda b, c: (0, 0)),
                  pl.BlockSpec((1, COLS_S), lambda b, c: (0, 0)),
                  pl.BlockSpec((1, W), lambda b, c: (0, 0))],
        out_specs=pl.BlockSpec((L, W), lambda b, c: (b * nc + c, 0)),
        out_shape=jax.ShapeDtypeStruct((B * S, W), MXU_DTYPE),
        scratch_shapes=[pltpu.VMEM((L + 2 * SUBLANES, W), jnp.float32),
                        pltpu.VMEM((MLSTM_HEADS, MLSTM_QK_DIM, 2 * LANES), jnp.float32),
                        pltpu.VMEM((SUBLANES, LANES), jnp.float32)],
        compiler_params=_cparams(("parallel", "arbitrary")),
    )(om, om, om, osm, conv_w, conv_b, gate_bias, norm_g)


def _layer_norm(z, g, b):
    mu = jnp.mean(z, axis=1, keepdims=True)
    zc = z - mu
    var = jnp.mean(zc * zc, axis=1, keepdims=True)
    return zc * lax.rsqrt(var + LN_EPS) * g + b


def _merge_kernel(ya_ref, ym_ref, g_ref, x_ref, wa_ref, wm_ref, wo_ref, lg_ref, lb_ref,
                  wr_ref, br_ref, x1_ref, xp_ref, idx_ref, gate_ref, *, d_model):
    D = d_model
    mix = (jax.nn.sigmoid(g_ref[:, :D]) * jnp.dot(ya_ref[...], wa_ref[...], preferred_element_type=jnp.float32)
           + jax.nn.sigmoid(g_ref[:, D:]) * jnp.dot(ym_ref[...], wm_ref[...], preferred_element_type=jnp.float32))
    y = jnp.dot(mix.astype(MXU_DTYPE), wo_ref[...], preferred_element_type=jnp.float32)
    x1 = _layer_norm(DEEPNORM_ALPHA * x_ref[...] + y, lg_ref[...], lb_ref[...])
    x1_ref[...] = x1
    xb = x1.astype(MXU_DTYPE)
    bits = pltpu.bitcast(x1.astype(jnp.bfloat16).astype(jnp.float32), jnp.uint32)
    xp_ref[...] = (bits[:, :D // 2] & jnp.uint32(0xFFFF0000)) | (bits[:, D // 2:] >> 16)

    logits = jnp.dot(xb, wr_ref[...], preferred_element_type=jnp.float32) + br_ref[...]
    tm = logits.shape[0]
    lane = lax.broadcasted_iota(jnp.int32, (tm, LANES), 1)
    lane_f = lane.astype(jnp.float32)
    vals, idxs = [], []
    for _ in range(TOP_K):
        mx = jnp.max(logits, axis=1, keepdims=True)
        ix = jnp.min(jnp.where(logits == mx, lane_f, float(LANES)), axis=1, keepdims=True)
        vals.append(mx)
        idxs.append(ix)
        logits = jnp.where(lane_f == ix, -jnp.inf, logits)
    es = [jnp.exp(v - vals[0]) for v in vals]
    tot = es[0]
    for e in es[1:]:
        tot = tot + e
    idx_out = jnp.zeros((tm, LANES), jnp.float32)
    gate_out = jnp.zeros((tm, LANES), jnp.float32)
    for k in range(TOP_K):
        idx_out = jnp.where(lane == k, idxs[k], idx_out)
        gate_out = jnp.where(lane == k, es[k] / tot, gate_out)
    idx_ref[...] = idx_out.astype(jnp.int32)
    gate_ref[...] = gate_out


def _merge(y_att, y_m, og, x2, wa, wm, wo, ln_g, ln_b, wr, br, tm):
    T, D = x2.shape
    full = lambda shape: pl.BlockSpec(shape, lambda i: (0, 0))
    return pl.pallas_call(
        functools.partial(_merge_kernel, d_model=D),
        grid=(T // tm,),
        in_specs=[pl.BlockSpec((tm, ATT_WIDTH), lambda i: (i, 0)),
                  pl.BlockSpec((tm, MLSTM_WIDTH), lambda i: (i, 0)),
                  pl.BlockSpec((tm, 2 * D), lambda i: (i, 0)),
                  pl.BlockSpec((tm, D), lambda i: (i, 0)),
                  full(wa.shape), full(wm.shape), full(wo.shape),
                  full((1, D)), full((1, D)), full(wr.shape), full((1, LANES))],
        out_specs=[pl.BlockSpec((tm, D), lambda i: (i, 0)),
                   pl.BlockSpec((tm, D // 2), lambda i: (i, 0)),
                   pl.BlockSpec((tm, LANES), lambda i: (i, 0)),
                   pl.BlockSpec((tm, LANES), lambda i: (i, 0))],
        out_shape=[jax.ShapeDtypeStruct((T, D), jnp.float32),
                   jax.ShapeDtypeStruct((T, D // 2), jnp.uint32),
                   jax.ShapeDtypeStruct((T, LANES), jnp.int32),
                   jax.ShapeDtypeStruct((T, LANES), jnp.float32)],
        compiler_params=_cparams(("parallel",)),
    )(y_att, y_m, og, x2, wa, wm, wo, ln_g, ln_b, wr, br)


def _rank_kernel(idx_ref, rank_ref, cnt_ref, carry_ref):
    @pl.when(pl.program_id(0) == 0)
    def _():
        carry_ref[...] = jnp.zeros(carry_ref.shape, jnp.float32)

    tm = idx_ref.shape[0]
    lane = lax.broadcasted_iota(jnp.int32, (tm, LANES), 1)
    idx = idx_ref[...]
    hots = [lane == idx[:, k:k + 1] for k in range(TOP_K)]
    c = jnp.zeros((tm, LANES), jnp.float32)
    for hot in hots:
        c = c + jnp.where(hot, 1.0, 0.0)
    before = jnp.where(lax.broadcasted_iota(jnp.int32, (tm, tm), 0)
                       > lax.broadcasted_iota(jnp.int32, (tm, tm), 1), 1.0, 0.0).astype(MXU_DTYPE)
    carry = carry_ref[0:1, :]
    prior = jnp.dot(before, c.astype(MXU_DTYPE), preferred_element_type=jnp.float32) + carry
    out = jnp.zeros((tm, LANES), jnp.float32)
    for k, hot in enumerate(hots):
        out = jnp.where(lane == k, jnp.sum(jnp.where(hot, prior, 0.0), axis=1, keepdims=True), out)
    rank_ref[...] = out.astype(jnp.int32)
    total = carry + jnp.sum(c, axis=0, keepdims=True)
    carry_ref[...] = jnp.broadcast_to(total, carry_ref.shape)
    cnt_ref[...] = jnp.broadcast_to(total, cnt_ref.shape)


def _ranks(idx, tm):
    T = idx.shape[0]
    return pl.pallas_call(
        _rank_kernel,
        grid=(T // tm,),
        in_specs=[pl.BlockSpec((tm, LANES), lambda i: (i, 0))],
        out_specs=[pl.BlockSpec((tm, LANES), lambda i: (i, 0)),
                   pl.BlockSpec((SUBLANES, LANES), lambda i: (0, 0))],
        out_shape=[jax.ShapeDtypeStruct((T, LANES), jnp.int32),
                   jax.ShapeDtypeStruct((SUBLANES, LANES), jnp.float32)],
        scratch_shapes=[pltpu.VMEM((SUBLANES, LANES), jnp.float32)],
        compiler_params=_cparams(("arbitrary",)),
    )(idx)


def _row_copy(src_ref, s, dst_ref, d, sem):
    return pltpu.make_async_copy(src_ref.at[pl.ds(s, 1), :], dst_ref.at[pl.ds(d, 1), :], sem)


def _dispatch_kernel(dest_ref, x_ref, xs_in_ref, xs_ref, sem):
    del xs_in_ref
    tm = x_ref.shape[0]

    def start(t, carry):
        for k in range(TOP_K):
            _row_copy(x_ref, t, xs_ref, dest_ref[t * TOP_K + k], sem).start()
        return carry

    lax.fori_loop(0, tm, start, 0)

    def wait(t, carry):
        for k in range(TOP_K):
            _row_copy(x_ref, 0, xs_ref, 0, sem).wait()
        return carry

    lax.fori_loop(0, tm, wait, 0)


def _dispatch(dest_flat, xp, cap, tm):
    T, W = xp.shape
    xs0 = jnp.zeros((cap, W), xp.dtype)
    return pl.pallas_call(
        _dispatch_kernel,
        grid=(T // tm,),
        in_specs=[pl.BlockSpec((tm * TOP_K,), lambda i: (i,), memory_space=pltpu.SMEM),
                  pl.BlockSpec((tm, W), lambda i: (i, 0)),
                  pl.BlockSpec(memory_space=pl.ANY)],
        out_specs=pl.BlockSpec(memory_space=pl.ANY),
        out_shape=jax.ShapeDtypeStruct((cap, W), xp.dtype),
        scratch_shapes=[pltpu.SemaphoreType.DMA],
        input_output_aliases={2: 0},
        compiler_params=_cparams(("arbitrary",)),
    )(dest_flat, xp, xs0)


def _ffn_kernel(be_ref, nu_ref, xs_ref, wgu_ref, bgu_ref, wd_ref, bd_ref, y_ref, wgu_b, wd_b,
                *, d_ff):
    r = pl.program_id(0)
    e = be_ref[r]
    prev = be_ref[jnp.maximum(r - 1, 0)]

    @pl.when((r == 0) | (e != prev))
    def _():
        wgu_b[...] = wgu_ref[0].astype(MXU_DTYPE)
        wd_b[...] = wd_ref[0].astype(MXU_DTYPE)

    @pl.when(r < nu_ref[0])
    def _():
        w = xs_ref[...]
        half = w.shape[1]
        x_hi = pltpu.bitcast(w & jnp.uint32(0xFFFF0000), jnp.float32).astype(MXU_DTYPE)
        x_lo = pltpu.bitcast(w << 16, jnp.float32).astype(MXU_DTYPE)
        step = 512
        acc = bd_ref[0] + jnp.zeros(y_ref.shape, jnp.float32)
        for j in range(0, d_ff, step):
            def gu(lo):
                return (jnp.dot(x_hi, wgu_b[0:half, lo:lo + step], preferred_element_type=jnp.float32)
                        + jnp.dot(x_lo, wgu_b[half:2 * half, lo:lo + step], preferred_element_type=jnp.float32)
                        + bgu_ref[0, :, lo:lo + step])
            gate = jnp.minimum(gu(j), SWIGLU_LIMIT)
            up = jnp.clip(gu(d_ff + j), -SWIGLU_LIMIT, SWIGLU_LIMIT)
            act = (up + 1.0) * (gate * jax.nn.sigmoid(SWIGLU_ALPHA * gate))
            acc = acc + jnp.dot(act.astype(MXU_DTYPE), wd_b[j:j + step, :],
                                preferred_element_type=jnp.float32)
        y_ref[...] = acc

    @pl.when(r >= nu_ref[0])
    def _():
        y_ref[...] = jnp.zeros(y_ref.shape, jnp.float32)


def _expert_ffn(block_expert, n_used, xs, w_gate_up, b_gate_up, w_down, b_down, bm):
    cap, half = xs.shape
    E, D, F2 = w_gate_up.shape
    d_ff = F2 // 2
    grid_spec = pltpu.PrefetchScalarGridSpec(
        num_scalar_prefetch=2,
        grid=(cap // bm,),
        in_specs=[pl.BlockSpec((bm, half), lambda r, be, nu: (r, 0)),
                  pl.BlockSpec((1, D, F2), lambda r, be, nu: (be[r], 0, 0)),
                  pl.BlockSpec((1, 1, F2), lambda r, be, nu: (be[r], 0, 0)),
                  pl.BlockSpec((1, d_ff, D), lambda r, be, nu: (be[r], 0, 0)),
                  pl.BlockSpec((1, 1, D), lambda r, be, nu: (be[r], 0, 0))],
        out_specs=pl.BlockSpec((bm, D), lambda r, be, nu: (r, 0)),
        scratch_shapes=[pltpu.VMEM((D, F2), MXU_DTYPE),
                        pltpu.VMEM((d_ff, D), MXU_DTYPE)],
    )
    return pl.pallas_call(
        functools.partial(_ffn_kernel, d_ff=d_ff),
        grid_spec=grid_spec,
        out_shape=jax.ShapeDtypeStruct((cap, D), jnp.float32),
        compiler_params=_cparams(("arbitrary",)),
    )(block_expert, n_used, xs, w_gate_up, b_gate_up.reshape(E, 1, F2), w_down,
      b_down.reshape(E, 1, D))


def _combine_kernel(dest_ref, gate_ref, x1_ref, yb_ref, lg_ref, lb_ref, o_ref, buf, sem):
    tm = x1_ref.shape[0]

    def start(t, carry):
        for k in range(TOP_K):
            _row_copy(yb_ref, dest_ref[t * TOP_K + k], buf.at[k], t, sem).start()
        return carry

    lax.fori_loop(0, tm, start, 0)

    def wait(t, carry):
        for k in range(TOP_K):
            _row_copy(yb_ref, 0, buf.at[k], 0, sem).wait()
        return carry

    lax.fori_loop(0, tm, wait, 0)
    y = gate_ref[:, 0:1] * buf[0]
    for k in range(1, TOP_K):
        y = y + gate_ref[:, k:k + 1] * buf[k]
    o_ref[...] = _layer_norm(DEEPNORM_ALPHA * x1_ref[...] + y, lg_ref[...], lb_ref[...])


def _combine(dest_flat, gates, x1, ybuf, ln_g, ln_b, tm):
    T, D = x1.shape
    return pl.pallas_call(
        _combine_kernel,
        grid=(T // tm,),
        in_specs=[pl.BlockSpec((tm * TOP_K,), lambda i: (i,), memory_space=pltpu.SMEM),
                  pl.BlockSpec((tm, LANES), lambda i: (i, 0)),
                  pl.BlockSpec((tm, D), lambda i: (i, 0)),
                  pl.BlockSpec(memory_space=pl.ANY),
                  pl.BlockSpec((1, D), lambda i: (0, 0)),
                  pl.BlockSpec((1, D), lambda i: (0, 0))],
        out_specs=pl.BlockSpec((tm, D), lambda i: (i, 0)),
        out_shape=jax.ShapeDtypeStruct((T, D), jnp.float32),
        scratch_shapes=[pltpu.VMEM((TOP_K, tm, D), jnp.float32), pltpu.SemaphoreType.DMA],
        compiler_params=_cparams(("arbitrary",)),
    )(dest_flat, gates, x1, ybuf, ln_g, ln_b)


def _tile(n, pref):
    t = min(n, pref)
    assert n % t == 0
    return t


def _relayout_w_in(w_in, d_model):
    sizes = (ATT_WIDTH, ATT_WIDTH, ATT_WIDTH, IDX_HEADS * IDX_HEAD_DIM, IDX_HEAD_DIM, IDX_HEADS,
             QK_M, QK_M, MLSTM_WIDTH, MLSTM_HEADS, MLSTM_HEADS, MLSTM_WIDTH, d_model, d_model)
    offs = [0]
    for s in sizes:
        offs.append(offs[-1] + s)
    seg = lambda k: w_in[:, offs[k]:offs[k + 1]]
    (q_a, k_a, v_a, q_i, k_i, w_i, q_m, k_m, v_m, i_m, f_m, o_m, g_a, g_m) = [seg(k) for k in range(14)]
    pad = jnp.zeros((w_in.shape[0], COLS_S - (IDX_HEAD_DIM + IDX_HEADS + 2 * MLSTM_HEADS)), w_in.dtype)
    cols = [q_a, k_a, v_a, q_i, k_i, w_i, i_m, f_m, pad, q_m, k_m, v_m, o_m, g_a, g_m, k_i, k_i]
    return jnp.concatenate(cols, axis=1).astype(MXU_DTYPE)


def _layer(x2, B, S, w_in, conv_w, conv_b, i_bias, f_bias, norm_g, w_branch_attn, w_branch_mlstm,
           w_out, ln1_g, ln1_b, w_router, b_router, w_gate_up, b_gate_up, w_down, b_down,
           ln2_g, ln2_b, rel_bias):
    T, D = x2.shape
    bf = MXU_DTYPE
    tq = _tile(S, 256)
    L = _tile(S, 256)
    tm = _tile(T, 256)

    oa, osm, om, og, okk = _project(x2, _relayout_w_in(w_in, D), tm)
    bias_t = _bias_tiles(rel_bias, tq)
    y_att = _dsa_attention(oa, osm, okk, bias_t, B, S, tq)

    gate_bias = jnp.zeros((1, COLS_S), jnp.float32)
    gate_bias = gate_bias.at[0, S_IM:S_IM + MLSTM_HEADS].set(i_bias)
    gate_bias = gate_bias.at[0, S_FM:S_FM + MLSTM_HEADS].set(f_bias)
    y_m = _mlstm(om, osm, conv_w, conv_b.reshape(1, -1), gate_bias, norm_g.reshape(1, -1), B, S, L)

    wr = jnp.zeros((D, LANES), bf).at[:, :N_EXPERTS].set(w_router.astype(bf))
    br = jnp.full((1, LANES), NEG_BIG, jnp.float32).at[0, :N_EXPERTS].set(b_router)
    x1, xp, idx, gates = _merge(y_att, y_m, og, x2, w_branch_attn.astype(bf), w_branch_mlstm.astype(bf),
                                w_out.astype(bf), ln1_g.reshape(1, D), ln1_b.reshape(1, D), wr, br, tm)

    bm = 512
    rank, cnt = _ranks(idx, _tile(T, 512))
    counts = cnt[0, :N_EXPERTS].astype(jnp.int32)
    padded = ((counts + bm - 1) // bm) * bm
    pend = jnp.cumsum(padded)
    pstart = pend - padded
    cap = ((T * TOP_K + bm - 1) // bm) * bm + N_EXPERTS * bm
    n_blocks = cap // bm
    dest = (pstart[idx[:, :TOP_K]] + rank[:, :TOP_K]).reshape(T * TOP_K)
    block_expert = jnp.minimum(
        jnp.searchsorted(pend, jnp.arange(n_blocks, dtype=jnp.int32) * bm, side='right'),
        N_EXPERTS - 1).astype(jnp.int32)
    n_used = (pend[-1:] // bm).astype(jnp.int32)

    xs = _dispatch(dest, xp, cap, _tile(T, 512))
    ybuf = _expert_ffn(block_expert, n_used, xs, w_gate_up, b_gate_up, w_down, b_down, bm)
    return _combine(dest, gates, x1, ybuf, ln2_g.reshape(1, D), ln2_b.reshape(1, D), tm)


def kernel(x, w_in, conv_w, conv_b, mlstm_i_bias, mlstm_f_bias, mlstm_norm_g, w_branch_attn,
           w_branch_mlstm, w_out, ln1_g, ln1_b, w_router, b_router, w_gate_up, b_gate_up,
           w_down, b_down, ln2_g, ln2_b, rel_bias):
    B, S, D = x.shape
    x2 = x.reshape(B * S, D)
    for l in range(w_in.shape[0]):
        x2 = _layer(x2, B, S, w_in[l], conv_w[l], conv_b[l], mlstm_i_bias[l], mlstm_f_bias[l],
                    mlstm_norm_g[l], w_branch_attn[l], w_branch_mlstm[l], w_out[l], ln1_g[l], ln1_b[l],
                    w_router[l], b_router[l], w_gate_up[l], b_gate_up[l], w_down[l], b_down[l],
                    ln2_g[l], ln2_b[l], rel_bias)
    return x2.reshape(B, S, D)
```

```python
import functools
import math

import jax
import jax.numpy as jnp
from jax import lax
from jax.experimental import pallas as pl
from jax.experimental.pallas import tpu as pltpu

ATT_HEADS = 8
ATT_HEAD_DIM = 64
ATT_WIDTH = ATT_HEADS * ATT_HEAD_DIM
IDX_HEADS = 8
IDX_HEAD_DIM = 64
TOPK_MAX = 256
MLSTM_HEADS = 4
MLSTM_QK_DIM = 64
MLSTM_V_DIM = 128
MLSTM_WIDTH = MLSTM_HEADS * MLSTM_V_DIM
CONV_WIDTH = 4
N_BUCKETS = 32
MAX_DISTANCE = 128
N_EXPERTS = 32
TOP_K = 4
SWIGLU_ALPHA = 1.702
SWIGLU_LIMIT = 7.0
LN_EPS = 1e-5
DEPTH = 1
DEEPNORM_ALPHA = (2 * DEPTH) ** 0.25

LANES = 128
SUBLANES = 8
VMEM_LIMIT_BYTES = 56 * 1024 * 1024

MXU_DTYPE = jnp.bfloat16

INT_MIN = -(2 ** 31)
NEG_BIG = -1e30

QK_M = MLSTM_HEADS * MLSTM_QK_DIM
COLS_A = 3 * ATT_WIDTH + IDX_HEADS * IDX_HEAD_DIM
COLS_S = LANES
COLS_M = 2 * QK_M + 2 * MLSTM_WIDTH
COLS_K = 2 * IDX_HEAD_DIM
S_KI = 0
S_WI = IDX_HEAD_DIM
S_IM = S_WI + IDX_HEADS
S_FM = S_IM + MLSTM_HEADS


def _cparams(sem):
    return pltpu.CompilerParams(dimension_semantics=sem, vmem_limit_bytes=VMEM_LIMIT_BYTES)


def _proj_kernel(x_ref, w_ref, oa_ref, os_ref, om_ref, og_ref, ok_ref, *, d_model):
    xb = x_ref[...].astype(MXU_DTYPE)
    step = 512

    def mm(lo, hi):
        return jnp.dot(xb, w_ref[:, lo:hi], preferred_element_type=jnp.float32)

    base = 0
    for j in range(0, COLS_A, step):
        oa_ref[:, j:j + step] = mm(base + j, base + j + step).astype(MXU_DTYPE)
    base += COLS_A
    os_ref[...] = mm(base, base + COLS_S)
    base += COLS_S
    for j in range(0, COLS_M, step):
        om_ref[:, j:j + step] = mm(base + j, base + j + step)
    base += COLS_M
    for j in range(0, 2 * d_model, step):
        og_ref[:, j:j + step] = mm(base + j, base + j + step)
    base += 2 * d_model
    ok_ref[...] = mm(base, base + COLS_K).astype(MXU_DTYPE)


def _project(x2, w_p, tm):
    T, D = x2.shape
    n_all = w_p.shape[1]
    return pl.pallas_call(
        functools.partial(_proj_kernel, d_model=D),
        grid=(T // tm,),
        in_specs=[pl.BlockSpec((tm, D), lambda i: (i, 0)),
                  pl.BlockSpec((D, n_all), lambda i: (0, 0))],
        out_specs=[pl.BlockSpec((tm, COLS_A), lambda i: (i, 0)),
                   pl.BlockSpec((tm, COLS_S), lambda i: (i, 0)),
                   pl.BlockSpec((tm, COLS_M), lambda i: (i, 0)),
                   pl.BlockSpec((tm, 2 * D), lambda i: (i, 0)),
                   pl.BlockSpec((tm, COLS_K), lambda i: (i, 0))],
        out_shape=[jax.ShapeDtypeStruct((T, COLS_A), MXU_DTYPE),
                   jax.ShapeDtypeStruct((T, COLS_S), jnp.float32),
                   jax.ShapeDtypeStruct((T, COLS_M), jnp.float32),
                   jax.ShapeDtypeStruct((T, 2 * D), jnp.float32),
                   jax.ShapeDtypeStruct((T, COLS_K), MXU_DTYPE)],
        compiler_params=_cparams(("parallel",)),
    )(x2, w_p)


def _bias_kernel(rb_ref, o_ref, *, tq):
    h = pl.program_id(0)
    s = lax.broadcasted_iota(jnp.int32, (tq, tq), 0)
    t = lax.broadcasted_iota(jnp.int32, (tq, tq), 1)
    max_exact = N_BUCKETS // 2
    for d in range(3):
        n = jnp.maximum(t - s + d * tq, 0)
        n_f = jnp.maximum(n, 1).astype(jnp.float32)
        large = max_exact + (jnp.log(n_f / max_exact) / math.log(MAX_DISTANCE / max_exact)
                             * (N_BUCKETS - max_exact)).astype(jnp.int32)
        large = jnp.minimum(large, N_BUCKETS - 1)
        bucket = jnp.where(n < max_exact, n, large)
        acc = jnp.zeros((tq, tq), jnp.float32)
        for k in range(N_BUCKETS):
            acc = jnp.where(bucket == k, rb_ref[h, k], acc)
        o_ref[0, d] = acc


def _bias_tiles(rel_bias, tq):
    assert tq + 1 >= MAX_DISTANCE
    H = rel_bias.shape[0]
    return pl.pallas_call(
        functools.partial(_bias_kernel, tq=tq),
        grid=(H,),
        in_specs=[pl.BlockSpec(memory_space=pltpu.SMEM)],
        out_specs=pl.BlockSpec((1, 3, tq, tq), lambda h: (h, 0, 0, 0)),
        out_shape=jax.ShapeDtypeStruct((H, 3, tq, tq), jnp.float32),
        compiler_params=_cparams(("parallel",)),
    )(rel_bias)


def _dsa_kernel(qa_ref, ka_ref, vt_ref, qi_ref, kk_ref, wq_ref, bias_ref, o_ref,
                keys_ref, hi_ref, lo_ref, msk_ref, qz_ref, qiz_ref, s_ref, p_ref, acc_ref, m_ref, l_ref,
                *, tq, n_sel):
    i = pl.program_id(1)
    nch = i + 1
    t_pos = i * tq + lax.broadcasted_iota(jnp.int32, (1, tq), 1)
    s_loc = lax.broadcasted_iota(jnp.int32, (tq, 1), 0)
    hd = ATT_HEAD_DIM
    n_pairs = ATT_HEADS // 2
    assert 2 * hd == LANES and IDX_HEAD_DIM == hd and IDX_HEADS == ATT_HEADS

    top = lax.broadcasted_iota(jnp.int32, (LANES, tq), 0) < hd

    def pair_operand(blk):
        bt = blk.astype(jnp.float32).T
        return jnp.concatenate([jnp.where(top, bt, 0.0), jnp.where(top, 0.0, bt)],
                               axis=1).astype(MXU_DTYPE)

    for p in range(n_pairs):
        pair = slice(p * LANES, (p + 1) * LANES)
        qz_ref[p] = pair_operand(qa_ref[:, pair] * (hd ** -0.5))
        qiz_ref[p] = pair_operand(qi_ref[:, pair])

    ws = wq_ref[...].T[S_WI:S_WI + IDX_HEADS, :] * (IDX_HEADS ** -0.5)

    def score_body(c, carry):
        off = pl.multiple_of(c * tq, tq)
        kk = kk_ref[pl.ds(off, tq), :]
        sc = jnp.zeros((tq, tq), jnp.float32)
        for p in range(n_pairs):
            d = jnp.dot(kk, qiz_ref[p], preferred_element_type=jnp.float32)
            for j in range(2):
                h = 2 * p + j
                sc = sc + ws[h:h + 1, :] * jnp.maximum(d[:, j * tq:(j + 1) * tq], 0.0)
        sc = sc + 0.0
        b = pltpu.bitcast(sc, jnp.int32)
        sk = b ^ ((b >> 31) & jnp.int32(0x7FFFFFFF))
        sk = jnp.where(off + s_loc <= t_pos, sk, jnp.int32(INT_MIN))
        keys_ref[c] = sk
        hi_ref[c] = (sk >> 16).astype(jnp.int16)
        return carry

    lax.fori_loop(0, nch, score_body, 0)

    pack = 2 * SUBLANES
    half_min = -(2 ** 15)

    def count16(ref, pred_fn):
        def cnt_body(c, acc):
            hit = jnp.where(pred_fn(ref[c]), jnp.int16(1), jnp.int16(0))
            parts = [hit[r * pack:(r + 1) * pack, :] for r in range(tq // pack)]
            while len(parts) > 1:
                parts = [a + b for a, b in zip(parts[::2], parts[1::2])]
            return acc + parts[0]
        acc = lax.fori_loop(0, nch, cnt_body, jnp.zeros((pack, tq), jnp.int16))
        return jnp.sum(acc.astype(jnp.float32), axis=0, keepdims=True)

    def search16(ref, target):
        def bit_body(it, u):
            cand_u = u | lax.shift_left(jnp.int32(1), 15 - it)
            cand = (cand_u + half_min).astype(jnp.int16)
            return jnp.where(count16(ref, lambda x: x >= cand) >= target, cand_u, u)
        return lax.fori_loop(0, 16, bit_body, jnp.zeros((1, tq), jnp.int32))

    u_hi = search16(hi_ref, float(n_sel))
    thr_hi = (u_hi + half_min).astype(jnp.int16)
    need_lo = n_sel - count16(hi_ref, lambda x: x > thr_hi)

    def low_body(c, carry):
        lo = ((keys_ref[c] & 0xFFFF) + half_min).astype(jnp.int16)
        lo_ref[c] = jnp.where(hi_ref[c] == thr_hi, lo, jnp.int16(half_min))
        return carry

    lax.fori_loop(0, nch, low_body, 0)
    u_lo = search16(lo_ref, need_lo)
    thr_lo = (u_lo + half_min).astype(jnp.int16)
    thr = lax.shift_left(u_hi + half_min, 16) | u_lo
    need = jnp.where(thr == INT_MIN, 0.0, need_lo - count16(lo_ref, lambda x: x > thr_lo))

    tri = jnp.where(lax.broadcasted_iota(jnp.int32, (tq, tq), 0)
                    >= lax.broadcasted_iota(jnp.int32, (tq, tq), 1), 1.0, 0.0).astype(MXU_DTYPE)

    def mask_body(c, run):
        kc = keys_ref[c]
        eq = kc == thr
        eqf = jnp.where(eq, 1.0, 0.0)
        prefix = jnp.dot(tri, eqf.astype(MXU_DTYPE), preferred_element_type=jnp.float32) + run
        msk_ref[c] = jnp.where(kc > thr, 0.0,
                               jnp.where(eq, jnp.where(prefix <= need, 0.0, NEG_BIG), NEG_BIG))
        return run + jnp.sum(eqf, axis=0, keepdims=True)

    lax.fori_loop(0, nch, mask_body, jnp.zeros((1, tq), jnp.float32))

    m_ref[...] = jnp.full(m_ref.shape, NEG_BIG, jnp.float32)
    l_ref[...] = jnp.zeros(l_ref.shape, jnp.float32)
    acc_ref[...] = jnp.zeros(acc_ref.shape, jnp.float32)

    def att_body(c, carry):
        off = pl.multiple_of(c * tq, tq)
        bsel = jnp.minimum(i - c, 2)
        mk = msk_ref[c]
        tile_max = []
        for p in range(n_pairs):
            pair = slice(p * LANES, (p + 1) * LANES)
            s2 = jnp.dot(ka_ref[pl.ds(off, tq), pair], qz_ref[p], preferred_element_type=jnp.float32)
            for j in range(2):
                h = 2 * p + j
                s = s2[:, j * tq:(j + 1) * tq] + bias_ref[h, bsel] + mk
                s_ref[h] = s
                tile_max.append(jnp.max(s, axis=0, keepdims=True))
        alphas = []
        for h in range(ATT_HEADS):
            m_prev = m_ref[h:h + 1, :]
            m_new = jnp.maximum(m_prev, tile_max[h])
            alpha = jnp.exp(m_prev - m_new)
            pe = jnp.exp(s_ref[h] - m_new)
            l_ref[h:h + 1, :] = alpha * l_ref[h:h + 1, :] + jnp.sum(pe, axis=0, keepdims=True)
            m_ref[h:h + 1, :] = m_new
            p_ref[h // 2, :, (h % 2) * tq:(h % 2 + 1) * tq] = pe.astype(MXU_DTYPE)
            alphas.append(alpha)
        for p in range(n_pairs):
            pv = jnp.dot(vt_ref[c, p * LANES:(p + 1) * LANES, :], p_ref[p],
                         preferred_element_type=jnp.float32)
            for j in range(2):
                h = 2 * p + j
                acc_ref[h] = alphas[h] * acc_ref[h] + pv[j * hd:(j + 1) * hd, j * tq:(j + 1) * tq]
        return carry

    lax.fori_loop(0, nch, att_body, 0)
    y_t = jnp.concatenate([acc_ref[h] / l_ref[h:h + 1, :] for h in range(ATT_HEADS)], axis=0)
    o_ref[...] = y_t.T.astype(o_ref.dtype)


def _dsa_attention(oa, osm, okk, bias_t, B, S, tq):
    nq = S // tq
    n_sel = min(TOPK_MAX, S // 4)
    W = ATT_WIDTH
    v_t = oa[:, 2 * W:3 * W].reshape(B * nq, tq, W).transpose(0, 2, 1)
    return pl.pallas_call(
        functools.partial(_dsa_kernel, tq=tq, n_sel=n_sel),
        grid=(B, nq),
        in_specs=[pl.BlockSpec((tq, W), lambda b, i: (b * nq + i, 0)),
                  pl.BlockSpec((S, W), lambda b, i: (b, 1)),
                  pl.BlockSpec((nq, W, tq), lambda b, i: (b, 0, 0)),
                  pl.BlockSpec((tq, W), lambda b, i: (b * nq + i, 3)),
                  pl.BlockSpec((S, COLS_K), lambda b, i: (b, 0)),
                  pl.BlockSpec((tq, COLS_S), lambda b, i: (b * nq + i, 0)),
                  pl.BlockSpec((ATT_HEADS, 3, tq, tq), lambda b, i: (0, 0, 0, 0))],
        out_specs=pl.BlockSpec((tq, W), lambda b, i: (b * nq + i, 0)),
        out_shape=jax.ShapeDtypeStruct((B * S, W), MXU_DTYPE),
        scratch_shapes=[pltpu.VMEM((nq, tq, tq), jnp.int32),
                        pltpu.VMEM((nq, tq, tq), jnp.int16),
                        pltpu.VMEM((nq, tq, tq), jnp.int16),
                        pltpu.VMEM((nq, tq, tq), jnp.float32),
                        pltpu.VMEM((ATT_HEADS // 2, LANES, 2 * tq), MXU_DTYPE),
                        pltpu.VMEM((IDX_HEADS // 2, LANES, 2 * tq), MXU_DTYPE),
                        pltpu.VMEM((ATT_HEADS, tq, tq), jnp.float32),
                        pltpu.VMEM((ATT_HEADS // 2, tq, 2 * tq), MXU_DTYPE),
                        pltpu.VMEM((ATT_HEADS, ATT_HEAD_DIM, tq), jnp.float32),
                        pltpu.VMEM((ATT_HEADS, tq), jnp.float32),
                        pltpu.VMEM((ATT_HEADS, tq), jnp.float32)],
        compiler_params=_cparams(("parallel", "arbitrary")),
    )(oa, oa, v_t, oa, okk, osm, bias_t)


def _mlstm_kernel(qk_ref, v_ref, og_ref, s_ref, cw_ref, cb_ref, gb_ref, ng_ref, y_ref,
                  xbuf, cst, mst, *, L):
    c = pl.program_id(1)
    H, DK, DV = MLSTM_HEADS, MLSTM_QK_DIM, MLSTM_V_DIM
    tail = SUBLANES

    @pl.when(c == 0)
    def _():
        xbuf[0:tail, :] = jnp.zeros((tail, 2 * QK_M), jnp.float32)
        cst[...] = jnp.zeros(cst.shape, jnp.float32)
        mst[...] = jnp.zeros(mst.shape, jnp.float32)

    @pl.when(c > 0)
    def _():
        xbuf[0:tail, :] = xbuf[L:L + tail, :]

    xbuf[tail:tail + L, :] = qk_ref[...]
    conv = cb_ref[...] + jnp.zeros((L, 2 * QK_M), jnp.float32)
    for j in range(CONV_WIDTH):
        conv = conv + cw_ref[j:j + 1, :] * xbuf[pl.ds(tail - (CONV_WIDTH - 1) + j, L), :]
    qk = conv * jax.nn.sigmoid(conv)
    q = qk[:, :QK_M]
    kT = (qk[:, QK_M:] * (DK ** -0.5)).T

    g = s_ref[...] + gb_ref[...]
    logf = jnp.minimum(g, 0.0) - jnp.log(1.0 + jnp.exp(-jnp.abs(g)))
    row = lax.broadcasted_iota(jnp.int32, (L, L), 0)
    col = lax.broadcasted_iota(jnp.int32, (L, L), 1)
    causal = row >= col
    bcum = jnp.dot(jnp.where(causal, 1.0, 0.0), logf, precision=lax.Precision.HIGHEST,
                   preferred_element_type=jnp.float32)
    gT = g.T
    bT = bcum.T
    ones_col = jnp.where(lax.broadcasted_iota(jnp.int32, (L, LANES), 1) == 0, 1.0, 0.0
                         ).astype(MXU_DTYPE)

    for h in range(H):
        bc = bcum[:, S_FM + h:S_FM + h + 1]
        br = bT[S_FM + h:S_FM + h + 1, :]
        lir = gT[S_IM + h:S_IM + h + 1, :]
        m0 = mst[h:h + 1, 0:1]
        d = jnp.where(causal, bc - br + lir, -jnp.inf)
        m_inter = bc + m0
        m_t = jnp.maximum(m_inter, jnp.max(d, axis=1, keepdims=True))
        qh = q[:, h * DK:(h + 1) * DK].astype(MXU_DTYPE)
        kTh = kT[h * DK:(h + 1) * DK, :]
        p = jnp.dot(qh, kTh.astype(MXU_DTYPE), preferred_element_type=jnp.float32) * jnp.exp(d - m_t)
        sc = jnp.exp(m_inter - m_t)
        vh = v_ref[:, h * DV:(h + 1) * DV].astype(MXU_DTYPE)
        caug = cst[h]
        inter = jnp.dot(qh, caug.astype(MXU_DTYPE), preferred_element_type=jnp.float32)
        num = jnp.dot(p.astype(MXU_DTYPE), vh, preferred_element_type=jnp.float32) + sc * inter[:, :DV]
        den = jnp.sum(p, axis=1, keepdims=True) + sc * inter[:, DV:DV + 1]
        hh = num / jnp.maximum(jnp.abs(den), jnp.exp(-m_t))
        b_end = bc[L - 1:L, :]
        a = b_end - br + lir
        m_loc = jnp.max(a, axis=1, keepdims=True)
        m_new = jnp.maximum(b_end + m0, m_loc)
        kTw = (kTh * jnp.exp(a - m_new)).astype(MXU_DTYPE)
        vaug = jnp.concatenate([vh, ones_col], axis=1)
        cst[h] = jnp.exp(b_end + m0 - m_new) * caug + jnp.dot(kTw, vaug, preferred_element_type=jnp.float32)
        mst[h:h + 1, :] = jnp.broadcast_to(m_new, (1, LANES))
        mu = jnp.mean(hh, axis=1, keepdims=True)
        hc = hh - mu
        var = jnp.mean(hc * hc, axis=1, keepdims=True)
        hn = hc * lax.rsqrt(var + LN_EPS) * ng_ref[:, h * DV:(h + 1) * DV]
        y_ref[:, h * DV:(h + 1) * DV] = (hn * jax.nn.sigmoid(og_ref[:, h * DV:(h + 1) * DV])).astype(y_ref.dtype)


def _mlstm(om, osm, conv_w, conv_b, gate_bias, norm_g, B, S, L):
    nc = S // L
    W = MLSTM_WIDTH
    assert 2 * QK_M == W
    return pl.pallas_call(
        functools.partial(_mlstm_kernel, L=L),
        grid=(B, nc),
        in_specs=[pl.BlockSpec((L, W), lambda b, c: (b * nc + c, 0)),
                  pl.BlockSpec((L, W), lambda b, c: (b * nc + c, 1)),
                  pl.BlockSpec((L, W), lambda b, c: (b * nc + c, 2)),
                  pl.BlockSpec((L, COLS_S), lambda b, c: (b * nc + c, 0)),
                  pl.BlockSpec((CONV_WIDTH, W), lambda b, c: (0, 0)),
                  pl.BlockSpec((1, W), lambda b, c: (0, 0)),
                  pl.BlockSpec((1, COLS_S), lambda b, c: (0, 0)),
                  pl.BlockSpec((1, W), lambda b, c: (0, 0))],
        out_specs=pl.BlockSpec((L, W), lambda b, c: (b * nc + c, 0)),
        out_shape=jax.ShapeDtypeStruct((B * S, W), MXU_DTYPE),
        scratch_shapes=[pltpu.VMEM((L + 2 * SUBLANES, W), jnp.float32),
                        pltpu.VMEM((MLSTM_HEADS, MLSTM_QK_DIM, 2 * LANES), jnp.float32),
                        pltpu.VMEM((SUBLANES, LANES), jnp.float32)],
        compiler_params=_cparams(("parallel", "arbitrary")),
    )(om, om, om, osm, conv_w, conv_b, gate_bias, norm_g)


def _layer_norm(z, g, b):
    mu = jnp.mean(z, axis=1, keepdims=True)
    zc = z - mu
    var = jnp.mean(zc * zc, axis=1, keepdims=True)
    return zc * lax.rsqrt(var + LN_EPS) * g + b


def _merge_kernel(ya_ref, ym_ref, g_ref, x_ref, wa_ref, wm_ref, wo_ref, lg_ref, lb_ref,
                  wr_ref, br_ref, x1_ref, xp_ref, idx_ref, gate_ref, *, d_model):
    D = d_model
    mix = (jax.nn.sigmoid(g_ref[:, :D]) * jnp.dot(ya_ref[...], wa_ref[...], preferred_element_type=jnp.float32)
           + jax.nn.sigmoid(g_ref[:, D:]) * jnp.dot(ym_ref[...], wm_ref[...], preferred_element_type=jnp.float32))
    y = jnp.dot(mix.astype(MXU_DTYPE), wo_ref[...], preferred_element_type=jnp.float32)
    x1 = _layer_norm(DEEPNORM_ALPHA * x_ref[...] + y, lg_ref[...], lb_ref[...])
    x1_ref[...] = x1
    xb = x1.astype(MXU_DTYPE)
    bits = pltpu.bitcast(x1.astype(jnp.bfloat16).astype(jnp.float32), jnp.uint32)
    xp_ref[...] = (bits[:, :D // 2] & jnp.uint32(0xFFFF0000)) | (bits[:, D // 2:] >> 16)

    logits = jnp.dot(xb, wr_ref[...], preferred_element_type=jnp.float32) + br_ref[...]
    tm = logits.shape[0]
    lane = lax.broadcasted_iota(jnp.int32, (tm, LANES), 1)
    lane_f = lane.astype(jnp.float32)
    vals, idxs = [], []
    for _ in range(TOP_K):
        mx = jnp.max(logits, axis=1, keepdims=True)
        ix = jnp.min(jnp.where(logits == mx, lane_f, float(LANES)), axis=1, keepdims=True)
        vals.append(mx)
        idxs.append(ix)
        logits = jnp.where(lane_f == ix, -jnp.inf, logits)
    es = [jnp.exp(v - vals[0]) for v in vals]
    tot = es[0]
    for e in es[1:]:
        tot = tot + e
    idx_out = jnp.zeros((tm, LANES), jnp.float32)
    gate_out = jnp.zeros((tm, LANES), jnp.float32)
    for k in range(TOP_K):
        idx_out = jnp.where(lane == k, idxs[k], idx_out)
        gate_out = jnp.where(lane == k, es[k] / tot, gate_out)
    idx_ref[...] = idx_out.astype(jnp.int32)
    gate_ref[...] = gate_out


def _merge(y_att, y_m, og, x2, wa, wm, wo, ln_g, ln_b, wr, br, tm):
    T, D = x2.shape
    full = lambda shape: pl.BlockSpec(shape, lambda i: (0, 0))
    return pl.pallas_call(
        functools.partial(_merge_kernel, d_model=D),
        grid=(T // tm,),
        in_specs=[pl.BlockSpec((tm, ATT_WIDTH), lambda i: (i, 0)),
                  pl.BlockSpec((tm, MLSTM_WIDTH), lambda i: (i, 0)),
                  pl.BlockSpec((tm, 2 * D), lambda i: (i, 0)),
                  pl.BlockSpec((tm, D), lambda i: (i, 0)),
                  full(wa.shape), full(wm.shape), full(wo.shape),
                  full((1, D)), full((1, D)), full(wr.shape), full((1, LANES))],
        out_specs=[pl.BlockSpec((tm, D), lambda i: (i, 0)),
                   pl.BlockSpec((tm, D // 2), lambda i: (i, 0)),
                   pl.BlockSpec((tm, LANES), lambda i: (i, 0)),
                   pl.BlockSpec((tm, LANES), lambda i: (i, 0))],
        out_shape=[jax.ShapeDtypeStruct((T, D), jnp.float32),
                   jax.ShapeDtypeStruct((T, D // 2), jnp.uint32),
                   jax.ShapeDtypeStruct((T, LANES), jnp.int32),
                   jax.ShapeDtypeStruct((T, LANES), jnp.float32)],
        compiler_params=_cparams(("parallel",)),
    )(y_att, y_m, og, x2, wa, wm, wo, ln_g, ln_b, wr, br)


def _rank_kernel(idx_ref, rank_ref, cnt_ref, carry_ref):
    @pl.when(pl.program_id(0) == 0)
    def _():
        carry_ref[...] = jnp.zeros(carry_ref.shape, jnp.float32)

    tm = idx_ref.shape[0]
    lane = lax.broadcasted_iota(jnp.int32, (tm, LANES), 1)
    idx = idx_ref[...]
    hots = [lane == idx[:, k:k + 1] for k in range(TOP_K)]
    c = jnp.zeros((tm, LANES), jnp.float32)
    for hot in hots:
        c = c + jnp.where(hot, 1.0, 0.0)
    before = jnp.where(lax.broadcasted_iota(jnp.int32, (tm, tm), 0)
                       > lax.broadcasted_iota(jnp.int32, (tm, tm), 1), 1.0, 0.0).astype(MXU_DTYPE)
    carry = carry_ref[0:1, :]
    prior = jnp.dot(before, c.astype(MXU_DTYPE), preferred_element_type=jnp.float32) + carry
    out = jnp.zeros((tm, LANES), jnp.float32)
    for k, hot in enumerate(hots):
        out = jnp.where(lane == k, jnp.sum(jnp.where(hot, prior, 0.0), axis=1, keepdims=True), out)
    rank_ref[...] = out.astype(jnp.int32)
    total = carry + jnp.sum(c, axis=0, keepdims=True)
    carry_ref[...] = jnp.broadcast_to(total, carry_ref.shape)
    cnt_ref[...] = jnp.broadcast_to(total, cnt_ref.shape)


def _ranks(idx, tm):
    T = idx.shape[0]
    return pl.pallas_call(
        _rank_kernel,
        grid=(T // tm,),
        in_specs=[pl.BlockSpec((tm, LANES), lambda i: (i, 0))],
        out_specs=[pl.BlockSpec((tm, LANES), lambda i: (i, 0)),
                   pl.BlockSpec((SUBLANES, LANES), lambda i: (0, 0))],
        out_shape=[jax.ShapeDtypeStruct((T, LANES), jnp.int32),
                   jax.ShapeDtypeStruct((SUBLANES, LANES), jnp.float32)],
        scratch_shapes=[pltpu.VMEM((SUBLANES, LANES), jnp.float32)],
        compiler_params=_cparams(("arbitrary",)),
    )(idx)


def _row_copy(src_ref, s, dst_ref, d, sem):
    return pltpu.make_async_copy(src_ref.at[pl.ds(s, 1), :], dst_ref.at[pl.ds(d, 1), :], sem)


def _dispatch_kernel(dest_ref, x_ref, xs_in_ref, xs_ref, sem):
    del xs_in_ref
    tm = x_ref.shape[0]

    def start(t, carry):
        for k in range(TOP_K):
            _row_copy(x_ref, t, xs_ref, dest_ref[t * TOP_K + k], sem).start()
        return carry

    lax.fori_loop(0, tm, start, 0)

    def wait(t, carry):
        for k in range(TOP_K):
            _row_copy(x_ref, 0, xs_ref, 0, sem).wait()
        return carry

    lax.fori_loop(0, tm, wait, 0)


def _dispatch(dest_flat, xp, cap, tm):
    T, W = xp.shape
    xs0 = jnp.zeros((cap, W), xp.dtype)
    return pl.pallas_call(
        _dispatch_kernel,
        grid=(T // tm,),
        in_specs=[pl.BlockSpec((tm * TOP_K,), lambda i: (i,), memory_space=pltpu.SMEM),
                  pl.BlockSpec((tm, W), lambda i: (i, 0)),
                  pl.BlockSpec(memory_space=pl.ANY)],
        out_specs=pl.BlockSpec(memory_space=pl.ANY),
        out_shape=jax.ShapeDtypeStruct((cap, W), xp.dtype),
        scratch_shapes=[pltpu.SemaphoreType.DMA],
        input_output_aliases={2: 0},
        compiler_params=_cparams(("arbitrary",)),
    )(dest_flat, xp, xs0)


def _ffn_kernel(be_ref, nu_ref, xs_ref, wgu_ref, bgu_ref, wd_ref, bd_ref, y_ref, wgu_b, wd_b,
                *, d_ff):
    r = pl.program_id(0)
    e = be_ref[r]
    prev = be_ref[jnp.maximum(r - 1, 0)]

    @pl.when((r == 0) | (e != prev))
    def _():
        wgu_b[...] = wgu_ref[0].astype(MXU_DTYPE)
        wd_b[...] = wd_ref[0].astype(MXU_DTYPE)

    @pl.when(r < nu_ref[0])
    def _():
        w = xs_ref[...]
        half = w.shape[1]
        x_hi = pltpu.bitcast(w & jnp.uint32(0xFFFF0000), jnp.float32).astype(MXU_DTYPE)
        x_lo = pltpu.bitcast(w << 16, jnp.float32).astype(MXU_DTYPE)
        step = 512
        acc = bd_ref[0] + jnp.zeros(y_ref.shape, jnp.float32)
        for j in range(0, d_ff, step):
            def gu(lo):
                return (jnp.dot(x_hi, wgu_b[0:half, lo:lo + step], preferred_element_type=jnp.float32)
                        + jnp.dot(x_lo, wgu_b[half:2 * half, lo:lo + step], preferred_element_type=jnp.float32)
                        + bgu_ref[0, :, lo:lo + step])
            gate = jnp.minimum(gu(j), SWIGLU_LIMIT)
            up = jnp.clip(gu(d_ff + j), -SWIGLU_LIMIT, SWIGLU_LIMIT)
            act = (up + 1.0) * (gate * jax.nn.sigmoid(SWIGLU_ALPHA * gate))
            acc = acc + jnp.dot(act.astype(MXU_DTYPE), wd_b[j:j + step, :],
                                preferred_element_type=jnp.float32)
        y_ref[...] = acc

    @pl.when(r >= nu_ref[0])
    def _():
        y_ref[...] = jnp.zeros(y_ref.shape, jnp.float32)


def _expert_ffn(block_expert, n_used, xs, w_gate_up, b_gate_up, w_down, b_down, bm):
    cap, half = xs.shape
    E, D, F2 = w_gate_up.shape
    d_ff = F2 // 2
    grid_spec = pltpu.PrefetchScalarGridSpec(
        num_scalar_prefetch=2,
        grid=(cap // bm,),
        in_specs=[pl.BlockSpec((bm, half), lambda r, be, nu: (r, 0)),
                  pl.BlockSpec((1, D, F2), lambda r, be, nu: (be[r], 0, 0)),
                  pl.BlockSpec((1, 1, F2), lambda r, be, nu: (be[r], 0, 0)),
                  pl.BlockSpec((1, d_ff, D), lambda r, be, nu: (be[r], 0, 0)),
                  pl.BlockSpec((1, 1, D), lambda r, be, nu: (be[r], 0, 0))],
        out_specs=pl.BlockSpec((bm, D), lambda r, be, nu: (r, 0)),
        scratch_shapes=[pltpu.VMEM((D, F2), MXU_DTYPE),
                        pltpu.VMEM((d_ff, D), MXU_DTYPE)],
    )
    return pl.pallas_call(
        functools.partial(_ffn_kernel, d_ff=d_ff),
        grid_spec=grid_spec,
        out_shape=jax.ShapeDtypeStruct((cap, D), jnp.float32),
        compiler_params=_cparams(("arbitrary",)),
    )(block_expert, n_used, xs, w_gate_up, b_gate_up.reshape(E, 1, F2), w_down,
      b_down.reshape(E, 1, D))


def _combine_kernel(dest_ref, gate_ref, x1_ref, yb_ref, lg_ref, lb_ref, o_ref, buf, sem):
    tm = x1_ref.shape[0]

    def start(t, carry):
        for k in range(TOP_K):
            _row_copy(yb_ref, dest_ref[t * TOP_K + k], buf.at[k], t, sem).start()
        return carry

    lax.fori_loop(0, tm, start, 0)

    def wait(t, carry):
        for k in range(TOP_K):
            _row_copy(yb_ref, 0, buf.at[k], 0, sem).wait()
        return carry

    lax.fori_loop(0, tm, wait, 0)
    y = gate_ref[:, 0:1] * buf[0]
    for k in range(1, TOP_K):
        y = y + gate_ref[:, k:k + 1] * buf[k]
    o_ref[...] = _layer_norm(DEEPNORM_ALPHA * x1_ref[...] + y, lg_ref[...], lb_ref[...])


def _combine(dest_flat, gates, x1, ybuf, ln_g, ln_b, tm):
    T, D = x1.shape
    return pl.pallas_call(
        _combine_kernel,
        grid=(T // tm,),
        in_specs=[pl.BlockSpec((tm * TOP_K,), lambda i: (i,), memory_space=pltpu.SMEM),
                  pl.BlockSpec((tm, LANES), lambda i: (i, 0)),
                  pl.BlockSpec((tm, D), lambda i: (i, 0)),
                  pl.BlockSpec(memory_space=pl.ANY),
                  pl.BlockSpec((1, D), lambda i: (0, 0)),
                  pl.BlockSpec((1, D), lambda i: (0, 0))],
        out_specs=pl.BlockSpec((tm, D), lambda i: (i, 0)),
        out_shape=jax.ShapeDtypeStruct((T, D), jnp.float32),
        scratch_shapes=[pltpu.VMEM((TOP_K, tm, D), jnp.float32), pltpu.SemaphoreType.DMA],
        compiler_params=_cparams(("arbitrary",)),
    )(dest_flat, gates, x1, ybuf, ln_g, ln_b)


def _tile(n, pref):
    t = min(n, pref)
    assert n % t == 0
    return t


def _relayout_w_in(w_in, d_model):
    sizes = (ATT_WIDTH, ATT_WIDTH, ATT_WIDTH, IDX_HEADS * IDX_HEAD_DIM, IDX_HEAD_DIM, IDX_HEADS,
             QK_M, QK_M, MLSTM_WIDTH, MLSTM_HEADS, MLSTM_HEADS, MLSTM_WIDTH, d_model, d_model)
    offs = [0]
    for s in sizes:
        offs.append(offs[-1] + s)
    seg = lambda k: w_in[:, offs[k]:offs[k + 1]]
    (q_a, k_a, v_a, q_i, k_i, w_i, q_m, k_m, v_m, i_m, f_m, o_m, g_a, g_m) = [seg(k) for k in range(14)]
    pad = jnp.zeros((w_in.shape[0], COLS_S - (IDX_HEAD_DIM + IDX_HEADS + 2 * MLSTM_HEADS)), w_in.dtype)
    cols = [q_a, k_a, v_a, q_i, k_i, w_i, i_m, f_m, pad, q_m, k_m, v_m, o_m, g_a, g_m, k_i, k_i]
    return jnp.concatenate(cols, axis=1).astype(MXU_DTYPE)


def _layer(x2, B, S, w_in, conv_w, conv_b, i_bias, f_bias, norm_g, w_branch_attn, w_branch_mlstm,
           w_out, ln1_g, ln1_b, w_router, b_router, w_gate_up, b_gate_up, w_down, b_down,
           ln2_g, ln2_b, rel_bias):
    T, D = x2.shape
    bf = MXU_DTYPE
    tq = _tile(S, 256)
    L = _tile(S, 256)
    tm = _tile(T, 256)

    oa, osm, om, og, okk = _project(x2, _relayout_w_in(w_in, D), tm)
    bias_t = _bias_tiles(rel_bias, tq)
    y_att = _dsa_attention(oa, osm, okk, bias_t, B, S, tq)

    gate_bias = jnp.zeros((1, COLS_S), jnp.float32)
    gate_bias = gate_bias.at[0, S_IM:S_IM + MLSTM_HEADS].set(i_bias)
    gate_bias = gate_bias.at[0, S_FM:S_FM + MLSTM_HEADS].set(f_bias)
    y_m = _mlstm(om, osm, conv_w, conv_b.reshape(1, -1), gate_bias, norm_g.reshape(1, -1), B, S, L)

    wr = jnp.zeros((D, LANES), bf).at[:, :N_EXPERTS].set(w_router.astype(bf))
    br = jnp.full((1, LANES), NEG_BIG, jnp.float32).at[0, :N_EXPERTS].set(b_router)
    x1, xp, idx, gates = _merge(y_att, y_m, og, x2, w_branch_attn.astype(bf), w_branch_mlstm.astype(bf),
                                w_out.astype(bf), ln1_g.reshape(1, D), ln1_b.reshape(1, D), wr, br, tm)

    bm = 512
    rank, cnt = _ranks(idx, _tile(T, 512))
    counts = cnt[0, :N_EXPERTS].astype(jnp.int32)
    padded = ((counts + bm - 1) // bm) * bm
    pend = jnp.cumsum(padded)
    pstart = pend - padded
    cap = ((T * TOP_K + bm - 1) // bm) * bm + N_EXPERTS * bm
    n_blocks = cap // bm
    dest = (pstart[idx[:, :TOP_K]] + rank[:, :TOP_K]).reshape(T * TOP_K)
    block_expert = jnp.minimum(
        jnp.searchsorted(pend, jnp.arange(n_blocks, dtype=jnp.int32) * bm, side='right'),
        N_EXPERTS - 1).astype(jnp.int32)
    n_used = (pend[-1:] // bm).astype(jnp.int32)

    xs = _dispatch(dest, xp, cap, _tile(T, 512))
    ybuf = _expert_ffn(block_expert, n_used, xs, w_gate_up, b_gate_up, w_down, b_down, bm)
    return _combine(dest, gates, x1, ybuf, ln2_g.reshape(1, D), ln2_b.reshape(1, D), tm)


def kernel(x, w_in, conv_w, conv_b, mlstm_i_bias, mlstm_f_bias, mlstm_norm_g, w_branch_attn,
           w_branch_mlstm, w_out, ln1_g, ln1_b, w_router, b_router, w_gate_up, b_gate_up,
           w_down, b_down, ln2_g, ln2_b, rel_bias):
    B, S, D = x.shape
    x2 = x.reshape(B * S, D)
    for l in range(w_in.shape[0]):
        x2 = _layer(x2, B, S, w_in[l], conv_w[l], conv_b[l], mlstm_i_bias[l], mlstm_f_bias[l],
                    mlstm_norm_g[l], w_branch_attn[l], w_branch_mlstm[l], w_out[l], ln1_g[l], ln1_b[l],
                    w_router[l], b_router[l], w_gate_up[l], b_gate_up[l], w_down[l], b_down[l],
                    ln2_g[l], ln2_b[l], rel_bias)
    return x2.reshape(B, S, D)
```

```python
import functools
import math

import jax
import jax.numpy as jnp
from jax import lax
from jax.experimental import pallas as pl
from jax.experimental.pallas import tpu as pltpu

ATT_HEADS = 8
ATT_HEAD_DIM = 64
ATT_WIDTH = ATT_HEADS * ATT_HEAD_DIM
IDX_HEADS = 8
IDX_HEAD_DIM = 64
TOPK_MAX = 256
MLSTM_HEADS = 4
MLSTM_QK_DIM = 64
MLSTM_V_DIM = 128
MLSTM_WIDTH = MLSTM_HEADS * MLSTM_V_DIM
CONV_WIDTH = 4
N_BUCKETS = 32
MAX_DISTANCE = 128
N_EXPERTS = 32
TOP_K = 4
SWIGLU_ALPHA = 1.702
SWIGLU_LIMIT = 7.0
LN_EPS = 1e-5
DEPTH = 1
DEEPNORM_ALPHA = (2 * DEPTH) ** 0.25

LANES = 128
SUBLANES = 8
VMEM_LIMIT_BYTES = 56 * 1024 * 1024

MXU_DTYPE = jnp.bfloat16

INT_MIN = -(2 ** 31)
NEG_BIG = -1e30

QK_M = MLSTM_HEADS * MLSTM_QK_DIM
COLS_A = 3 * ATT_WIDTH + IDX_HEADS * IDX_HEAD_DIM
COLS_S = LANES
COLS_M = 2 * QK_M + 2 * MLSTM_WIDTH
COLS_K = 2 * IDX_HEAD_DIM
S_KI = 0
S_WI = IDX_HEAD_DIM
S_IM = S_WI + IDX_HEADS
S_FM = S_IM + MLSTM_HEADS


def _cparams(sem):
    return pltpu.CompilerParams(dimension_semantics=sem, vmem_limit_bytes=VMEM_LIMIT_BYTES)


def _proj_kernel(x_ref, w_ref, oa_ref, os_ref, om_ref, og_ref, ok_ref, *, d_model):
    xb = x_ref[...].astype(MXU_DTYPE)
    step = 512

    def mm(lo, hi):
        return jnp.dot(xb, w_ref[:, lo:hi], preferred_element_type=jnp.float32)

    base = 0
    for j in range(0, COLS_A, step):
        oa_ref[:, j:j + step] = mm(base + j, base + j + step).astype(MXU_DTYPE)
    base += COLS_A
    os_ref[...] = mm(base, base + COLS_S)
    base += COLS_S
    for j in range(0, COLS_M, step):
        om_ref[:, j:j + step] = mm(base + j, base + j + step)
    base += COLS_M
    for j in range(0, 2 * d_model, step):
        og_ref[:, j:j + step] = mm(base + j, base + j + step)
    base += 2 * d_model
    ok_ref[...] = mm(base, base + COLS_K).astype(MXU_DTYPE)


def _project(x2, w_p, tm):
    T, D = x2.shape
    n_all = w_p.shape[1]
    return pl.pallas_call(
        functools.partial(_proj_kernel, d_model=D),
        grid=(T // tm,),
        in_specs=[pl.BlockSpec((tm, D), lambda i: (i, 0)),
                  pl.BlockSpec((D, n_all), lambda i: (0, 0))],
        out_specs=[pl.BlockSpec((tm, COLS_A), lambda i: (i, 0)),
                   pl.BlockSpec((tm, COLS_S), lambda i: (i, 0)),
                   pl.BlockSpec((tm, COLS_M), lambda i: (i, 0)),
                   pl.BlockSpec((tm, 2 * D), lambda i: (i, 0)),
                   pl.BlockSpec((tm, COLS_K), lambda i: (i, 0))],
        out_shape=[jax.ShapeDtypeStruct((T, COLS_A), MXU_DTYPE),
                   jax.ShapeDtypeStruct((T, COLS_S), jnp.float32),
                   jax.ShapeDtypeStruct((T, COLS_M), jnp.float32),
                   jax.ShapeDtypeStruct((T, 2 * D), jnp.float32),
                   jax.ShapeDtypeStruct((T, COLS_K), MXU_DTYPE)],
        compiler_params=_cparams(("parallel",)),
    )(x2, w_p)


def _bias_kernel(rb_ref, o_ref, *, tq):
    h = pl.program_id(0)
    s = lax.broadcasted_iota(jnp.int32, (tq, tq), 0)
    t = lax.broadcasted_iota(jnp.int32, (tq, tq), 1)
    max_exact = N_BUCKETS // 2
    for d in range(3):
        n = jnp.maximum(t - s + d * tq, 0)
        n_f = jnp.maximum(n, 1).astype(jnp.float32)
        large = max_exact + (jnp.log(n_f / max_exact) / math.log(MAX_DISTANCE / max_exact)
                             * (N_BUCKETS - max_exact)).astype(jnp.int32)
        large = jnp.minimum(large, N_BUCKETS - 1)
        bucket = jnp.where(n < max_exact, n, large)
        acc = jnp.zeros((tq, tq), jnp.float32)
        for k in range(N_BUCKETS):
            acc = jnp.where(bucket == k, rb_ref[h, k], acc)
        o_ref[0, d] = acc


def _bias_tiles(rel_bias, tq):
    assert tq + 1 >= MAX_DISTANCE
    H = rel_bias.shape[0]
    return pl.pallas_call(
        functools.partial(_bias_kernel, tq=tq),
        grid=(H,),
        in_specs=[pl.BlockSpec(memory_space=pltpu.SMEM)],
        out_specs=pl.BlockSpec((1, 3, tq, tq), lambda h: (h, 0, 0, 0)),
        out_shape=jax.ShapeDtypeStruct((H, 3, tq, tq), jnp.float32),
        compiler_params=_cparams(("parallel",)),
    )(rel_bias)


def _dsa_kernel(qa_ref, ka_ref, vt_ref, qi_ref, kk_ref, wq_ref, bias_ref, o_ref,
                keys_ref, hi_ref, lo_ref, msk_ref, qz_ref, qiz_ref, s_ref, p_ref, acc_ref, m_ref, l_ref,
                *, tq, n_sel):
    i = pl.program_id(1)
    nch = i + 1
    t_pos = i * tq + lax.broadcasted_iota(jnp.int32, (1, tq), 1)
    s_loc = lax.broadcasted_iota(jnp.int32, (tq, 1), 0)
    hd = ATT_HEAD_DIM
    n_pairs = ATT_HEADS // 2
    assert 2 * hd == LANES and IDX_HEAD_DIM == hd and IDX_HEADS == ATT_HEADS

    top = lax.broadcasted_iota(jnp.int32, (LANES, tq), 0) < hd

    def pair_operand(blk):
        bt = blk.astype(jnp.float32).T
        return jnp.concatenate([jnp.where(top, bt, 0.0), jnp.where(top, 0.0, bt)],
                               axis=1).astype(MXU_DTYPE)

    for p in range(n_pairs):
        pair = slice(p * LANES, (p + 1) * LANES)
        qz_ref[p] = pair_operand(qa_ref[:, pair] * (hd ** -0.5))
        qiz_ref[p] = pair_operand(qi_ref[:, pair])

    ws = wq_ref[...].T[S_WI:S_WI + IDX_HEADS, :] * (IDX_HEADS ** -0.5)

    def score_body(c, carry):
        off = pl.multiple_of(c * tq, tq)
        kk = kk_ref[pl.ds(off, tq), :]
        sc = jnp.zeros((tq, tq), jnp.float32)
        for p in range(n_pairs):
            d = jnp.dot(kk, qiz_ref[p], preferred_element_type=jnp.float32)
            for j in range(2):
                h = 2 * p + j
                sc = sc + ws[h:h + 1, :] * jnp.maximum(d[:, j * tq:(j + 1) * tq], 0.0)
        sc = sc + 0.0
        b = pltpu.bitcast(sc, jnp.int32)
        sk = b ^ ((b >> 31) & jnp.int32(0x7FFFFFFF))
        sk = jnp.where(off + s_loc <= t_pos, sk, jnp.int32(INT_MIN))
        keys_ref[c] = sk
        hi_ref[c] = (sk >> 16).astype(jnp.int16)
        return carry

    lax.fori_loop(0, nch, score_body, 0)

    pack = 2 * SUBLANES
    half_min = -(2 ** 15)

    def count16(ref, pred_fn):
        def cnt_body(c, acc):
            hit = jnp.where(pred_fn(ref[c]), jnp.int16(1), jnp.int16(0))
            parts = [hit[r * pack:(r + 1) * pack, :] for r in range(tq // pack)]
            while len(parts) > 1:
                parts = [a + b for a, b in zip(parts[::2], parts[1::2])]
            return acc + parts[0]
        acc = lax.fori_loop(0, nch, cnt_body, jnp.zeros((pack, tq), jnp.int16))
        return jnp.sum(acc.astype(jnp.float32), axis=0, keepdims=True)

    def search16(ref, target):
        def bit_body(it, u):
            cand_u = u | lax.shift_left(jnp.int32(1), 15 - it)
            cand = (cand_u + half_min).astype(jnp.int16)
            return jnp.where(count16(ref, lambda x: x >= cand) >= target, cand_u, u)
        return lax.fori_loop(0, 16, bit_body, jnp.zeros((1, tq), jnp.int32))

    u_hi = search16(hi_ref, float(n_sel))
    thr_hi = (u_hi + half_min).astype(jnp.int16)
    need_lo = n_sel - count16(hi_ref, lambda x: x > thr_hi)

    def low_body(c, carry):
        lo = ((keys_ref[c] & 0xFFFF) + half_min).astype(jnp.int16)
        lo_ref[c] = jnp.where(hi_ref[c] == thr_hi, lo, jnp.int16(half_min))
        return carry

    lax.fori_loop(0, nch, low_body, 0)
    u_lo = search16(lo_ref, need_lo)
    thr_lo = (u_lo + half_min).astype(jnp.int16)
    thr = lax.shift_left(u_hi + half_min, 16) | u_lo
    need = jnp.where(thr == INT_MIN, 0.0, need_lo - count16(lo_ref, lambda x: x > thr_lo))

    tri = jnp.where(lax.broadcasted_iota(jnp.int32, (tq, tq), 0)
                    >= lax.broadcasted_iota(jnp.int32, (tq, tq), 1), 1.0, 0.0).astype(MXU_DTYPE)

    def mask_body(c, run):
        kc = keys_ref[c]
        eq = kc == thr
        eqf = jnp.where(eq, 1.0, 0.0)
        prefix = jnp.dot(tri, eqf.astype(MXU_DTYPE), preferred_element_type=jnp.float32) + run
        msk_ref[c] = jnp.where(kc > thr, 0.0,
                               jnp.where(eq, jnp.where(prefix <= need, 0.0, NEG_BIG), NEG_BIG))
        return run + jnp.sum(eqf, axis=0, keepdims=True)

    lax.fori_loop(0, nch, mask_body, jnp.zeros((1, tq), jnp.float32))

    m_ref[...] = jnp.full(m_ref.shape, NEG_BIG, jnp.float32)
    l_ref[...] = jnp.zeros(l_ref.shape, jnp.float32)
    acc_ref[...] = jnp.zeros(acc_ref.shape, jnp.float32)

    def att_body(c, carry):
        off = pl.multiple_of(c * tq, tq)
        bsel = jnp.minimum(i - c, 2)
        mk = msk_ref[c]
        tile_max = []
        for p in range(n_pairs):
            pair = slice(p * LANES, (p + 1) * LANES)
            s2 = jnp.dot(ka_ref[pl.ds(off, tq), pair], qz_ref[p], preferred_element_type=jnp.float32)
            for j in range(2):
                h = 2 * p + j
                s = s2[:, j * tq:(j + 1) * tq] + bias_ref[h, bsel] + mk
                s_ref[h] = s
                tile_max.append(jnp.max(s, axis=0, keepdims=True))
        alphas = []
        for h in range(ATT_HEADS):
            m_prev = m_ref[h:h + 1, :]
            m_new = jnp.maximum(m_prev, tile_max[h])
            alpha = jnp.exp(m_prev - m_new)
            pe = jnp.exp(s_ref[h] - m_new)
            l_ref[h:h + 1, :] = alpha * l_ref[h:h + 1, :] + jnp.sum(pe, axis=0, keepdims=True)
            m_ref[h:h + 1, :] = m_new
            p_ref[h // 2, :, (h % 2) * tq:(h % 2 + 1) * tq] = pe.astype(MXU_DTYPE)
            alphas.append(alpha)
        for p in range(n_pairs):
            pv = jnp.dot(vt_ref[c, p * LANES:(p + 1) * LANES, :], p_ref[p],
                         preferred_element_type=jnp.float32)
            for j in range(2):
                h = 2 * p + j
                acc_ref[h] = alphas[h] * acc_ref[h] + pv[j * hd:(j + 1) * hd, j * tq:(j + 1) * tq]
        return carry

    lax.fori_loop(0, nch, att_body, 0)
    y_t = jnp.concatenate([acc_ref[h] / l_ref[h:h + 1, :] for h in range(ATT_HEADS)], axis=0)
    o_ref[...] = y_t.T.astype(o_ref.dtype)


def _dsa_attention(oa, osm, okk, bias_t, B, S, tq):
    nq = S // tq
    n_sel = min(TOPK_MAX, S // 4)
    W = ATT_WIDTH
    v_t = oa[:, 2 * W:3 * W].reshape(B * nq, tq, W).transpose(0, 2, 1)
    return pl.pallas_call(
        functools.partial(_dsa_kernel, tq=tq, n_sel=n_sel),
        grid=(B, nq),
        in_specs=[pl.BlockSpec((tq, W), lambda b, i: (b * nq + i, 0)),
                  pl.BlockSpec((S, W), lambda b, i: (b, 1)),
                  pl.BlockSpec((nq, W, tq), lambda b, i: (b, 0, 0)),
                  pl.BlockSpec((tq, W), lambda b, i: (b * nq + i, 3)),
                  pl.BlockSpec((S, COLS_K), lambda b, i: (b, 0)),
                  pl.BlockSpec((tq, COLS_S), lambda b, i: (b * nq + i, 0)),
                  pl.BlockSpec((ATT_HEADS, 3, tq, tq), lambda b, i: (0, 0, 0, 0))],
        out_specs=pl.BlockSpec((tq, W), lambda b, i: (b * nq + i, 0)),
        out_shape=jax.ShapeDtypeStruct((B * S, W), MXU_DTYPE),
        scratch_shapes=[pltpu.VMEM((nq, tq, tq), jnp.int32),
                        pltpu.VMEM((nq, tq, tq), jnp.int16),
                        pltpu.VMEM((nq, tq, tq), jnp.int16),
                        pltpu.VMEM((nq, tq, tq), jnp.float32),
                        pltpu.VMEM((ATT_HEADS // 2, LANES, 2 * tq), MXU_DTYPE),
                        pltpu.VMEM((IDX_HEADS // 2, LANES, 2 * tq), MXU_DTYPE),
                        pltpu.VMEM((ATT_HEADS, tq, tq), jnp.float32),
                        pltpu.VMEM((ATT_HEADS // 2, tq, 2 * tq), MXU_DTYPE),
                        pltpu.VMEM((ATT_HEADS, ATT_HEAD_DIM, tq), jnp.float32),
                        pltpu.VMEM((ATT_HEADS, tq), jnp.float32),
                        pltpu.VMEM((ATT_HEADS, tq), jnp.float32)],
        compiler_params=_cparams(("parallel", "arbitrary")),
    )(oa, oa, v_t, oa, okk, osm, bias_t)


def _mlstm_kernel(qk_ref, v_ref, og_ref, s_ref, cw_ref, cb_ref, gb_ref, ng_ref, y_ref,
                  xbuf, cst, mst, *, L):
    c = pl.program_id(1)
    H, DK, DV = MLSTM_HEADS, MLSTM_QK_DIM, MLSTM_V_DIM
    tail = SUBLANES

    @pl.when(c == 0)
    def _():
        xbuf[0:tail, :] = jnp.zeros((tail, 2 * QK_M), jnp.float32)
        cst[...] = jnp.zeros(cst.shape, jnp.float32)
        mst[...] = jnp.zeros(mst.shape, jnp.float32)

    @pl.when(c > 0)
    def _():
        xbuf[0:tail, :] = xbuf[L:L + tail, :]

    xbuf[tail:tail + L, :] = qk_ref[...]
    conv = cb_ref[...] + jnp.zeros((L, 2 * QK_M), jnp.float32)
    for j in range(CONV_WIDTH):
        conv = conv + cw_ref[j:j + 1, :] * xbuf[pl.ds(tail - (CONV_WIDTH - 1) + j, L), :]
    qk = conv * jax.nn.sigmoid(conv)
    q = qk[:, :QK_M]
    kT = (qk[:, QK_M:] * (DK ** -0.5)).T

    g = s_ref[...] + gb_ref[...]
    logf = jnp.minimum(g, 0.0) - jnp.log(1.0 + jnp.exp(-jnp.abs(g)))
    row = lax.broadcasted_iota(jnp.int32, (L, L), 0)
    col = lax.broadcasted_iota(jnp.int32, (L, L), 1)
    causal = row >= col
    bcum = jnp.dot(jnp.where(causal, 1.0, 0.0), logf, precision=lax.Precision.HIGHEST,
                   preferred_element_type=jnp.float32)
    gT = g.T
    bT = bcum.T
    ones_col = jnp.where(lax.broadcasted_iota(jnp.int32, (L, LANES), 1) == 0, 1.0, 0.0
                         ).astype(MXU_DTYPE)

    for h in range(H):
        bc = bcum[:, S_FM + h:S_FM + h + 1]
        br = bT[S_FM + h:S_FM + h + 1, :]
        lir = gT[S_IM + h:S_IM + h + 1, :]
        m0 = mst[h:h + 1, 0:1]
        d = jnp.where(causal, bc - br + lir, -jnp.inf)
        m_inter = bc + m0
        m_t = jnp.maximum(m_inter, jnp.max(d, axis=1, keepdims=True))
        qh = q[:, h * DK:(h + 1) * DK].astype(MXU_DTYPE)
        kTh = kT[h * DK:(h + 1) * DK, :]
        p = jnp.dot(qh, kTh.astype(MXU_DTYPE), preferred_element_type=jnp.float32) * jnp.exp(d - m_t)
        sc = jnp.exp(m_inter - m_t)
        vh = v_ref[:, h * DV:(h + 1) * DV].astype(MXU_DTYPE)
        caug = cst[h]
        inter = jnp.dot(qh, caug.astype(MXU_DTYPE), preferred_element_type=jnp.float32)
        num = jnp.dot(p.astype(MXU_DTYPE), vh, preferred_element_type=jnp.float32) + sc * inter[:, :DV]
        den = jnp.sum(p, axis=1, keepdims=True) + sc * inter[:, DV:DV + 1]
        hh = num / jnp.maximum(jnp.abs(den), jnp.exp(-m_t))
        b_end = bc[L - 1:L, :]
        a = b_end - br + lir
        m_loc = jnp.max(a, axis=1, keepdims=True)
        m_new = jnp.maximum(b_end + m0, m_loc)
        kTw = (kTh * jnp.exp(a - m_new)).astype(MXU_DTYPE)
        vaug = jnp.concatenate([vh, ones_col], axis=1)
        cst[h] = jnp.exp(b_end + m0 - m_new) * caug + jnp.dot(kTw, vaug, preferred_element_type=jnp.float32)
        mst[h:h + 1, :] = jnp.broadcast_to(m_new, (1, LANES))
        mu = jnp.mean(hh, axis=1, keepdims=True)
        hc = hh - mu
        var = jnp.mean(hc * hc, axis=1, keepdims=True)
        hn = hc * lax.rsqrt(var + LN_EPS) * ng_ref[:, h * DV:(h + 1) * DV]
        y_ref[:, h * DV:(h + 1) * DV] = (hn * jax.nn.sigmoid(og_ref[:, h * DV:(h + 1) * DV])).astype(y_ref.dtype)


def _mlstm(om, osm, conv_w, conv_b, gate_bias, norm_g, B, S, L):
    nc = S // L
    W = MLSTM_WIDTH
    assert 2 * QK_M == W
    return pl.pallas_call(
        functools.partial(_mlstm_kernel, L=L),
        grid=(B, nc),
        in_specs=[pl.BlockSpec((L, W), lambda b, c: (b * nc + c, 0)),
                  pl.BlockSpec((L, W), lambda b, c: (b * nc + c, 1)),
                  pl.BlockSpec((L, W), lambda b, c: (b * nc + c, 2)),
                  pl.BlockSpec((L, COLS_S), lambda b, c: (b * nc + c, 0)),
                  pl.BlockSpec((CONV_WIDTH, W), lambda b, c: (0, 0)),
                  pl.BlockSpec((1, W), lambda b, c: (0, 0)),
                  pl.BlockSpec((1, COLS_S), lambda b, c: (0, 0)),
                  pl.BlockSpec((1, W), lambda b, c: (0, 0))],
        out_specs=pl.BlockSpec((L, W), lambda b, c: (b * nc + c, 0)),
        out_shape=jax.ShapeDtypeStruct((B * S, W), MXU_DTYPE),
        scratch_shapes=[pltpu.VMEM((L + 2 * SUBLANES, W), jnp.float32),
                        pltpu.VMEM((MLSTM_HEADS, MLSTM_QK_DIM, 2 * LANES), jnp.float32),
                        pltpu.VMEM((SUBLANES, LANES), jnp.float32)],
        compiler_params=_cparams(("parallel", "arbitrary")),
    )(om, om, om, osm, conv_w, conv_b, gate_bias, norm_g)


def _layer_norm(z, g, b):
    mu = jnp.mean(z, axis=1, keepdims=True)
    zc = z - mu
    var = jnp.mean(zc * zc, axis=1, keepdims=True)
    return zc * lax.rsqrt(var + LN_EPS) * g + b


def _merge_kernel(ya_ref, ym_ref, g_ref, x_ref, wa_ref, wm_ref, wo_ref, lg_ref, lb_ref,
                  wr_ref, br_ref, x1_ref, xp_ref, idx_ref, gate_ref, *, d_model):
    D = d_model
    mix = (jax.nn.sigmoid(g_ref[:, :D]) * jnp.dot(ya_ref[...], wa_ref[...], preferred_element_type=jnp.float32)
           + jax.nn.sigmoid(g_ref[:, D:]) * jnp.dot(ym_ref[...], wm_ref[...], preferred_element_type=jnp.float32))
    y = jnp.dot(mix.astype(MXU_DTYPE), wo_ref[...], preferred_element_type=jnp.float32)
    x1 = _layer_norm(DEEPNORM_ALPHA * x_ref[...] + y, lg_ref[...], lb_ref[...])
    x1_ref[...] = x1
    xb = x1.astype(MXU_DTYPE)
    bits = pltpu.bitcast(x1.astype(jnp.bfloat16).astype(jnp.float32), jnp.uint32)
    xp_ref[...] = (bits[:, :D // 2] & jnp.uint32(0xFFFF0000)) | (bits[:, D // 2:] >> 16)

    logits = jnp.dot(xb, wr_ref[...], preferred_element_type=jnp.float32) + br_ref[...]
    tm = logits.shape[0]
    lane = lax.broadcasted_iota(jnp.int32, (tm, LANES), 1)
    lane_f = lane.astype(jnp.float32)
    vals, idxs = [], []
    for _ in range(TOP_K):
        mx = jnp.max(logits, axis=1, keepdims=True)
        ix = jnp.min(jnp.where(logits == mx, lane_f, float(LANES)), axis=1, keepdims=True)
        vals.append(mx)
        idxs.append(ix)
        logits = jnp.where(lane_f == ix, -jnp.inf, logits)
    es = [jnp.exp(v - vals[0]) for v in vals]
    tot = es[0]
    for e in es[1:]:
        tot = tot + e
    idx_out = jnp.zeros((tm, LANES), jnp.float32)
    gate_out = jnp.zeros((tm, LANES), jnp.float32)
    for k in range(TOP_K):
        idx_out = jnp.where(lane == k, idxs[k], idx_out)
        gate_out = jnp.where(lane == k, es[k] / tot, gate_out)
    idx_ref[...] = idx_out.astype(jnp.int32)
    gate_ref[...] = gate_out


def _merge(y_att, y_m, og, x2, wa, wm, wo, ln_g, ln_b, wr, br, tm):
    T, D = x2.shape
    full = lambda shape: pl.BlockSpec(shape, lambda i: (0, 0))
    return pl.pallas_call(
        functools.partial(_merge_kernel, d_model=D),
        grid=(T // tm,),
        in_specs=[pl.BlockSpec((tm, ATT_WIDTH), lambda i: (i, 0)),
                  pl.BlockSpec((tm, MLSTM_WIDTH), lambda i: (i, 0)),
                  pl.BlockSpec((tm, 2 * D), lambda i: (i, 0)),
                  pl.BlockSpec((tm, D), lambda i: (i, 0)),
                  full(wa.shape), full(wm.shape), full(wo.shape),
                  full((1, D)), full((1, D)), full(wr.shape), full((1, LANES))],
        out_specs=[pl.BlockSpec((tm, D), lambda i: (i, 0)),
                   pl.BlockSpec((tm, D // 2), lambda i: (i, 0)),
                   pl.BlockSpec((tm, LANES), lambda i: (i, 0)),
                   pl.BlockSpec((tm, LANES), lambda i: (i, 0))],
        out_shape=[jax.ShapeDtypeStruct((T, D), jnp.float32),
                   jax.ShapeDtypeStruct((T, D // 2), jnp.uint32),
                   jax.ShapeDtypeStruct((T, LANES), jnp.int32),
                   jax.ShapeDtypeStruct((T, LANES), jnp.float32)],
        compiler_params=_cparams(("parallel",)),
    )(y_att, y_m, og, x2, wa, wm, wo, ln_g, ln_b, wr, br)


def _rank_kernel(idx_ref, rank_ref, cnt_ref, carry_ref):
    @pl.when(pl.program_id(0) == 0)
    def _():
        carry_ref[...] = jnp.zeros(carry_ref.shape, jnp.float32)

    tm = idx_ref.shape[0]
    lane = lax.broadcasted_iota(jnp.int32, (tm, LANES), 1)
    idx = idx_ref[...]
    hots = [lane == idx[:, k:k + 1] for k in range(TOP_K)]
    c = jnp.zeros((tm, LANES), jnp.float32)
    for hot in hots:
        c = c + jnp.where(hot, 1.0, 0.0)
    before = jnp.where(lax.broadcasted_iota(jnp.int32, (tm, tm), 0)
                       > lax.broadcasted_iota(jnp.int32, (tm, tm), 1), 1.0, 0.0).astype(MXU_DTYPE)
    carry = carry_ref[0:1, :]
    prior = jnp.dot(before, c.astype(MXU_DTYPE), preferred_element_type=jnp.float32) + carry
    out = jnp.zeros((tm, LANES), jnp.float32)
    for k, hot in enumerate(hots):
        out = jnp.where(lane == k, jnp.sum(jnp.where(hot, prior, 0.0), axis=1, keepdims=True), out)
    rank_ref[...] = out.astype(jnp.int32)
    total = carry + jnp.sum(c, axis=0, keepdims=True)
    carry_ref[...] = jnp.broadcast_to(total, carry_ref.shape)
    cnt_ref[...] = jnp.broadcast_to(total, cnt_ref.shape)


def _ranks(idx, tm):
    T = idx.shape[0]
    return pl.pallas_call(
        _rank_kernel,
        grid=(T // tm,),
        in_specs=[pl.BlockSpec((tm, LANES), lambda i: (i, 0))],
        out_specs=[pl.BlockSpec((tm, LANES), lambda i: (i, 0)),
                   pl.BlockSpec((SUBLANES, LANES), lambda i: (0, 0))],
        out_shape=[jax.ShapeDtypeStruct((T, LANES), jnp.int32),
                   jax.ShapeDtypeStruct((SUBLANES, LANES), jnp.float32)],
        scratch_shapes=[pltpu.VMEM((SUBLANES, LANES), jnp.float32)],
        compiler_params=_cparams(("arbitrary",)),
    )(idx)


def _row_copy(src_ref, s, dst_ref, d, sem):
    return pltpu.make_async_copy(src_ref.at[pl.ds(s, 1), :], dst_ref.at[pl.ds(d, 1), :], sem)


def _dispatch_kernel(grp_ref, dest_ref, x_ref, xs_ref, zero_ref, sem, zsem, *, bm):
    tm = x_ref.shape[0]

    @pl.when(pl.program_id(0) == 0)
    def _():
        zero_ref[...] = jnp.zeros(zero_ref.shape, zero_ref.dtype)

        def zero_copy(e):
            row = pl.multiple_of(grp_ref[0, e] - bm, bm)
            return pltpu.make_async_copy(zero_ref, xs_ref.at[pl.ds(row, bm), :], zsem)

        def zstart(e, carry):
            @pl.when(grp_ref[1, e] > 0)
            def _():
                zero_copy(e).start()
            return carry

        def zwait(e, carry):
            @pl.when(grp_ref[1, e] > 0)
            def _():
                zero_copy(e).wait()
            return carry

        lax.fori_loop(0, N_EXPERTS, zstart, 0)
        lax.fori_loop(0, N_EXPERTS, zwait, 0)

    def start(t, carry):
        for k in range(TOP_K):
            _row_copy(x_ref, t, xs_ref, dest_ref[t * TOP_K + k], sem).start(priority=k % 2)
        return carry

    lax.fori_loop(0, tm, start, 0)

    def wait(t, carry):
        for k in range(TOP_K):
            _row_copy(x_ref, 0, xs_ref, 0, sem).wait()
        return carry

    lax.fori_loop(0, tm, wait, 0)


def _dispatch(groups, dest_flat, xp, cap, tm, bm):
    T, W = xp.shape
    return pl.pallas_call(
        functools.partial(_dispatch_kernel, bm=bm),
        grid=(T // tm,),
        in_specs=[pl.BlockSpec(memory_space=pltpu.SMEM),
                  pl.BlockSpec((tm * TOP_K,), lambda i: (i,), memory_space=pltpu.SMEM),
                  pl.BlockSpec((tm, W), lambda i: (i, 0))],
        out_specs=pl.BlockSpec(memory_space=pl.ANY),
        out_shape=jax.ShapeDtypeStruct((cap, W), xp.dtype),
        scratch_shapes=[pltpu.VMEM((bm, W), xp.dtype), pltpu.SemaphoreType.DMA,
                        pltpu.SemaphoreType.DMA],
        compiler_params=_cparams(("arbitrary",)),
    )(groups, dest_flat, xp)


def _ffn_kernel(be_ref, nu_ref, xs_ref, wgu_ref, bgu_ref, wd_ref, bd_ref, y_ref, wgu_b, wd_b,
                *, d_ff):
    r = pl.program_id(0)
    e = be_ref[r]
    prev = be_ref[jnp.maximum(r - 1, 0)]

    @pl.when((r == 0) | (e != prev))
    def _():
        wgu_b[...] = wgu_ref[0].astype(MXU_DTYPE)
        wd_b[...] = wd_ref[0].astype(MXU_DTYPE)

    @pl.when(r < nu_ref[0])
    def _():
        w = xs_ref[...]
        half = w.shape[1]
        x_hi = pltpu.bitcast(w & jnp.uint32(0xFFFF0000), jnp.float32).astype(MXU_DTYPE)
        x_lo = pltpu.bitcast(w << 16, jnp.float32).astype(MXU_DTYPE)
        step = 512
        acc = bd_ref[0] + jnp.zeros(y_ref.shape, jnp.float32)
        for j in range(0, d_ff, step):
            def gu(lo):
                return (jnp.dot(x_hi, wgu_b[0:half, lo:lo + step], preferred_element_type=jnp.float32)
                        + jnp.dot(x_lo, wgu_b[half:2 * half, lo:lo + step], preferred_element_type=jnp.float32)
                        + bgu_ref[0, :, lo:lo + step])
            gate = jnp.minimum(gu(j), SWIGLU_LIMIT)
            up = jnp.clip(gu(d_ff + j), -SWIGLU_LIMIT, SWIGLU_LIMIT)
            act = (up + 1.0) * (gate * jax.nn.sigmoid(SWIGLU_ALPHA * gate))
            acc = acc + jnp.dot(act.astype(MXU_DTYPE), wd_b[j:j + step, :],
                                preferred_element_type=jnp.float32)
        y_ref[...] = acc

    @pl.when(r >= nu_ref[0])
    def _():
        y_ref[...] = jnp.zeros(y_ref.shape, jnp.float32)


def _expert_ffn(block_expert, n_used, xs, w_gate_up, b_gate_up, w_down, b_down, bm):
    cap, half = xs.shape
    E, D, F2 = w_gate_up.shape
    d_ff = F2 // 2
    grid_spec = pltpu.PrefetchScalarGridSpec(
        num_scalar_prefetch=2,
        grid=(cap // bm,),
        in_specs=[pl.BlockSpec((bm, half), lambda r, be, nu: (jnp.minimum(r, nu[0] - 1), 0)),
                  pl.BlockSpec((1, D, F2), lambda r, be, nu: (be[r], 0, 0)),
                  pl.BlockSpec((1, 1, F2), lambda r, be, nu: (be[r], 0, 0)),
                  pl.BlockSpec((1, d_ff, D), lambda r, be, nu: (be[r], 0, 0)),
                  pl.BlockSpec((1, 1, D), lambda r, be, nu: (be[r], 0, 0))],
        out_specs=pl.BlockSpec((bm, D), lambda r, be, nu: (r, 0)),
        scratch_shapes=[pltpu.VMEM((D, F2), MXU_DTYPE),
                        pltpu.VMEM((d_ff, D), MXU_DTYPE)],
    )
    return pl.pallas_call(
        functools.partial(_ffn_kernel, d_ff=d_ff),
        grid_spec=grid_spec,
        out_shape=jax.ShapeDtypeStruct((cap, D), jnp.float32),
        compiler_params=_cparams(("arbitrary",)),
    )(block_expert, n_used, xs, w_gate_up, b_gate_up.reshape(E, 1, F2), w_down,
      b_down.reshape(E, 1, D))


def _combine_kernel(dest_ref, gate_ref, x1_ref, yb_ref, lg_ref, lb_ref, o_ref, buf, sem):
    tm = x1_ref.shape[0]

    def start(t, carry):
        for k in range(TOP_K):
            _row_copy(yb_ref, dest_ref[t * TOP_K + k], buf.at[k], t, sem).start(priority=k % 2)
        return carry

    lax.fori_loop(0, tm, start, 0)

    def wait(t, carry):
        for k in range(TOP_K):
            _row_copy(yb_ref, 0, buf.at[k], 0, sem).wait()
        return carry

    lax.fori_loop(0, tm, wait, 0)
    y = gate_ref[:, 0:1] * buf[0]
    for k in range(1, TOP_K):
        y = y + gate_ref[:, k:k + 1] * buf[k]
    o_ref[...] = _layer_norm(DEEPNORM_ALPHA * x1_ref[...] + y, lg_ref[...], lb_ref[...])


def _combine(dest_flat, gates, x1, ybuf, ln_g, ln_b, tm):
    T, D = x1.shape
    return pl.pallas_call(
        _combine_kernel,
        grid=(T // tm,),
        in_specs=[pl.BlockSpec((tm * TOP_K,), lambda i: (i,), memory_space=pltpu.SMEM),
                  pl.BlockSpec((tm, LANES), lambda i: (i, 0)),
                  pl.BlockSpec((tm, D), lambda i: (i, 0)),
                  pl.BlockSpec(memory_space=pl.ANY),
                  pl.BlockSpec((1, D), lambda i: (0, 0)),
                  pl.BlockSpec((1, D), lambda i: (0, 0))],
        out_specs=pl.BlockSpec((tm, D), lambda i: (i, 0)),
        out_shape=jax.ShapeDtypeStruct((T, D), jnp.float32),
        scratch_shapes=[pltpu.VMEM((TOP_K, tm, D), jnp.float32), pltpu.SemaphoreType.DMA],
        compiler_params=_cparams(("arbitrary",)),
    )(dest_flat, gates, x1, ybuf, ln_g, ln_b)


def _tile(n, pref):
    t = min(n, pref)
    assert n % t == 0
    return t


def _relayout_w_in(w_in, d_model):
    sizes = (ATT_WIDTH, ATT_WIDTH, ATT_WIDTH, IDX_HEADS * IDX_HEAD_DIM, IDX_HEAD_DIM, IDX_HEADS,
             QK_M, QK_M, MLSTM_WIDTH, MLSTM_HEADS, MLSTM_HEADS, MLSTM_WIDTH, d_model, d_model)
    offs = [0]
    for s in sizes:
        offs.append(offs[-1] + s)
    seg = lambda k: w_in[:, offs[k]:offs[k + 1]]
    (q_a, k_a, v_a, q_i, k_i, w_i, q_m, k_m, v_m, i_m, f_m, o_m, g_a, g_m) = [seg(k) for k in range(14)]
    pad = jnp.zeros((w_in.shape[0], COLS_S - (IDX_HEAD_DIM + IDX_HEADS + 2 * MLSTM_HEADS)), w_in.dtype)
    cols = [q_a, k_a, v_a, q_i, k_i, w_i, i_m, f_m, pad, q_m, k_m, v_m, o_m, g_a, g_m, k_i, k_i]
    return jnp.concatenate(cols, axis=1).astype(MXU_DTYPE)


def _layer(x2, B, S, w_in, conv_w, conv_b, i_bias, f_bias, norm_g, w_branch_attn, w_branch_mlstm,
           w_out, ln1_g, ln1_b, w_router, b_router, w_gate_up, b_gate_up, w_down, b_down,
           ln2_g, ln2_b, rel_bias):
    T, D = x2.shape
    bf = MXU_DTYPE
    tq = _tile(S, 256)
    L = _tile(S, 256)
    tm = _tile(T, 256)

    oa, osm, om, og, okk = _project(x2, _relayout_w_in(w_in, D), tm)
    bias_t = _bias_tiles(rel_bias, tq)
    y_att = _dsa_attention(oa, osm, okk, bias_t, B, S, tq)

    gate_bias = jnp.zeros((1, COLS_S), jnp.float32)
    gate_bias = gate_bias.at[0, S_IM:S_IM + MLSTM_HEADS].set(i_bias)
    gate_bias = gate_bias.at[0, S_FM:S_FM + MLSTM_HEADS].set(f_bias)
    y_m = _mlstm(om, osm, conv_w, conv_b.reshape(1, -1), gate_bias, norm_g.reshape(1, -1), B, S, L)

    wr = jnp.zeros((D, LANES), bf).at[:, :N_EXPERTS].set(w_router.astype(bf))
    br = jnp.full((1, LANES), NEG_BIG, jnp.float32).at[0, :N_EXPERTS].set(b_router)
    x1, xp, idx, gates = _merge(y_att, y_m, og, x2, w_branch_attn.astype(bf), w_branch_mlstm.astype(bf),
                                w_out.astype(bf), ln1_g.reshape(1, D), ln1_b.reshape(1, D), wr, br, tm)

    bm = 512
    rank, cnt = _ranks(idx, _tile(T, 512))
    counts = cnt[0, :N_EXPERTS].astype(jnp.int32)
    padded = ((counts + bm - 1) // bm) * bm
    pend = jnp.cumsum(padded)
    pstart = pend - padded
    cap = ((T * TOP_K + bm - 1) // bm) * bm + N_EXPERTS * bm
    n_blocks = cap // bm
    experts = jnp.arange(N_EXPERTS, dtype=jnp.int32)
    sel = idx[:, :TOP_K, None] == experts
    dest = (jnp.sum(jnp.where(sel, pstart, 0), axis=-1) + rank[:, :TOP_K]).reshape(T * TOP_K)
    block_row = jnp.arange(n_blocks, dtype=jnp.int32) * bm
    block_expert = jnp.minimum(jnp.sum((pend[None, :] <= block_row[:, None]).astype(jnp.int32), axis=1),
                               N_EXPERTS - 1)
    n_used = (pend[-1:] // bm).astype(jnp.int32)

    xs = _dispatch(jnp.stack([pend, padded]), dest, xp, cap, _tile(T, 512), bm)
    ybuf = _expert_ffn(block_expert, n_used, xs, w_gate_up, b_gate_up, w_down, b_down, bm)
    return _combine(dest, gates, x1, ybuf, ln2_g.reshape(1, D), ln2_b.reshape(1, D), tm)


def kernel(x, w_in, conv_w, conv_b, mlstm_i_bias, mlstm_f_bias, mlstm_norm_g, w_branch_attn,
           w_branch_mlstm, w_out, ln1_g, ln1_b, w_router, b_router, w_gate_up, b_gate_up,
           w_down, b_down, ln2_g, ln2_b, rel_bias):
    B, S, D = x.shape
    x2 = x.reshape(B * S, D)
    for l in range(w_in.shape[0]):
        x2 = _layer(x2, B, S, w_in[l], conv_w[l], conv_b[l], mlstm_i_bias[l], mlstm_f_bias[l],
                    mlstm_norm_g[l], w_branch_attn[l], w_branch_mlstm[l], w_out[l], ln1_g[l], ln1_b[l],
                    w_router[l], b_router[l], w_gate_up[l], b_gate_up[l], w_down[l], b_down[l],
                    ln2_g[l], ln2_b[l], rel_bias)
    return x2.reshape(B, S, D)
```

```python
import functools
import math

import jax
import jax.numpy as jnp
from jax import lax
from jax.experimental import pallas as pl
from jax.experimental.pallas import tpu as pltpu

ATT_HEADS = 8
ATT_HEAD_DIM = 64
ATT_WIDTH = ATT_HEADS * ATT_HEAD_DIM
IDX_HEADS = 8
IDX_HEAD_DIM = 64
TOPK_MAX = 256
MLSTM_HEADS = 4
MLSTM_QK_DIM = 64
MLSTM_V_DIM = 128
MLSTM_WIDTH = MLSTM_HEADS * MLSTM_V_DIM
CONV_WIDTH = 4
N_BUCKETS = 32
MAX_DISTANCE = 128
N_EXPERTS = 32
TOP_K = 4
SWIGLU_ALPHA = 1.702
SWIGLU_LIMIT = 7.0
LN_EPS = 1e-5
DEPTH = 1
DEEPNORM_ALPHA = (2 * DEPTH) ** 0.25

LANES = 128
SUBLANES = 8
VMEM_LIMIT_BYTES = 56 * 1024 * 1024

MXU_DTYPE = jnp.bfloat16

INT_MIN = -(2 ** 31)
NEG_BIG = -1e30

QK_M = MLSTM_HEADS * MLSTM_QK_DIM
COLS_A = 3 * ATT_WIDTH + IDX_HEADS * IDX_HEAD_DIM
COLS_S = LANES
COLS_M = 2 * QK_M + 2 * MLSTM_WIDTH
COLS_K = 2 * IDX_HEAD_DIM
S_KI = 0
S_WI = IDX_HEAD_DIM
S_IM = S_WI + IDX_HEADS
S_FM = S_IM + MLSTM_HEADS


def _cparams(sem):
    return pltpu.CompilerParams(dimension_semantics=sem, vmem_limit_bytes=VMEM_LIMIT_BYTES)


def _proj_kernel(x_ref, w_ref, oa_ref, os_ref, om_ref, og_ref, ok_ref, *, d_model):
    xb = x_ref[...].astype(MXU_DTYPE)
    step = 512

    def mm(lo, hi):
        return jnp.dot(xb, w_ref[:, lo:hi], preferred_element_type=jnp.float32)

    base = 0
    for j in range(0, COLS_A, step):
        oa_ref[:, j:j + step] = mm(base + j, base + j + step).astype(MXU_DTYPE)
    base += COLS_A
    os_ref[...] = mm(base, base + COLS_S)
    base += COLS_S
    for j in range(0, COLS_M, step):
        om_ref[:, j:j + step] = mm(base + j, base + j + step)
    base += COLS_M
    for j in range(0, 2 * d_model, step):
        og_ref[:, j:j + step] = mm(base + j, base + j + step)
    base += 2 * d_model
    ok_ref[...] = mm(base, base + COLS_K).astype(MXU_DTYPE)


def _project(x2, w_p, tm):
    T, D = x2.shape
    n_all = w_p.shape[1]
    return pl.pallas_call(
        functools.partial(_proj_kernel, d_model=D),
        grid=(T // tm,),
        in_specs=[pl.BlockSpec((tm, D), lambda i: (i, 0)),
                  pl.BlockSpec((D, n_all), lambda i: (0, 0))],
        out_specs=[pl.BlockSpec((tm, COLS_A), lambda i: (i, 0)),
                   pl.BlockSpec((tm, COLS_S), lambda i: (i, 0)),
                   pl.BlockSpec((tm, COLS_M), lambda i: (i, 0)),
                   pl.BlockSpec((tm, 2 * D), lambda i: (i, 0)),
                   pl.BlockSpec((tm, COLS_K), lambda i: (i, 0))],
        out_shape=[jax.ShapeDtypeStruct((T, COLS_A), MXU_DTYPE),
                   jax.ShapeDtypeStruct((T, COLS_S), jnp.float32),
                   jax.ShapeDtypeStruct((T, COLS_M), jnp.float32),
                   jax.ShapeDtypeStruct((T, 2 * D), jnp.float32),
                   jax.ShapeDtypeStruct((T, COLS_K), MXU_DTYPE)],
        compiler_params=_cparams(("parallel",)),
    )(x2, w_p)


def _bias_kernel(rb_ref, o_ref, *, tq):
    h = pl.program_id(0)
    s = lax.broadcasted_iota(jnp.int32, (tq, tq), 0)
    t = lax.broadcasted_iota(jnp.int32, (tq, tq), 1)
    max_exact = N_BUCKETS // 2
    for d in range(3):
        n = jnp.maximum(t - s + d * tq, 0)
        n_f = jnp.maximum(n, 1).astype(jnp.float32)
        large = max_exact + (jnp.log(n_f / max_exact) / math.log(MAX_DISTANCE / max_exact)
                             * (N_BUCKETS - max_exact)).astype(jnp.int32)
        large = jnp.minimum(large, N_BUCKETS - 1)
        bucket = jnp.where(n < max_exact, n, large)
        acc = jnp.zeros((tq, tq), jnp.float32)
        for k in range(N_BUCKETS):
            acc = jnp.where(bucket == k, rb_ref[h, k], acc)
        o_ref[0, d] = acc


def _bias_tiles(rel_bias, tq):
    assert tq + 1 >= MAX_DISTANCE
    H = rel_bias.shape[0]
    return pl.pallas_call(
        functools.partial(_bias_kernel, tq=tq),
        grid=(H,),
        in_specs=[pl.BlockSpec(memory_space=pltpu.SMEM)],
        out_specs=pl.BlockSpec((1, 3, tq, tq), lambda h: (h, 0, 0, 0)),
        out_shape=jax.ShapeDtypeStruct((H, 3, tq, tq), jnp.float32),
        compiler_params=_cparams(("parallel",)),
    )(rel_bias)


def _dsa_kernel(qa_ref, ka_ref, vt_ref, qi_ref, kk_ref, wq_ref, bias_ref, o_ref,
                keys_ref, hi_ref, lo_ref, msk_ref, qz_ref, qiz_ref, s_ref, p_ref, acc_ref, m_ref, l_ref,
                *, tq, n_sel):
    i = pl.program_id(1)
    nch = i + 1
    t_pos = i * tq + lax.broadcasted_iota(jnp.int32, (1, tq), 1)
    s_loc = lax.broadcasted_iota(jnp.int32, (tq, 1), 0)
    hd = ATT_HEAD_DIM
    n_pairs = ATT_HEADS // 2
    assert 2 * hd == LANES and IDX_HEAD_DIM == hd and IDX_HEADS == ATT_HEADS

    top = lax.broadcasted_iota(jnp.int32, (LANES, tq), 0) < hd

    def pair_operand(blk):
        bt = blk.astype(jnp.float32).T
        return jnp.concatenate([jnp.where(top, bt, 0.0), jnp.where(top, 0.0, bt)],
                               axis=1).astype(MXU_DTYPE)

    for p in range(n_pairs):
        pair = slice(p * LANES, (p + 1) * LANES)
        qz_ref[p] = pair_operand(qa_ref[:, pair] * (hd ** -0.5))
        qiz_ref[p] = pair_operand(qi_ref[:, pair])

    ws = wq_ref[...].T[S_WI:S_WI + IDX_HEADS, :] * (IDX_HEADS ** -0.5)

    def score_body(c, carry):
        off = pl.multiple_of(c * tq, tq)
        kk = kk_ref[pl.ds(off, tq), :]
        sc = jnp.zeros((tq, tq), jnp.float32)
        for p in range(n_pairs):
            d = jnp.dot(kk, qiz_ref[p], preferred_element_type=jnp.float32)
            for j in range(2):
                h = 2 * p + j
                sc = sc + ws[h:h + 1, :] * jnp.maximum(d[:, j * tq:(j + 1) * tq], 0.0)
        sc = sc + 0.0
        b = pltpu.bitcast(sc, jnp.int32)
        sk = b ^ ((b >> 31) & jnp.int32(0x7FFFFFFF))
        sk = jnp.where(off + s_loc <= t_pos, sk, jnp.int32(INT_MIN))
        keys_ref[c] = sk
        hi_ref[c] = (sk >> 16).astype(jnp.int16)
        return carry

    lax.fori_loop(0, nch, score_body, 0)

    pack = 2 * SUBLANES
    half_min = -(2 ** 15)

    def count16(ref, pred_fn):
        def cnt_body(c, acc):
            hit = jnp.where(pred_fn(ref[c]), jnp.int16(1), jnp.int16(0))
            parts = [hit[r * pack:(r + 1) * pack, :] for r in range(tq // pack)]
            while len(parts) > 1:
                parts = [a + b for a, b in zip(parts[::2], parts[1::2])]
            return acc + parts[0]
        acc = lax.fori_loop(0, nch, cnt_body, jnp.zeros((pack, tq), jnp.int16))
        return jnp.sum(acc.astype(jnp.float32), axis=0, keepdims=True)

    def search16(ref, target):
        def bit_body(it, u):
            cand_u = u | lax.shift_left(jnp.int32(1), 15 - it)
            cand = (cand_u + half_min).astype(jnp.int16)
            return jnp.where(count16(ref, lambda x: x >= cand) >= target, cand_u, u)
        return lax.fori_loop(0, 16, bit_body, jnp.zeros((1, tq), jnp.int32))

    u_hi = search16(hi_ref, float(n_sel))
    thr_hi = (u_hi + half_min).astype(jnp.int16)
    need_lo = n_sel - count16(hi_ref, lambda x: x > thr_hi)

    def low_body(c, carry):
        lo = ((keys_ref[c] & 0xFFFF) + half_min).astype(jnp.int16)
        lo_ref[c] = jnp.where(hi_ref[c] == thr_hi, lo, jnp.int16(half_min))
        return carry

    lax.fori_loop(0, nch, low_body, 0)
    u_lo = search16(lo_ref, need_lo)
    thr_lo = (u_lo + half_min).astype(jnp.int16)
    thr = lax.shift_left(u_hi + half_min, 16) | u_lo
    need = jnp.where(thr == INT_MIN, 0.0, need_lo - count16(lo_ref, lambda x: x > thr_lo))
    n_eq = count16(lo_ref, lambda x: x == thr_lo)
    surplus = jnp.where((thr != INT_MIN) & (n_eq > need), 1.0, 0.0)
    has_surplus = jnp.max(surplus) > 0.0

    @pl.when(has_surplus)
    def _():
        tri = jnp.where(lax.broadcasted_iota(jnp.int32, (tq, tq), 0)
                        >= lax.broadcasted_iota(jnp.int32, (tq, tq), 1), 1.0, 0.0).astype(MXU_DTYPE)

        def mask_body(c, run):
            kc = keys_ref[c]
            eq = kc == thr
            eqf = jnp.where(eq, 1.0, 0.0)
            prefix = jnp.dot(tri, eqf.astype(MXU_DTYPE), preferred_element_type=jnp.float32) + run
            msk_ref[c] = jnp.where(kc > thr, 0.0,
                                   jnp.where(eq, jnp.where(prefix <= need, 0.0, NEG_BIG), NEG_BIG))
            return run + jnp.sum(eqf, axis=0, keepdims=True)

        lax.fori_loop(0, nch, mask_body, jnp.zeros((1, tq), jnp.float32))

    @pl.when(jnp.logical_not(has_surplus))
    def _():
        floor = jnp.maximum(thr, INT_MIN + 1)

        def mask_body(c, carry):
            msk_ref[c] = jnp.where(keys_ref[c] >= floor, 0.0, NEG_BIG)
            return carry

        lax.fori_loop(0, nch, mask_body, 0)

    m_ref[...] = jnp.full(m_ref.shape, NEG_BIG, jnp.float32)
    l_ref[...] = jnp.zeros(l_ref.shape, jnp.float32)
    acc_ref[...] = jnp.zeros(acc_ref.shape, jnp.float32)

    def att_step(c, far):
        off = pl.multiple_of(c * tq, tq)
        mk = msk_ref[c]
        tile_max = []
        for p in range(n_pairs):
            pair = slice(p * LANES, (p + 1) * LANES)
            s2 = jnp.dot(ka_ref[pl.ds(off, tq), pair], qz_ref[p], preferred_element_type=jnp.float32)
            for j in range(2):
                h = 2 * p + j
                s = s2[:, j * tq:(j + 1) * tq] + mk
                if not far:
                    s = s + bias_ref[h, i - c]
                s_ref[h] = s
                tile_max.append(jnp.max(s, axis=0, keepdims=True))
        alphas = []
        for h in range(ATT_HEADS):
            m_prev = m_ref[h:h + 1, :]
            if far:
                b_far = bias_ref[h, 2, 0:1, 0:1]
                m_new = jnp.maximum(m_prev, tile_max[h] + b_far)
                shift = m_new - b_far
            else:
                m_new = jnp.maximum(m_prev, tile_max[h])
                shift = m_new
            alpha = jnp.exp(m_prev - m_new)
            pe = jnp.exp(s_ref[h] - shift)
            l_ref[h:h + 1, :] = alpha * l_ref[h:h + 1, :] + jnp.sum(pe, axis=0, keepdims=True)
            m_ref[h:h + 1, :] = m_new
            p_ref[h // 2, :, (h % 2) * tq:(h % 2 + 1) * tq] = pe.astype(MXU_DTYPE)
            alphas.append(alpha)
        for p in range(n_pairs):
            pv = jnp.dot(vt_ref[c, p * LANES:(p + 1) * LANES, :], p_ref[p],
                         preferred_element_type=jnp.float32)
            for j in range(2):
                h = 2 * p + j
                acc_ref[h] = alphas[h] * acc_ref[h] + pv[j * hd:(j + 1) * hd, j * tq:(j + 1) * tq]

    def far_body(c, carry):
        att_step(c, True)
        return carry

    def near_body(c, carry):
        att_step(c, False)
        return carry

    n_far = jnp.maximum(i - 1, 0)
    lax.fori_loop(0, n_far, far_body, 0)
    lax.fori_loop(n_far, nch, near_body, 0)
    y_t = jnp.concatenate([acc_ref[h] / l_ref[h:h + 1, :] for h in range(ATT_HEADS)], axis=0)
    o_ref[...] = y_t.T.astype(o_ref.dtype)


def _dsa_attention(oa, osm, okk, bias_t, B, S, tq):
    nq = S // tq
    n_sel = min(TOPK_MAX, S // 4)
    W = ATT_WIDTH
    v_t = oa[:, 2 * W:3 * W].reshape(B * nq, tq, W).transpose(0, 2, 1)
    return pl.pallas_call(
        functools.partial(_dsa_kernel, tq=tq, n_sel=n_sel),
        grid=(B, nq),
        in_specs=[pl.BlockSpec((tq, W), lambda b, i: (b * nq + i, 0)),
                  pl.BlockSpec((S, W), lambda b, i: (b, 1)),
                  pl.BlockSpec((nq, W, tq), lambda b, i: (b, 0, 0)),
                  pl.BlockSpec((tq, W), lambda b, i: (b * nq + i, 3)),
                  pl.BlockSpec((S, COLS_K), lambda b, i: (b, 0)),
                  pl.BlockSpec((tq, COLS_S), lambda b, i: (b * nq + i, 0)),
                  pl.BlockSpec((ATT_HEADS, 3, tq, tq), lambda b, i: (0, 0, 0, 0))],
        out_specs=pl.BlockSpec((tq, W), lambda b, i: (b * nq + i, 0)),
        out_shape=jax.ShapeDtypeStruct((B * S, W), MXU_DTYPE),
        scratch_shapes=[pltpu.VMEM((nq, tq, tq), jnp.int32),
                        pltpu.VMEM((nq, tq, tq), jnp.int16),
                        pltpu.VMEM((nq, tq, tq), jnp.int16),
                        pltpu.VMEM((nq, tq, tq), jnp.float32),
                        pltpu.VMEM((ATT_HEADS // 2, LANES, 2 * tq), MXU_DTYPE),
                        pltpu.VMEM((IDX_HEADS // 2, LANES, 2 * tq), MXU_DTYPE),
                        pltpu.VMEM((ATT_HEADS, tq, tq), jnp.float32),
                        pltpu.VMEM((ATT_HEADS // 2, tq, 2 * tq), MXU_DTYPE),
                        pltpu.VMEM((ATT_HEADS, ATT_HEAD_DIM, tq), jnp.float32),
                        pltpu.VMEM((ATT_HEADS, tq), jnp.float32),
                        pltpu.VMEM((ATT_HEADS, tq), jnp.float32)],
        compiler_params=_cparams(("parallel", "arbitrary")),
    )(oa, oa, v_t, oa, okk, osm, bias_t)


def _mlstm_kernel(qk_ref, v_ref, og_ref, s_ref, cw_ref, cb_ref, gb_ref, ng_ref, y_ref,
                  xbuf, cst, mst, *, L):
    c = pl.program_id(1)
    H, DK, DV = MLSTM_HEADS, MLSTM_QK_DIM, MLSTM_V_DIM
    tail = SUBLANES

    @pl.when(c == 0)
    def _():
        xbuf[0:tail, :] = jnp.zeros((tail, 2 * QK_M), jnp.float32)
        cst[...] = jnp.zeros(cst.shape, jnp.float32)
        mst[...] = jnp.zeros(mst.shape, jnp.float32)

    @pl.when(c > 0)
    def _():
        xbuf[0:tail, :] = xbuf[L:L + tail, :]

    xbuf[tail:tail + L, :] = qk_ref[...]
    conv = cb_ref[...] + jnp.zeros((L, 2 * QK_M), jnp.float32)
    for j in range(CONV_WIDTH):
        conv = conv + cw_ref[j:j + 1, :] * xbuf[pl.ds(tail - (CONV_WIDTH - 1) + j, L), :]
    qk = conv * jax.nn.sigmoid(conv)
    q = qk[:, :QK_M]
    kT = (qk[:, QK_M:] * (DK ** -0.5)).T

    g = s_ref[...] + gb_ref[...]
    logf = jnp.minimum(g, 0.0) - jnp.log(1.0 + jnp.exp(-jnp.abs(g)))
    row = lax.broadcasted_iota(jnp.int32, (L, L), 0)
    col = lax.broadcasted_iota(jnp.int32, (L, L), 1)
    causal = row >= col
    bcum = jnp.dot(jnp.where(causal, 1.0, 0.0), logf, precision=lax.Precision.HIGHEST,
                   preferred_element_type=jnp.float32)
    gT = g.T
    bT = bcum.T
    ones_col = jnp.where(lax.broadcasted_iota(jnp.int32, (L, LANES), 1) == 0, 1.0, 0.0
                         ).astype(MXU_DTYPE)

    for h in range(H):
        bc = bcum[:, S_FM + h:S_FM + h + 1]
        br = bT[S_FM + h:S_FM + h + 1, :]
        lir = gT[S_IM + h:S_IM + h + 1, :]
        m0 = mst[h:h + 1, 0:1]
        d = jnp.where(causal, bc - br + lir, -jnp.inf)
        m_inter = bc + m0
        m_t = jnp.maximum(m_inter, jnp.max(d, axis=1, keepdims=True))
        qh = q[:, h * DK:(h + 1) * DK].astype(MXU_DTYPE)
        kTh = kT[h * DK:(h + 1) * DK, :]
        p = jnp.dot(qh, kTh.astype(MXU_DTYPE), preferred_element_type=jnp.float32) * jnp.exp(d - m_t)
        sc = jnp.exp(m_inter - m_t)
        vh = v_ref[:, h * DV:(h + 1) * DV].astype(MXU_DTYPE)
        caug = cst[h]
        inter = jnp.dot(qh, caug.astype(MXU_DTYPE), preferred_element_type=jnp.float32)
        num = jnp.dot(p.astype(MXU_DTYPE), vh, preferred_element_type=jnp.float32) + sc * inter[:, :DV]
        den = jnp.sum(p, axis=1, keepdims=True) + sc * inter[:, DV:DV + 1]
        hh = num / jnp.maximum(jnp.abs(den), jnp.exp(-m_t))
        b_end = bc[L - 1:L, :]
        a = b_end - br + lir
        m_loc = jnp.max(a, axis=1, keepdims=True)
        m_new = jnp.maximum(b_end + m0, m_loc)
        kTw = (kTh * jnp.exp(a - m_new)).astype(MXU_DTYPE)
        vaug = jnp.concatenate([vh, ones_col], axis=1)
        cst[h] = jnp.exp(b_end + m0 - m_new) * caug + jnp.dot(kTw, vaug, preferred_element_type=jnp.float32)
        mst[h:h + 1, :] = jnp.broadcast_to(m_new, (1, LANES))
        mu = jnp.mean(hh, axis=1, keepdims=True)
        hc = hh - mu
        var = jnp.mean(hc * hc, axis=1, keepdims=True)
        hn = hc * lax.rsqrt(var + LN_EPS) * ng_ref[:, h * DV:(h + 1) * DV]
        y_ref[:, h * DV:(h + 1) * DV] = (hn * jax.nn.sigmoid(og_ref[:, h * DV:(h + 1) * DV])).astype(y_ref.dtype)


def _mlstm(om, osm, conv_w, conv_b, gate_bias, norm_g, B, S, L):
    nc = S // L
    W = MLSTM_WIDTH
    assert 2 * QK_M == W
    return pl.pallas_call(
        functools.partial(_mlstm_kernel, L=L),
        grid=(B, nc),
        in_specs=[pl.BlockSpec((L, W), lambda b, c: (b * nc + c, 0)),
                  pl.BlockSpec((L, W), lambda b, c: (b * nc + c, 1)),
                  pl.BlockSpec((L, W), lambda b, c: (b * nc + c, 2)),
                  pl.BlockSpec((L, COLS_S), lambda b, c: (b * nc + c, 0)),
                  pl.BlockSpec((CONV_WIDTH, W), lambda b, c: (0, 0)),
                  pl.BlockSpec((1, W), lambda b, c: (0, 0)),
                  pl.BlockSpec((1, COLS_S), lambda b, c: (0, 0)),
                  pl.BlockSpec((1, W), lambda b, c: (0, 0))],
        out_specs=pl.BlockSpec((L, W), lambda b, c: (b * nc + c, 0)),
        out_shape=jax.ShapeDtypeStruct((B * S, W), MXU_DTYPE),
        scratch_shapes=[pltpu.VMEM((L + 2 * SUBLANES, W), jnp.float32),
                        pltpu.VMEM((MLSTM_HEADS, MLSTM_QK_DIM, 2 * LANES), jnp.float32),
                        pltpu.VMEM((SUBLANES, LANES), jnp.float32)],
        compiler_params=_cparams(("parallel", "arbitrary")),
    )(om, om, om, osm, conv_w, conv_b, gate_bias, norm_g)


def _layer_norm(z, g, b):
    mu = jnp.mean(z, axis=1, keepdims=True)
    zc = z - mu
    var = jnp.mean(zc * zc, axis=1, keepdims=True)
    return zc * lax.rsqrt(var + LN_EPS) * g + b


def _merge_kernel(ya_ref, ym_ref, g_ref, x_ref, wa_ref, wm_ref, wo_ref, lg_ref, lb_ref,
                  wr_ref, br_ref, x1_ref, xp_ref, idx_ref, gate_ref, *, d_model):
    D = d_model
    mix = (jax.nn.sigmoid(g_ref[:, :D]) * jnp.dot(ya_ref[...], wa_ref[...], preferred_element_type=jnp.float32)
           + jax.nn.sigmoid(g_ref[:, D:]) * jnp.dot(ym_ref[...], wm_ref[...], preferred_element_type=jnp.float32))
    y = jnp.dot(mix.astype(MXU_DTYPE), wo_ref[...], preferred_element_type=jnp.float32)
    x1 = _layer_norm(DEEPNORM_ALPHA * x_ref[...] + y, lg_ref[...], lb_ref[...])
    x1_ref[...] = x1
    xb = x1.astype(MXU_DTYPE)
    bits = pltpu.bitcast(x1.astype(jnp.bfloat16).astype(jnp.float32), jnp.uint32)
    xp_ref[...] = (bits[:, :D // 2] & jnp.uint32(0xFFFF0000)) | (bits[:, D // 2:] >> 16)

    logits = jnp.dot(xb, wr_ref[...], preferred_element_type=jnp.float32) + br_ref[...]
    tm = logits.shape[0]
    lane = lax.broadcasted_iota(jnp.int32, (tm, LANES), 1)
    lane_f = lane.astype(jnp.float32)
    vals, idxs = [], []
    for _ in range(TOP_K):
        mx = jnp.max(logits, axis=1, keepdims=True)
        ix = jnp.min(jnp.where(logits == mx, lane_f, float(LANES)), axis=1, keepdims=True)
        vals.append(mx)
        idxs.append(ix)
        logits = jnp.where(lane_f == ix, -jnp.inf, logits)
    es = [jnp.exp(v - vals[0]) for v in vals]
    tot = es[0]
    for e in es[1:]:
        tot = tot + e
    idx_out = jnp.zeros((tm, LANES), jnp.float32)
    gate_out = jnp.zeros((tm, LANES), jnp.float32)
    for k in range(TOP_K):
        idx_out = jnp.where(lane == k, idxs[k], idx_out)
        gate_out = jnp.where(lane == k, es[k] / tot, gate_out)
    idx_ref[...] = idx_out.astype(jnp.int32)
    gate_ref[...] = gate_out


def _merge(y_att, y_m, og, x2, wa, wm, wo, ln_g, ln_b, wr, br, tm):
    T, D = x2.shape
    full = lambda shape: pl.BlockSpec(shape, lambda i: (0, 0))
    return pl.pallas_call(
        functools.partial(_merge_kernel, d_model=D),
        grid=(T // tm,),
        in_specs=[pl.BlockSpec((tm, ATT_WIDTH), lambda i: (i, 0)),
                  pl.BlockSpec((tm, MLSTM_WIDTH), lambda i: (i, 0)),
                  pl.BlockSpec((tm, 2 * D), lambda i: (i, 0)),
                  pl.BlockSpec((tm, D), lambda i: (i, 0)),
                  full(wa.shape), full(wm.shape), full(wo.shape),
                  full((1, D)), full((1, D)), full(wr.shape), full((1, LANES))],
        out_specs=[pl.BlockSpec((tm, D), lambda i: (i, 0)),
                   pl.BlockSpec((tm, D // 2), lambda i: (i, 0)),
                   pl.BlockSpec((tm, LANES), lambda i: (i, 0)),
                   pl.BlockSpec((tm, LANES), lambda i: (i, 0))],
        out_shape=[jax.ShapeDtypeStruct((T, D), jnp.float32),
                   jax.ShapeDtypeStruct((T, D // 2), jnp.uint32),
                   jax.ShapeDtypeStruct((T, LANES), jnp.int32),
                   jax.ShapeDtypeStruct((T, LANES), jnp.float32)],
        compiler_params=_cparams(("parallel",)),
    )(y_att, y_m, og, x2, wa, wm, wo, ln_g, ln_b, wr, br)


def _rank_kernel(idx_ref, rank_ref, cnt_ref, carry_ref):
    @pl.when(pl.program_id(0) == 0)
    def _():
        carry_ref[...] = jnp.zeros(carry_ref.shape, jnp.float32)

    tm = idx_ref.shape[0]
    lane = lax.broadcasted_iota(jnp.int32, (tm, LANES), 1)
    idx = idx_ref[...]
    hots = [lane == idx[:, k:k + 1] for k in range(TOP_K)]
    c = jnp.zeros((tm, LANES), jnp.float32)
    for hot in hots:
        c = c + jnp.where(hot, 1.0, 0.0)
    before = jnp.where(lax.broadcasted_iota(jnp.int32, (tm, tm), 0)
                       > lax.broadcasted_iota(jnp.int32, (tm, tm), 1), 1.0, 0.0).astype(MXU_DTYPE)
    carry = carry_ref[0:1, :]
    prior = jnp.dot(before, c.astype(MXU_DTYPE), preferred_element_type=jnp.float32) + carry
    out = jnp.zeros((tm, LANES), jnp.float32)
    for k, hot in enumerate(hots):
        out = jnp.where(lane == k, jnp.sum(jnp.where(hot, prior, 0.0), axis=1, keepdims=True), out)
    rank_ref[...] = out.astype(jnp.int32)
    total = carry + jnp.sum(c, axis=0, keepdims=True)
    carry_ref[...] = jnp.broadcast_to(total, carry_ref.shape)
    cnt_ref[...] = jnp.broadcast_to(total, cnt_ref.shape)


def _ranks(idx, tm):
    T = idx.shape[0]
    return pl.pallas_call(
        _rank_kernel,
        grid=(T // tm,),
        in_specs=[pl.BlockSpec((tm, LANES), lambda i: (i, 0))],
        out_specs=[pl.BlockSpec((tm, LANES), lambda i: (i, 0)),
                   pl.BlockSpec((SUBLANES, LANES), lambda i: (0, 0))],
        out_shape=[jax.ShapeDtypeStruct((T, LANES), jnp.int32),
                   jax.ShapeDtypeStruct((SUBLANES, LANES), jnp.float32)],
        scratch_shapes=[pltpu.VMEM((SUBLANES, LANES), jnp.float32)],
        compiler_params=_cparams(("arbitrary",)),
    )(idx)


def _row_copy(src_ref, s, dst_ref, d, sem):
    return pltpu.make_async_copy(src_ref.at[pl.ds(s, 1), :], dst_ref.at[pl.ds(d, 1), :], sem)


def _dispatch_kernel(grp_ref, dest_ref, x_ref, xs_ref, zero_ref, sem, zsem, *, bm):
    tm = x_ref.shape[0]

    @pl.when(pl.program_id(0) == 0)
    def _():
        zero_ref[...] = jnp.zeros(zero_ref.shape, zero_ref.dtype)

        def zero_copy(e):
            row = pl.multiple_of(grp_ref[0, e] - bm, bm)
            return pltpu.make_async_copy(zero_ref, xs_ref.at[pl.ds(row, bm), :], zsem)

        def zstart(e, carry):
            @pl.when(grp_ref[1, e] > 0)
            def _():
                zero_copy(e).start()
            return carry

        def zwait(e, carry):
            @pl.when(grp_ref[1, e] > 0)
            def _():
                zero_copy(e).wait()
            return carry

        lax.fori_loop(0, N_EXPERTS, zstart, 0)
        lax.fori_loop(0, N_EXPERTS, zwait, 0)

    def start(t, carry):
        for k in range(TOP_K):
            _row_copy(x_ref, t, xs_ref, dest_ref[t * TOP_K + k], sem).start(priority=k % 2)
        return carry

    lax.fori_loop(0, tm, start, 0)

    def wait(t, carry):
        for k in range(TOP_K):
            _row_copy(x_ref, 0, xs_ref, 0, sem).wait()
        return carry

    lax.fori_loop(0, tm, wait, 0)


def _dispatch(groups, dest_flat, xp, cap, tm, bm):
    T, W = xp.shape
    return pl.pallas_call(
        functools.partial(_dispatch_kernel, bm=bm),
        grid=(T // tm,),
        in_specs=[pl.BlockSpec(memory_space=pltpu.SMEM),
                  pl.BlockSpec((tm * TOP_K,), lambda i: (i,), memory_space=pltpu.SMEM),
                  pl.BlockSpec((tm, W), lambda i: (i, 0))],
        out_specs=pl.BlockSpec(memory_space=pl.ANY),
        out_shape=jax.ShapeDtypeStruct((cap, W), xp.dtype),
        scratch_shapes=[pltpu.VMEM((bm, W), xp.dtype), pltpu.SemaphoreType.DMA,
                        pltpu.SemaphoreType.DMA],
        compiler_params=_cparams(("arbitrary",)),
    )(groups, dest_flat, xp)


def _ffn_kernel(be_ref, nu_ref, xs_ref, wgu_ref, bgu_ref, wd_ref, bd_ref, y_ref, wgu_b, wd_b,
                *, d_ff):
    r = pl.program_id(0)
    e = be_ref[r]
    prev = be_ref[jnp.maximum(r - 1, 0)]

    @pl.when((r == 0) | (e != prev))
    def _():
        wgu_b[...] = wgu_ref[0].astype(MXU_DTYPE)
        wd_b[...] = wd_ref[0].astype(MXU_DTYPE)

    @pl.when(r < nu_ref[0])
    def _():
        w = xs_ref[...]
        half = w.shape[1]
        x_hi = pltpu.bitcast(w & jnp.uint32(0xFFFF0000), jnp.float32).astype(MXU_DTYPE)
        x_lo = pltpu.bitcast(w << 16, jnp.float32).astype(MXU_DTYPE)
        step = 512
        acc = bd_ref[0] + jnp.zeros(y_ref.shape, jnp.float32)
        for j in range(0, d_ff, step):
            def gu(lo):
                return (jnp.dot(x_hi, wgu_b[0:half, lo:lo + step], preferred_element_type=jnp.float32)
                        + jnp.dot(x_lo, wgu_b[half:2 * half, lo:lo + step], preferred_element_type=jnp.float32)
                        + bgu_ref[0, :, lo:lo + step])
            gate = jnp.minimum(gu(j), SWIGLU_LIMIT)
            up = jnp.clip(gu(d_ff + j), -SWIGLU_LIMIT, SWIGLU_LIMIT)
            act = (up + 1.0) * (gate * jax.nn.sigmoid(SWIGLU_ALPHA * gate))
            acc = acc + jnp.dot(act.astype(MXU_DTYPE), wd_b[j:j + step, :],
                                preferred_element_type=jnp.float32)
        y_ref[...] = acc

    @pl.when(r >= nu_ref[0])
    def _():
        y_ref[...] = jnp.zeros(y_ref.shape, jnp.float32)


def _expert_ffn(block_expert, n_used, xs, w_gate_up, b_gate_up, w_down, b_down, bm):
    cap, half = xs.shape
    E, D, F2 = w_gate_up.shape
    d_ff = F2 // 2
    grid_spec = pltpu.PrefetchScalarGridSpec(
        num_scalar_prefetch=2,
        grid=(cap // bm,),
        in_specs=[pl.BlockSpec((bm, half), lambda r, be, nu: (jnp.minimum(r, nu[0] - 1), 0)),
                  pl.BlockSpec((1, D, F2), lambda r, be, nu: (be[r], 0, 0)),
                  pl.BlockSpec((1, 1, F2), lambda r, be, nu: (be[r], 0, 0)),
                  pl.BlockSpec((1, d_ff, D), lambda r, be, nu: (be[r], 0, 0)),
                  pl.BlockSpec((1, 1, D), lambda r, be, nu: (be[r], 0, 0))],
        out_specs=pl.BlockSpec((bm, D), lambda r, be, nu: (r, 0)),
        scratch_shapes=[pltpu.VMEM((D, F2), MXU_DTYPE),
                        pltpu.VMEM((d_ff, D), MXU_DTYPE)],
    )
    return pl.pallas_call(
        functools.partial(_ffn_kernel, d_ff=d_ff),
        grid_spec=grid_spec,
        out_shape=jax.ShapeDtypeStruct((cap, D), jnp.float32),
        compiler_params=_cparams(("arbitrary",)),
    )(block_expert, n_used, xs, w_gate_up, b_gate_up.reshape(E, 1, F2), w_down,
      b_down.reshape(E, 1, D))


def _combine_kernel(dest_ref, gate_ref, x1_ref, yb_ref, lg_ref, lb_ref, o_ref, buf, sem):
    tm = x1_ref.shape[0]

    def start(t, carry):
        for k in range(TOP_K):
            _row_copy(yb_ref, dest_ref[t * TOP_K + k], buf.at[k], t, sem).start(priority=k % 2)
        return carry

    lax.fori_loop(0, tm, start, 0)

    def wait(t, carry):
        for k in range(TOP_K):
            _row_copy(yb_ref, 0, buf.at[k], 0, sem).wait()
        return carry

    lax.fori_loop(0, tm, wait, 0)
    y = gate_ref[:, 0:1] * buf[0]
    for k in range(1, TOP_K):
        y = y + gate_ref[:, k:k + 1] * buf[k]
    o_ref[...] = _layer_norm(DEEPNORM_ALPHA * x1_ref[...] + y, lg_ref[...], lb_ref[...])


def _combine(dest_flat, gates, x1, ybuf, ln_g, ln_b, tm):
    T, D = x1.shape
    return pl.pallas_call(
        _combine_kernel,
        grid=(T // tm,),
        in_specs=[pl.BlockSpec((tm * TOP_K,), lambda i: (i,), memory_space=pltpu.SMEM),
                  pl.BlockSpec((tm, LANES), lambda i: (i, 0)),
                  pl.BlockSpec((tm, D), lambda i: (i, 0)),
                  pl.BlockSpec(memory_space=pl.ANY),
                  pl.BlockSpec((1, D), lambda i: (0, 0)),
                  pl.BlockSpec((1, D), lambda i: (0, 0))],
        out_specs=pl.BlockSpec((tm, D), lambda i: (i, 0)),
        out_shape=jax.ShapeDtypeStruct((T, D), jnp.float32),
        scratch_shapes=[pltpu.VMEM((TOP_K, tm, D), jnp.float32), pltpu.SemaphoreType.DMA],
        compiler_params=_cparams(("arbitrary",)),
    )(dest_flat, gates, x1, ybuf, ln_g, ln_b)


def _tile(n, pref):
    t = min(n, pref)
    assert n % t == 0
    return t


def _relayout_w_in(w_in, d_model):
    sizes = (ATT_WIDTH, ATT_WIDTH, ATT_WIDTH, IDX_HEADS * IDX_HEAD_DIM, IDX_HEAD_DIM, IDX_HEADS,
             QK_M, QK_M, MLSTM_WIDTH, MLSTM_HEADS, MLSTM_HEADS, MLSTM_WIDTH, d_model, d_model)
    offs = [0]
    for s in sizes:
        offs.append(offs[-1] + s)
    seg = lambda k: w_in[:, offs[k]:offs[k + 1]]
    (q_a, k_a, v_a, q_i, k_i, w_i, q_m, k_m, v_m, i_m, f_m, o_m, g_a, g_m) = [seg(k) for k in range(14)]
    pad = jnp.zeros((w_in.shape[0], COLS_S - (IDX_HEAD_DIM + IDX_HEADS + 2 * MLSTM_HEADS)), w_in.dtype)
    cols = [q_a, k_a, v_a, q_i, k_i, w_i, i_m, f_m, pad, q_m, k_m, v_m, o_m, g_a, g_m, k_i, k_i]
    return jnp.concatenate(cols, axis=1).astype(MXU_DTYPE)


def _layer(x2, B, S, w_in, conv_w, conv_b, i_bias, f_bias, norm_g, w_branch_attn, w_branch_mlstm,
           w_out, ln1_g, ln1_b, w_router, b_router, w_gate_up, b_gate_up, w_down, b_down,
           ln2_g, ln2_b, rel_bias):
    T, D = x2.shape
    bf = MXU_DTYPE
    tq = _tile(S, 256)
    L = _tile(S, 256)
    tm = _tile(T, 256)

    oa, osm, om, og, okk = _project(x2, _relayout_w_in(w_in, D), tm)
    bias_t = _bias_tiles(rel_bias, tq)
    y_att = _dsa_attention(oa, osm, okk, bias_t, B, S, tq)

    gate_bias = jnp.zeros((1, COLS_S), jnp.float32)
    gate_bias = gate_bias.at[0, S_IM:S_IM + MLSTM_HEADS].set(i_bias)
    gate_bias = gate_bias.at[0, S_FM:S_FM + MLSTM_HEADS].set(f_bias)
    y_m = _mlstm(om, osm, conv_w, conv_b.reshape(1, -1), gate_bias, norm_g.reshape(1, -1), B, S, L)

    wr = jnp.zeros((D, LANES), bf).at[:, :N_EXPERTS].set(w_router.astype(bf))
    br = jnp.full((1, LANES), NEG_BIG, jnp.float32).at[0, :N_EXPERTS].set(b_router)
    x1, xp, idx, gates = _merge(y_att, y_m, og, x2, w_branch_attn.astype(bf), w_branch_mlstm.astype(bf),
                                w_out.astype(bf), ln1_g.reshape(1, D), ln1_b.reshape(1, D), wr, br, tm)

    bm = 512
    rank, cnt = _ranks(idx, _tile(T, 512))
    counts = cnt[0, :N_EXPERTS].astype(jnp.int32)
    padded = ((counts + bm - 1) // bm) * bm
    pend = jnp.cumsum(padded)
    pstart = pend - padded
    cap = ((T * TOP_K + bm - 1) // bm) * bm + N_EXPERTS * bm
    n_blocks = cap // bm
    experts = jnp.arange(N_EXPERTS, dtype=jnp.int32)
    sel = idx[:, :TOP_K, None] == experts
    dest = (jnp.sum(jnp.where(sel, pstart, 0), axis=-1) + rank[:, :TOP_K]).reshape(T * TOP_K)
    block_row = jnp.arange(n_blocks, dtype=jnp.int32) * bm
    block_expert = jnp.minimum(jnp.sum((pend[None, :] <= block_row[:, None]).astype(jnp.int32), axis=1),
                               N_EXPERTS - 1)
    n_used = (pend[-1:] // bm).astype(jnp.int32)

    xs = _dispatch(jnp.stack([pend, padded]), dest, xp, cap, _tile(T, 512), bm)
    ybuf = _expert_ffn(block_expert, n_used, xs, w_gate_up, b_gate_up, w_down, b_down, bm)
    return _combine(dest, gates, x1, ybuf, ln2_g.reshape(1, D), ln2_b.reshape(1, D), tm)


def kernel(x, w_in, conv_w, conv_b, mlstm_i_bias, mlstm_f_bias, mlstm_norm_g, w_branch_attn,
           w_branch_mlstm, w_out, ln1_g, ln1_b, w_router, b_router, w_gate_up, b_gate_up,
           w_down, b_down, ln2_g, ln2_b, rel_bias):
    B, S, D = x.shape
    x2 = x.reshape(B * S, D)
    for l in range(w_in.shape[0]):
        x2 = _layer(x2, B, S, w_in[l], conv_w[l], conv_b[l], mlstm_i_bias[l], mlstm_f_bias[l],
                    mlstm_norm_g[l], w_branch_attn[l], w_branch_mlstm[l], w_out[l], ln1_g[l], ln1_b[l],
                    w_router[l], b_router[l], w_gate_up[l], b_gate_up[l], w_down[l], b_down[l],
                    ln2_g[l], ln2_b[l], rel_bias)
    return x2.reshape(B, S, D)
```

```python
import functools
import math

import jax
import jax.numpy as jnp
from jax import lax
from jax.experimental import pallas as pl
from jax.experimental.pallas import tpu as pltpu
from jax.experimental.pallas import tpu_sc as plsc

ATT_HEADS = 8
ATT_HEAD_DIM = 64
ATT_WIDTH = ATT_HEADS * ATT_HEAD_DIM
IDX_HEADS = 8
IDX_HEAD_DIM = 64
TOPK_MAX = 256
MLSTM_HEADS = 4
MLSTM_QK_DIM = 64
MLSTM_V_DIM = 128
MLSTM_WIDTH = MLSTM_HEADS * MLSTM_V_DIM
CONV_WIDTH = 4
N_BUCKETS = 32
MAX_DISTANCE = 128
N_EXPERTS = 32
TOP_K = 4
SWIGLU_ALPHA = 1.702
SWIGLU_LIMIT = 7.0
LN_EPS = 1e-5
DEPTH = 1
DEEPNORM_ALPHA = (2 * DEPTH) ** 0.25

LANES = 128
SUBLANES = 8
VMEM_LIMIT_BYTES = 56 * 1024 * 1024
SC_CORES = 2
SC_SUBCORES = 16
SC_CHUNK = 128

MXU_DTYPE = jnp.bfloat16

INT_MIN = -(2 ** 31)
NEG_BIG = -1e30

QK_M = MLSTM_HEADS * MLSTM_QK_DIM
COLS_A = 2 * ATT_WIDTH + IDX_HEADS * IDX_HEAD_DIM
COLS_S = LANES
COLS_M = 2 * QK_M + 2 * MLSTM_WIDTH
COLS_K = 2 * IDX_HEAD_DIM
S_KI = 0
S_WI = IDX_HEAD_DIM
S_IM = S_WI + IDX_HEADS
S_FM = S_IM + MLSTM_HEADS


def _cparams(sem):
    return pltpu.CompilerParams(dimension_semantics=sem, vmem_limit_bytes=VMEM_LIMIT_BYTES)


def _proj_kernel(x_ref, w_ref, oa_ref, os_ref, om_ref, og_ref, ok_ref, ovt_ref, *, d_model):
    xb = x_ref[...].astype(MXU_DTYPE)
    step = 512

    def mm(lo, hi):
        return jnp.dot(xb, w_ref[:, lo:hi], preferred_element_type=jnp.float32)

    base = 0
    for j in range(0, COLS_A, step):
        oa_ref[:, j:j + step] = mm(base + j, base + j + step).astype(MXU_DTYPE)
    base += COLS_A
    os_ref[...] = mm(base, base + COLS_S)
    base += COLS_S
    for j in range(0, COLS_M, step):
        om_ref[:, j:j + step] = mm(base + j, base + j + step)
    base += COLS_M
    for j in range(0, 2 * d_model, step):
        og_ref[:, j:j + step] = mm(base + j, base + j + step)
    base += 2 * d_model
    ok_ref[...] = mm(base, base + COLS_K).astype(MXU_DTYPE)
    base += COLS_K
    v = mm(base, base + ATT_WIDTH)
    tq = ovt_ref.shape[2]
    for s in range(ovt_ref.shape[0]):
        ovt_ref[s] = v[s * tq:(s + 1) * tq, :].T.astype(MXU_DTYPE)


def _project(x2, w_p, tm, tq):
    T, D = x2.shape
    n_all = w_p.shape[1]
    assert tm % tq == 0
    return pl.pallas_call(
        functools.partial(_proj_kernel, d_model=D),
        grid=(T // tm,),
        in_specs=[pl.BlockSpec((tm, D), lambda i: (i, 0)),
                  pl.BlockSpec((D, n_all), lambda i: (0, 0))],
        out_specs=[pl.BlockSpec((tm, COLS_A), lambda i: (i, 0)),
                   pl.BlockSpec((tm, COLS_S), lambda i: (i, 0)),
                   pl.BlockSpec((tm, COLS_M), lambda i: (i, 0)),
                   pl.BlockSpec((tm, 2 * D), lambda i: (i, 0)),
                   pl.BlockSpec((tm, COLS_K), lambda i: (i, 0)),
                   pl.BlockSpec((tm // tq, ATT_WIDTH, tq), lambda i: (i, 0, 0))],
        out_shape=[jax.ShapeDtypeStruct((T, COLS_A), MXU_DTYPE),
                   jax.ShapeDtypeStruct((T, COLS_S), jnp.float32),
                   jax.ShapeDtypeStruct((T, COLS_M), jnp.float32),
                   jax.ShapeDtypeStruct((T, 2 * D), jnp.float32),
                   jax.ShapeDtypeStruct((T, COLS_K), MXU_DTYPE),
                   jax.ShapeDtypeStruct((T // tq, ATT_WIDTH, tq), MXU_DTYPE)],
        compiler_params=_cparams(("parallel",)),
    )(x2, w_p)


def _bias_kernel(rb_ref, o_ref, *, tq):
    h = pl.program_id(0)
    s = lax.broadcasted_iota(jnp.int32, (tq, tq), 0)
    t = lax.broadcasted_iota(jnp.int32, (tq, tq), 1)
    max_exact = N_BUCKETS // 2
    for d in range(3):
        n = jnp.maximum(t - s + d * tq, 0)
        n_f = jnp.maximum(n, 1).astype(jnp.float32)
        large = max_exact + (jnp.log(n_f / max_exact) / math.log(MAX_DISTANCE / max_exact)
                             * (N_BUCKETS - max_exact)).astype(jnp.int32)
        large = jnp.minimum(large, N_BUCKETS - 1)
        bucket = jnp.where(n < max_exact, n, large)
        acc = jnp.zeros((tq, tq), jnp.float32)
        for k in range(N_BUCKETS):
            acc = jnp.where(bucket == k, rb_ref[h, k], acc)
        o_ref[0, d] = acc


def _bias_tiles(rel_bias, tq):
    assert tq + 1 >= MAX_DISTANCE
    H = rel_bias.shape[0]
    return pl.pallas_call(
        functools.partial(_bias_kernel, tq=tq),
        grid=(H,),
        in_specs=[pl.BlockSpec(memory_space=pltpu.SMEM)],
        out_specs=pl.BlockSpec((1, 3, tq, tq), lambda h: (h, 0, 0, 0)),
        out_shape=jax.ShapeDtypeStruct((H, 3, tq, tq), jnp.float32),
        compiler_params=_cparams(("parallel",)),
    )(rel_bias)


def _dsa_kernel(qa_ref, ka_ref, vt_ref, qi_ref, kk_ref, wq_ref, bias_ref, o_ref,
                keys_ref, hi_ref, lo_ref, msk_ref, qz_ref, qiz_ref, s_ref, p_ref, acc_ref, m_ref, l_ref,
                *, tq, n_sel):
    i = pl.program_id(1)
    nch = i + 1
    t_pos = i * tq + lax.broadcasted_iota(jnp.int32, (1, tq), 1)
    s_loc = lax.broadcasted_iota(jnp.int32, (tq, 1), 0)
    hd = ATT_HEAD_DIM
    n_pairs = ATT_HEADS // 2
    assert 2 * hd == LANES and IDX_HEAD_DIM == hd and IDX_HEADS == ATT_HEADS

    top = lax.broadcasted_iota(jnp.int32, (LANES, tq), 0) < hd

    def pair_operand(blk):
        bt = blk.astype(jnp.float32).T
        return jnp.concatenate([jnp.where(top, bt, 0.0), jnp.where(top, 0.0, bt)],
                               axis=1).astype(MXU_DTYPE)

    for p in range(n_pairs):
        pair = slice(p * LANES, (p + 1) * LANES)
        qz_ref[p] = pair_operand(qa_ref[:, pair] * (hd ** -0.5))
        qiz_ref[p] = pair_operand(qi_ref[:, pair])

    ws = wq_ref[...].T[S_WI:S_WI + IDX_HEADS, :] * (IDX_HEADS ** -0.5)

    def score_tile(c):
        off = pl.multiple_of(c * tq, tq)
        kk = kk_ref[pl.ds(off, tq), :]
        sc = jnp.zeros((tq, tq), jnp.float32)
        for p in range(n_pairs):
            d = jnp.dot(kk, qiz_ref[p], preferred_element_type=jnp.float32)
            for j in range(2):
                h = 2 * p + j
                sc = sc + ws[h:h + 1, :] * jnp.maximum(d[:, j * tq:(j + 1) * tq], 0.0)
        sc = sc + 0.0
        b = pltpu.bitcast(sc, jnp.int32)
        sk = b ^ ((b >> 31) & jnp.int32(0x7FFFFFFF))
        sk = jnp.where(off + s_loc <= t_pos, sk, jnp.int32(INT_MIN))
        keys_ref[c] = sk
        hi_ref[c] = (sk >> 16).astype(jnp.int16)

    def score_pair(c2, carry):
        score_tile(2 * c2)
        score_tile(2 * c2 + 1)
        return carry

    def score_single(c, carry):
        score_tile(c)
        return carry

    lax.fori_loop(0, nch // 2, score_pair, 0)
    lax.fori_loop(2 * (nch // 2), nch, score_single, 0)

    pack = 2 * SUBLANES
    half_min = -(2 ** 15)

    def count16(ref, pred_fn):
        def tile_count(c):
            hit = jnp.where(pred_fn(ref[c]), jnp.int16(1), jnp.int16(0))
            parts = [hit[r * pack:(r + 1) * pack, :] for r in range(tq // pack)]
            while len(parts) > 1:
                parts = [a + b for a, b in zip(parts[::2], parts[1::2])]
            return parts[0]
        acc = lax.fori_loop(0, nch // 2,
                            lambda c2, a: a + (tile_count(2 * c2) + tile_count(2 * c2 + 1)),
                            jnp.zeros((pack, tq), jnp.int16))
        acc = lax.fori_loop(2 * (nch // 2), nch, lambda c, a: a + tile_count(c), acc)
        return jnp.sum(acc.astype(jnp.float32), axis=0, keepdims=True)

    def search16(ref, target):
        def bit_body(it, u):
            cand_u = u | lax.shift_left(jnp.int32(1), 15 - it)
            cand = (cand_u + half_min).astype(jnp.int16)
            return jnp.where(count16(ref, lambda x: x >= cand) >= target, cand_u, u)
        return lax.fori_loop(0, 16, bit_body, jnp.zeros((1, tq), jnp.int32))

    u_hi = search16(hi_ref, float(n_sel))
    thr_hi = (u_hi + half_min).astype(jnp.int16)
    need_lo = n_sel - count16(hi_ref, lambda x: x > thr_hi)

    def low_body(c, carry):
        lo = ((keys_ref[c] & 0xFFFF) + half_min).astype(jnp.int16)
        lo_ref[c] = jnp.where(hi_ref[c] == thr_hi, lo, jnp.int16(half_min))
        return carry

    lax.fori_loop(0, nch, low_body, 0)
    u_lo = search16(lo_ref, need_lo)
    thr_lo = (u_lo + half_min).astype(jnp.int16)
    thr = lax.shift_left(u_hi + half_min, 16) | u_lo
    need = jnp.where(thr == INT_MIN, 0.0, need_lo - count16(lo_ref, lambda x: x > thr_lo))
    n_eq = count16(lo_ref, lambda x: x == thr_lo)
    surplus = jnp.where((thr != INT_MIN) & (n_eq > need), 1.0, 0.0)
    has_surplus = jnp.max(surplus) > 0.0

    @pl.when(has_surplus)
    def _():
        tri = jnp.where(lax.broadcasted_iota(jnp.int32, (tq, tq), 0)
                        >= lax.broadcasted_iota(jnp.int32, (tq, tq), 1), 1.0, 0.0).astype(MXU_DTYPE)

        def mask_body(c, run):
            kc = keys_ref[c]
            eq = kc == thr
            eqf = jnp.where(eq, 1.0, 0.0)
            prefix = jnp.dot(tri, eqf.astype(MXU_DTYPE), preferred_element_type=jnp.float32) + run
            msk_ref[c] = jnp.where(kc > thr, 0.0,
                                   jnp.where(eq, jnp.where(prefix <= need, 0.0, NEG_BIG), NEG_BIG))
            return run + jnp.sum(eqf, axis=0, keepdims=True)

        lax.fori_loop(0, nch, mask_body, jnp.zeros((1, tq), jnp.float32))

    @pl.when(jnp.logical_not(has_surplus))
    def _():
        floor = jnp.maximum(thr, INT_MIN + 1)

        def mask_body(c, carry):
            msk_ref[c] = jnp.where(keys_ref[c] >= floor, 0.0, NEG_BIG)
            return carry

        lax.fori_loop(0, nch, mask_body, 0)

    m_ref[...] = jnp.full(m_ref.shape, NEG_BIG, jnp.float32)
    l_ref[...] = jnp.zeros(l_ref.shape, jnp.float32)
    acc_ref[...] = jnp.zeros(acc_ref.shape, jnp.float32)

    def att_step(cs, far):
        tile_max = [None] * ATT_HEADS
        for n, c in enumerate(cs):
            off = pl.multiple_of(c * tq, tq)
            mk = msk_ref[c]
            for p in range(n_pairs):
                pair = slice(p * LANES, (p + 1) * LANES)
                s2 = jnp.dot(ka_ref[pl.ds(off, tq), pair], qz_ref[p],
                             preferred_element_type=jnp.float32)
                for j in range(2):
                    h = 2 * p + j
                    s = s2[:, j * tq:(j + 1) * tq] + mk
                    if not far:
                        s = s + bias_ref[h, i - c]
                    s_ref[n, h] = s
                    mx = jnp.max(s, axis=0, keepdims=True)
                    tile_max[h] = mx if tile_max[h] is None else jnp.maximum(tile_max[h], mx)
        alphas = []
        for h in range(ATT_HEADS):
            m_prev = m_ref[h:h + 1, :]
            if far:
                b_far = bias_ref[h, 2, 0:1, 0:1]
                m_new = jnp.maximum(m_prev, tile_max[h] + b_far)
                shift = m_new - b_far
            else:
                m_new = jnp.maximum(m_prev, tile_max[h])
                shift = m_new
            alpha = jnp.exp(m_prev - m_new)
            l_new = alpha * l_ref[h:h + 1, :]
            for n in range(len(cs)):
                pe = jnp.exp(s_ref[n, h] - shift)
                l_new = l_new + jnp.sum(pe, axis=0, keepdims=True)
                p_ref[n, h // 2, :, (h % 2) * tq:(h % 2 + 1) * tq] = pe.astype(MXU_DTYPE)
            l_ref[h:h + 1, :] = l_new
            m_ref[h:h + 1, :] = m_new
            alphas.append(alpha)
        for p in range(n_pairs):
            pv = None
            for n, c in enumerate(cs):
                d = jnp.dot(vt_ref[c, p * LANES:(p + 1) * LANES, :], p_ref[n, p],
                            preferred_element_type=jnp.float32)
                pv = d if pv is None else pv + d
            for j in range(2):
                h = 2 * p + j
                acc_ref[h] = alphas[h] * acc_ref[h] + pv[j * hd:(j + 1) * hd, j * tq:(j + 1) * tq]

    def far_pair(c2, carry):
        att_step([2 * c2, 2 * c2 + 1], True)
        return carry

    def far_single(c, carry):
        att_step([c], True)
        return carry

    n_far = jnp.maximum(i - 1, 0)
    lax.fori_loop(0, n_far // 2, far_pair, 0)
    lax.fori_loop(2 * (n_far // 2), n_far, far_single, 0)

    @pl.when(i > 0)
    def _():
        att_step([i - 1, i], False)

    @pl.when(i == 0)
    def _():
        att_step([i], False)
    y_t = jnp.concatenate([acc_ref[h] / l_ref[h:h + 1, :] for h in range(ATT_HEADS)], axis=0)
    o_ref[...] = y_t.T.astype(o_ref.dtype)


def _dsa_attention(oa, osm, okk, v_t, bias_t, B, S, tq):
    nq = S // tq
    n_sel = min(TOPK_MAX, S // 4)
    W = ATT_WIDTH
    assert v_t.shape == (B * nq, W, tq)
    return pl.pallas_call(
        functools.partial(_dsa_kernel, tq=tq, n_sel=n_sel),
        grid=(B, nq),
        in_specs=[pl.BlockSpec((tq, W), lambda b, i: (b * nq + i, 0)),
                  pl.BlockSpec((S, W), lambda b, i: (b, 1)),
                  pl.BlockSpec((nq, W, tq), lambda b, i: (b, 0, 0)),
                  pl.BlockSpec((tq, W), lambda b, i: (b * nq + i, 2)),
                  pl.BlockSpec((S, COLS_K), lambda b, i: (b, 0)),
                  pl.BlockSpec((tq, COLS_S), lambda b, i: (b * nq + i, 0)),
                  pl.BlockSpec((ATT_HEADS, 3, tq, tq), lambda b, i: (0, 0, 0, 0))],
        out_specs=pl.BlockSpec((tq, W), lambda b, i: (b * nq + i, 0)),
        out_shape=jax.ShapeDtypeStruct((B * S, W), MXU_DTYPE),
        scratch_shapes=[pltpu.VMEM((nq, tq, tq), jnp.int32),
                        pltpu.VMEM((nq, tq, tq), jnp.int16),
                        pltpu.VMEM((nq, tq, tq), jnp.int16),
                        pltpu.VMEM((nq, tq, tq), jnp.float32),
                        pltpu.VMEM((ATT_HEADS // 2, LANES, 2 * tq), MXU_DTYPE),
                        pltpu.VMEM((IDX_HEADS // 2, LANES, 2 * tq), MXU_DTYPE),
                        pltpu.VMEM((2, ATT_HEADS, tq, tq), jnp.float32),
                        pltpu.VMEM((2, ATT_HEADS // 2, tq, 2 * tq), MXU_DTYPE),
                        pltpu.VMEM((ATT_HEADS, ATT_HEAD_DIM, tq), jnp.float32),
                        pltpu.VMEM((ATT_HEADS, tq), jnp.float32),
                        pltpu.VMEM((ATT_HEADS, tq), jnp.float32)],
        compiler_params=pltpu.CompilerParams(
            dimension_semantics=("parallel", "arbitrary"), vmem_limit_bytes=VMEM_LIMIT_BYTES,
            ),
    )(oa, oa, v_t, oa, okk, osm, bias_t)


def _mlstm_kernel(qk_ref, v_ref, og_ref, s_ref, cw_ref, cb_ref, gb_ref, ng_ref, y_ref,
                  xbuf, cst, mst, qk_s, in_s, p_s, kw_s, *, L):
    c = pl.program_id(1)
    H, DK, DV = MLSTM_HEADS, MLSTM_QK_DIM, MLSTM_V_DIM
    tail = SUBLANES

    @pl.when(c == 0)
    def _():
        xbuf[0:tail, :] = jnp.zeros((tail, 2 * QK_M), jnp.float32)
        cst[...] = jnp.zeros(cst.shape, jnp.float32)
        mst[...] = jnp.zeros(mst.shape, jnp.float32)

    @pl.when(c > 0)
    def _():
        xbuf[0:tail, :] = xbuf[L:L + tail, :]

    xbuf[tail:tail + L, :] = qk_ref[...]
    conv = cb_ref[...] + jnp.zeros((L, 2 * QK_M), jnp.float32)
    for j in range(CONV_WIDTH):
        conv = conv + cw_ref[j:j + 1, :] * xbuf[pl.ds(tail - (CONV_WIDTH - 1) + j, L), :]
    qk = conv * jax.nn.sigmoid(conv)
    q = qk[:, :QK_M]
    kT = (qk[:, QK_M:] * (DK ** -0.5)).T

    g = s_ref[...] + gb_ref[...]
    logf = jnp.minimum(g, 0.0) - jnp.log(1.0 + jnp.exp(-jnp.abs(g)))
    row = lax.broadcasted_iota(jnp.int32, (L, L), 0)
    col = lax.broadcasted_iota(jnp.int32, (L, L), 1)
    causal = row >= col
    bcum = jnp.dot(jnp.where(causal, 1.0, 0.0), logf, precision=lax.Precision.HIGHEST,
                   preferred_element_type=jnp.float32)
    gT = g.T
    bT = bcum.T
    ones_col = jnp.where(lax.broadcasted_iota(jnp.int32, (L, LANES), 1) == 0, 1.0, 0.0
                         ).astype(MXU_DTYPE)

    for h in range(H):
        qh = q[:, h * DK:(h + 1) * DK].astype(MXU_DTYPE)
        qk_s[h] = jnp.dot(qh, kT[h * DK:(h + 1) * DK, :].astype(MXU_DTYPE),
                          preferred_element_type=jnp.float32)
        in_s[h] = jnp.dot(qh, cst[h].astype(MXU_DTYPE), preferred_element_type=jnp.float32)

    stats = []
    for h in range(H):
        bc = bcum[:, S_FM + h:S_FM + h + 1]
        br = bT[S_FM + h:S_FM + h + 1, :]
        lir = gT[S_IM + h:S_IM + h + 1, :]
        m0 = mst[h:h + 1, 0:1]
        d = jnp.where(causal, bc - br + lir, -jnp.inf)
        m_inter = bc + m0
        m_t = jnp.maximum(m_inter, jnp.max(d, axis=1, keepdims=True))
        p = qk_s[h] * jnp.exp(d - m_t)
        p_s[h] = p.astype(MXU_DTYPE)
        b_end = bc[L - 1:L, :]
        a = b_end - br + lir
        m_new = jnp.maximum(b_end + m0, jnp.max(a, axis=1, keepdims=True))
        kw_s[h] = (kT[h * DK:(h + 1) * DK, :] * jnp.exp(a - m_new)).astype(MXU_DTYPE)
        stats.append((jnp.sum(p, axis=1, keepdims=True), jnp.exp(m_inter - m_t), m_t,
                      jnp.exp(b_end + m0 - m_new), m_new))

    for h in range(H):
        den_intra, sc, m_t, decay, m_new = stats[h]
        vh = v_ref[:, h * DV:(h + 1) * DV].astype(MXU_DTYPE)
        inter = in_s[h]
        num = jnp.dot(p_s[h], vh, preferred_element_type=jnp.float32) + sc * inter[:, :DV]
        den = den_intra + sc * inter[:, DV:DV + 1]
        hh = num / jnp.maximum(jnp.abs(den), jnp.exp(-m_t))
        vaug = jnp.concatenate([vh, ones_col], axis=1)
        cst[h] = decay * cst[h] + jnp.dot(kw_s[h], vaug, preferred_element_type=jnp.float32)
        mst[h:h + 1, :] = jnp.broadcast_to(m_new, (1, LANES))
        mu = jnp.mean(hh, axis=1, keepdims=True)
        hc = hh - mu
        var = jnp.mean(hc * hc, axis=1, keepdims=True)
        hn = hc * lax.rsqrt(var + LN_EPS) * ng_ref[:, h * DV:(h + 1) * DV]
        y_ref[:, h * DV:(h + 1) * DV] = (hn * jax.nn.sigmoid(og_ref[:, h * DV:(h + 1) * DV])).astype(y_ref.dtype)


def _mlstm(om, osm, conv_w, conv_b, gate_bias, norm_g, B, S, L):
    nc = S // L
    W = MLSTM_WIDTH
    assert 2 * QK_M == W
    return pl.pallas_call(
        functools.partial(_mlstm_kernel, L=L),
        grid=(B, nc),
        in_specs=[pl.BlockSpec((L, W), lambda b, c: (b * nc + c, 0)),
                  pl.BlockSpec((L, W), lambda b, c: (b * nc + c, 1)),
                  pl.BlockSpec((L, W), lambda b, c: (b * nc + c, 2)),
                  pl.BlockSpec((L, COLS_S), lambda b, c: (b * nc + c, 0)),
                  pl.BlockSpec((CONV_WIDTH, W), lambda b, c: (0, 0)),
                  pl.BlockSpec((1, W), lambda b, c: (0, 0)),
                  pl.BlockSpec((1, COLS_S), lambda b, c: (0, 0)),
                  pl.BlockSpec((1, W), lambda b, c: (0, 0))],
        out_specs=pl.BlockSpec((L, W), lambda b, c: (b * nc + c, 0)),
        out_shape=jax.ShapeDtypeStruct((B * S, W), MXU_DTYPE),
        scratch_shapes=[pltpu.VMEM((L + 2 * SUBLANES, W), jnp.float32),
                        pltpu.VMEM((MLSTM_HEADS, MLSTM_QK_DIM, 2 * LANES), jnp.float32),
                        pltpu.VMEM((SUBLANES, LANES), jnp.float32),
                        pltpu.VMEM((MLSTM_HEADS, L, L), jnp.float32),
                        pltpu.VMEM((MLSTM_HEADS, L, 2 * LANES), jnp.float32),
                        pltpu.VMEM((MLSTM_HEADS, L, L), MXU_DTYPE),
                        pltpu.VMEM((MLSTM_HEADS, MLSTM_QK_DIM, L), MXU_DTYPE)],
        compiler_params=_cparams(("parallel", "arbitrary")),
    )(om, om, om, osm, conv_w, conv_b, gate_bias, norm_g)


def _layer_norm(z, g, b):
    mu = jnp.mean(z, axis=1, keepdims=True)
    zc = z - mu
    var = jnp.mean(zc * zc, axis=1, keepdims=True)
    return zc * lax.rsqrt(var + LN_EPS) * g + b


def _merge_kernel(ya_ref, ym_ref, g_ref, x_ref, wa_ref, wm_ref, wo_ref, lg_ref, lb_ref,
                  wr_ref, br_ref, x1_ref, xp_ref, idx_ref, gate_ref, cnt_ref, carry_ref, *, d_model):
    D = d_model
    mix = (jax.nn.sigmoid(g_ref[:, :D]) * jnp.dot(ya_ref[...], wa_ref[...], preferred_element_type=jnp.float32)
           + jax.nn.sigmoid(g_ref[:, D:]) * jnp.dot(ym_ref[...], wm_ref[...], preferred_element_type=jnp.float32))
    y = jnp.dot(mix.astype(MXU_DTYPE), wo_ref[...], preferred_element_type=jnp.float32)
    x1 = _layer_norm(DEEPNORM_ALPHA * x_ref[...] + y, lg_ref[...], lb_ref[...])
    x1_ref[...] = x1
    xb = x1.astype(MXU_DTYPE)
    bits = pltpu.bitcast(x1.astype(jnp.bfloat16).astype(jnp.float32), jnp.uint32)
    xp_ref[...] = (bits[:, :D // 2] & jnp.uint32(0xFFFF0000)) | (bits[:, D // 2:] >> 16)

    logits = jnp.dot(xb, wr_ref[...], preferred_element_type=jnp.float32) + br_ref[...]
    tm = logits.shape[0]
    lane = lax.broadcasted_iota(jnp.int32, (tm, LANES), 1)
    lane_f = lane.astype(jnp.float32)
    vals, idxs = [], []
    for _ in range(TOP_K):
        mx = jnp.max(logits, axis=1, keepdims=True)
        ix = jnp.min(jnp.where(logits == mx, lane_f, float(LANES)), axis=1, keepdims=True)
        vals.append(mx)
        idxs.append(ix)
        logits = jnp.where(lane_f == ix, -jnp.inf, logits)
    es = [jnp.exp(v - vals[0]) for v in vals]
    tot = es[0]
    for e in es[1:]:
        tot = tot + e
    @pl.when(pl.program_id(0) == 0)
    def _():
        carry_ref[...] = jnp.zeros(carry_ref.shape, jnp.float32)

    hots = [lane_f == ix for ix in idxs]
    c = jnp.zeros((tm, LANES), jnp.float32)
    for hot in hots:
        c = c + jnp.where(hot, 1.0, 0.0)
    before = jnp.where(lax.broadcasted_iota(jnp.int32, (tm, tm), 0)
                       > lax.broadcasted_iota(jnp.int32, (tm, tm), 1), 1.0, 0.0).astype(MXU_DTYPE)
    carry = carry_ref[0:1, :]
    prior = jnp.dot(before, c.astype(MXU_DTYPE), preferred_element_type=jnp.float32) + carry
    total = carry + jnp.sum(c, axis=0, keepdims=True)
    carry_ref[...] = jnp.broadcast_to(total, carry_ref.shape)
    cnt_ref[...] = jnp.broadcast_to(total, cnt_ref.shape)

    idx_out = jnp.zeros((tm, LANES), jnp.float32)
    gate_out = jnp.zeros((tm, LANES), jnp.float32)
    for k in range(TOP_K):
        idx_out = jnp.where(lane == k, idxs[k], idx_out)
        rank_k = jnp.sum(jnp.where(hots[k], prior, 0.0), axis=1, keepdims=True)
        idx_out = jnp.where(lane == TOP_K + k, rank_k, idx_out)
        gate_out = jnp.where(lane == k, es[k] / tot, gate_out)
    idx_ref[...] = idx_out.astype(jnp.int32)
    gate_ref[...] = gate_out


def _merge(y_att, y_m, og, x2, wa, wm, wo, ln_g, ln_b, wr, br, tm):
    T, D = x2.shape
    full = lambda shape: pl.BlockSpec(shape, lambda i: (0, 0))
    return pl.pallas_call(
        functools.partial(_merge_kernel, d_model=D),
        grid=(T // tm,),
        in_specs=[pl.BlockSpec((tm, ATT_WIDTH), lambda i: (i, 0)),
                  pl.BlockSpec((tm, MLSTM_WIDTH), lambda i: (i, 0)),
                  pl.BlockSpec((tm, 2 * D), lambda i: (i, 0)),
                  pl.BlockSpec((tm, D), lambda i: (i, 0)),
                  full(wa.shape), full(wm.shape), full(wo.shape),
                  full((1, D)), full((1, D)), full(wr.shape), full((1, LANES))],
        out_specs=[pl.BlockSpec((tm, D), lambda i: (i, 0)),
                   pl.BlockSpec((tm, D // 2), lambda i: (i, 0)),
                   pl.BlockSpec((tm, LANES), lambda i: (i, 0)),
                   pl.BlockSpec((tm, LANES), lambda i: (i, 0)),
                   pl.BlockSpec((SUBLANES, LANES), lambda i: (0, 0))],
        out_shape=[jax.ShapeDtypeStruct((T, D), jnp.float32),
                   jax.ShapeDtypeStruct((T, D // 2), jnp.uint32),
                   jax.ShapeDtypeStruct((T, LANES), jnp.int32),
                   jax.ShapeDtypeStruct((T, LANES), jnp.float32),
                   jax.ShapeDtypeStruct((SUBLANES, LANES), jnp.float32)],
        scratch_shapes=[pltpu.VMEM((SUBLANES, LANES), jnp.float32)],
        compiler_params=_cparams(("arbitrary",)),
    )(y_att, y_m, og, x2, wa, wm, wo, ln_g, ln_b, wr, br)


def _dispatch_sc(dest, xp, cap):
    T, W = xp.shape
    n_chunks = T // SC_CHUNK
    n_workers = SC_CORES * SC_SUBCORES
    assert T % SC_CHUNK == 0 and n_chunks % n_workers == 0
    per_worker = n_chunks // n_workers
    idx = dest.reshape(n_chunks, SC_CHUNK, TOP_K).transpose(0, 2, 1)
    mesh = plsc.VectorSubcoreMesh(core_axis_name="c", subcore_axis_name="s")

    @functools.partial(
        pl.kernel, mesh=mesh,
        out_type=jax.ShapeDtypeStruct((cap, W), xp.dtype),
        scratch_types=[pltpu.VMEM((SC_CHUNK, W), xp.dtype),
                       pltpu.VMEM((TOP_K, SC_CHUNK), jnp.int32)])
    def scatter_rows(x_hbm, idx_hbm, xs_hbm, rows_v, idx_v):
        worker = lax.axis_index("s") * SC_CORES + lax.axis_index("c")

        def body(j, carry):
            c = worker * per_worker + j
            pltpu.sync_copy(x_hbm.at[pl.ds(c * SC_CHUNK, SC_CHUNK)], rows_v)
            pltpu.sync_copy(idx_hbm.at[c], idx_v)
            for k in range(TOP_K):
                pltpu.sync_copy(rows_v, xs_hbm.at[idx_v.at[k]])
            return carry

        lax.fori_loop(0, per_worker, body, 0)

    return scatter_rows(xp, idx)


def _ffn_kernel(be_ref, nu_ref, nv_ref, xs_ref, wgu_ref, bgu_ref, wd_ref, bd_ref, y_ref, wgu_b,
                wd_b, act_s, *, d_ff):
    r = pl.program_id(0)
    e = be_ref[r]
    prev = be_ref[jnp.maximum(r - 1, 0)]

    @pl.when((r == 0) | (e != prev))
    def _():
        wgu_b[...] = wgu_ref[0].astype(MXU_DTYPE)
        wd_b[...] = wd_ref[0].astype(MXU_DTYPE)

    @pl.when(r < nu_ref[0])
    def _():
        live = lax.broadcasted_iota(jnp.int32, (xs_ref.shape[0], 1), 0) < nv_ref[r]
        w = jnp.where(live, xs_ref[...], jnp.uint32(0))
        half = w.shape[1]
        x_hi = pltpu.bitcast(w & jnp.uint32(0xFFFF0000), jnp.float32).astype(MXU_DTYPE)
        x_lo = pltpu.bitcast(w << 16, jnp.float32).astype(MXU_DTYPE)
        x = jnp.concatenate([x_hi, x_lo], axis=1)
        step = 512
        for j in range(0, d_ff, step):
            def gu(lo):
                return (jnp.dot(x, wgu_b[:, lo:lo + step], preferred_element_type=jnp.float32)
                        + bgu_ref[0, :, lo:lo + step])
            gate = jnp.minimum(gu(j), SWIGLU_LIMIT)
            up = jnp.clip(gu(d_ff + j), -SWIGLU_LIMIT, SWIGLU_LIMIT)
            act = (up + 1.0) * (gate * jax.nn.sigmoid(SWIGLU_ALPHA * gate))
            act_s[:, j:j + step] = act.astype(MXU_DTYPE)
        y = jnp.dot(act_s[...], wd_b[...], preferred_element_type=jnp.float32) + bd_ref[0]
        bits = pltpu.bitcast(y.astype(jnp.bfloat16).astype(jnp.float32), jnp.uint32)
        y_ref[...] = (bits[:, :half] & jnp.uint32(0xFFFF0000)) | (bits[:, half:] >> 16)

    @pl.when(r >= nu_ref[0])
    def _():
        y_ref[...] = jnp.zeros(y_ref.shape, y_ref.dtype)


def _expert_ffn(block_expert, n_used, n_valid, xs, w_gate_up, b_gate_up, w_down, b_down, bm):
    cap, half = xs.shape
    E, D, F2 = w_gate_up.shape
    d_ff = F2 // 2
    grid_spec = pltpu.PrefetchScalarGridSpec(
        num_scalar_prefetch=3,
        grid=(cap // bm,),
        in_specs=[pl.BlockSpec((bm, half), lambda r, be, nu, nv: (jnp.minimum(r, nu[0] - 1), 0)),
                  pl.BlockSpec((1, D, F2), lambda r, be, nu, nv: (be[r], 0, 0)),
                  pl.BlockSpec((1, 1, F2), lambda r, be, nu, nv: (be[r], 0, 0)),
                  pl.BlockSpec((1, d_ff, D), lambda r, be, nu, nv: (be[r], 0, 0)),
                  pl.BlockSpec((1, 1, D), lambda r, be, nu, nv: (be[r], 0, 0))],
        out_specs=pl.BlockSpec((bm, half), lambda r, be, nu, nv: (r, 0)),
        scratch_shapes=[pltpu.VMEM((D, F2), MXU_DTYPE),
                        pltpu.VMEM((d_ff, D), MXU_DTYPE),
                        pltpu.VMEM((bm, d_ff), MXU_DTYPE)],
    )
    return pl.pallas_call(
        functools.partial(_ffn_kernel, d_ff=d_ff),
        grid_spec=grid_spec,
        out_shape=jax.ShapeDtypeStruct((cap, half), jnp.uint32),
        compiler_params=_cparams(("arbitrary",)),
    )(block_expert, n_used, n_valid, xs, w_gate_up, b_gate_up.reshape(E, 1, F2), w_down,
      b_down.reshape(E, 1, D))


def _gather_sc(dest, ybuf):
    cap, D = ybuf.shape
    T = dest.shape[0] // TOP_K
    chunk = SC_CHUNK
    n_chunks = T // chunk
    n_workers = SC_CORES * SC_SUBCORES
    assert T % chunk == 0 and n_chunks % n_workers == 0
    per_worker = n_chunks // n_workers
    idx = dest.reshape(n_chunks, chunk, TOP_K).transpose(0, 2, 1)
    mesh = plsc.VectorSubcoreMesh(core_axis_name="c", subcore_axis_name="s")

    @functools.partial(
        pl.kernel, mesh=mesh,
        out_type=jax.ShapeDtypeStruct((TOP_K, T, D), ybuf.dtype),
        scratch_types=[pltpu.VMEM((chunk, D), ybuf.dtype),
                       pltpu.VMEM((TOP_K, chunk), jnp.int32)])
    def gather_rows(y_hbm, idx_hbm, out_hbm, rows_v, idx_v):
        worker = lax.axis_index("s") * SC_CORES + lax.axis_index("c")

        def body(j, carry):
            c = worker * per_worker + j
            pltpu.sync_copy(idx_hbm.at[c], idx_v)
            for k in range(TOP_K):
                pltpu.sync_copy(y_hbm.at[idx_v.at[k]], rows_v)
                pltpu.sync_copy(rows_v, out_hbm.at[k, pl.ds(c * chunk, chunk)])
            return carry

        lax.fori_loop(0, per_worker, body, 0)

    return gather_rows(ybuf, idx)


def _combine_dense_kernel(yk_ref, gate_ref, x1_ref, lg_ref, lb_ref, *rest):
    o_ref = rest[-1]
    left = right = None
    for k in range(TOP_K):
        w = yk_ref[k]
        g = gate_ref[:, k:k + 1]
        hi = g * pltpu.bitcast(w & jnp.uint32(0xFFFF0000), jnp.float32)
        lo = g * pltpu.bitcast(w << 16, jnp.float32)
        left = hi if left is None else left + hi
        right = lo if right is None else right + lo
    y = jnp.concatenate([left, right], axis=1)
    o_ref[...] = _layer_norm(DEEPNORM_ALPHA * x1_ref[...] + y, lg_ref[...], lb_ref[...])


def _combine_dense(yk, first_tile, gates, x1, ln_g, ln_b, prev, tm):
    T, D = x1.shape
    n = yk.shape[1] // tm
    in_specs = [pl.BlockSpec((TOP_K, tm, D // 2), lambda i: (0, i, 0)),
                pl.BlockSpec((tm, LANES), lambda i: (i + first_tile, 0)),
                pl.BlockSpec((tm, D), lambda i: (i + first_tile, 0)),
                pl.BlockSpec((1, D), lambda i: (0, 0)),
                pl.BlockSpec((1, D), lambda i: (0, 0))]
    args = [yk, gates, x1, ln_g, ln_b]
    aliases = {}
    if prev is not None:
        in_specs.append(pl.BlockSpec(memory_space=pl.ANY))
        args.append(prev)
        aliases = {len(args) - 1: 0}
    return pl.pallas_call(
        _combine_dense_kernel,
        grid=(n,),
        in_specs=in_specs,
        out_specs=pl.BlockSpec((tm, D), lambda i: (i + first_tile, 0)),
        out_shape=jax.ShapeDtypeStruct((T, D), jnp.float32),
        input_output_aliases=aliases,
        compiler_params=_cparams(("parallel",)),
    )(*args)


def _tile(n, pref):
    t = min(n, pref)
    assert n % t == 0
    return t


def _relayout_w_in(w_in, d_model):
    sizes = (ATT_WIDTH, ATT_WIDTH, ATT_WIDTH, IDX_HEADS * IDX_HEAD_DIM, IDX_HEAD_DIM, IDX_HEADS,
             QK_M, QK_M, MLSTM_WIDTH, MLSTM_HEADS, MLSTM_HEADS, MLSTM_WIDTH, d_model, d_model)
    offs = [0]
    for s in sizes:
        offs.append(offs[-1] + s)
    seg = lambda k: w_in[:, offs[k]:offs[k + 1]]
    (q_a, k_a, v_a, q_i, k_i, w_i, q_m, k_m, v_m, i_m, f_m, o_m, g_a, g_m) = [seg(k) for k in range(14)]
    pad = jnp.zeros((w_in.shape[0], COLS_S - (IDX_HEAD_DIM + IDX_HEADS + 2 * MLSTM_HEADS)), w_in.dtype)
    cols = [q_a, k_a, q_i, k_i, w_i, i_m, f_m, pad, q_m, k_m, v_m, o_m, g_a, g_m, k_i, k_i, v_a]
    return jnp.concatenate(cols, axis=1).astype(MXU_DTYPE)


def _layer(x2, B, S, w_in, conv_w, conv_b, i_bias, f_bias, norm_g, w_branch_attn, w_branch_mlstm,
           w_out, ln1_g, ln1_b, w_router, b_router, w_gate_up, b_gate_up, w_down, b_down,
           ln2_g, ln2_b, rel_bias):
    T, D = x2.shape
    bf = MXU_DTYPE
    tq = _tile(S, 256)
    L = _tile(S, 256)
    tm = _tile(T, 256)

    oa, osm, om, og, okk, v_t = _project(x2, _relayout_w_in(w_in, D), _tile(T, 512), tq)
    bias_t = _bias_tiles(rel_bias, tq)
    y_att = _dsa_attention(oa, osm, okk, v_t, bias_t, B, S, tq)

    gate_bias = jnp.zeros((1, COLS_S), jnp.float32)
    gate_bias = gate_bias.at[0, S_IM:S_IM + MLSTM_HEADS].set(i_bias)
    gate_bias = gate_bias.at[0, S_FM:S_FM + MLSTM_HEADS].set(f_bias)
    y_m = _mlstm(om, osm, conv_w, conv_b.reshape(1, -1), gate_bias, norm_g.reshape(1, -1), B, S, L)

    wr = jnp.zeros((D, LANES), bf).at[:, :N_EXPERTS].set(w_router.astype(bf))
    br = jnp.full((1, LANES), NEG_BIG, jnp.float32).at[0, :N_EXPERTS].set(b_router)
    x1, xp, idx, gates, cnt = _merge(y_att, y_m, og, x2, w_branch_attn.astype(bf),
                                     w_branch_mlstm.astype(bf), w_out.astype(bf), ln1_g.reshape(1, D),
                                     ln1_b.reshape(1, D), wr, br, _tile(T, 512))

    bm = 1024
    rank = idx[:, TOP_K:]
    counts = cnt[0, :N_EXPERTS].astype(jnp.int32)
    padded = ((counts + bm - 1) // bm) * bm
    pend = jnp.cumsum(padded)
    pstart = pend - padded
    cap = ((T * TOP_K + bm - 1) // bm) * bm + N_EXPERTS * bm
    n_blocks = cap // bm
    experts = jnp.arange(N_EXPERTS, dtype=jnp.int32)
    sel = idx[:, :TOP_K, None] == experts
    dest = (jnp.sum(jnp.where(sel, pstart, 0), axis=-1) + rank[:, :TOP_K]).reshape(T * TOP_K)
    block_row = jnp.arange(n_blocks, dtype=jnp.int32) * bm
    block_expert = jnp.minimum(jnp.sum((pend[None, :] <= block_row[:, None]).astype(jnp.int32), axis=1),
                               N_EXPERTS - 1)
    n_used = (pend[-1:] // bm).astype(jnp.int32)
    n_valid = jnp.clip((pstart + counts)[block_expert] - block_row, 0, bm).astype(jnp.int32)

    xs = _dispatch_sc(dest, xp, cap)
    ybuf = _expert_ffn(block_expert, n_used, n_valid, xs, w_gate_up, b_gate_up, w_down, b_down, bm)
    n_groups = 4 if T % (4 * SC_CORES * SC_SUBCORES * SC_CHUNK) == 0 else 1
    per_group = T // n_groups
    slabs = [_gather_sc(dest[g * per_group * TOP_K:(g + 1) * per_group * TOP_K], ybuf)
             for g in range(n_groups)]
    out = None
    tc = _tile(per_group, 1024)
    for g, yk in enumerate(slabs):
        out = _combine_dense(yk, g * per_group // tc, gates, x1, ln2_g.reshape(1, D),
                             ln2_b.reshape(1, D), out, tc)
    return out


def kernel(x, w_in, conv_w, conv_b, mlstm_i_bias, mlstm_f_bias, mlstm_norm_g, w_branch_attn,
           w_branch_mlstm, w_out, ln1_g, ln1_b, w_router, b_router, w_gate_up, b_gate_up,
           w_down, b_down, ln2_g, ln2_b, rel_bias):
    B, S, D = x.shape
    x2 = x.reshape(B * S, D)
    for l in range(w_in.shape[0]):
        x2 = _layer(x2, B, S, w_in[l], conv_w[l], conv_b[l], mlstm_i_bias[l], mlstm_f_bias[l],
                    mlstm_norm_g[l], w_branch_attn[l], w_branch_mlstm[l], w_out[l], ln1_g[l], ln1_b[l],
                    w_router[l], b_router[l], w_gate_up[l], b_gate_up[l], w_down[l], b_down[l],
                    ln2_g[l], ln2_b[l], rel_bias)
    return x2.reshape(B, S, D)
```

```python
import functools
import math

import jax
import jax.numpy as jnp
from jax import lax
from jax.experimental import pallas as pl
from jax.experimental.pallas import tpu as pltpu
from jax.experimental.pallas import tpu_sc as plsc

ATT_HEADS = 8
ATT_HEAD_DIM = 64
ATT_WIDTH = ATT_HEADS * ATT_HEAD_DIM
IDX_HEADS = 8
IDX_HEAD_DIM = 64
TOPK_MAX = 256
MLSTM_HEADS = 4
MLSTM_QK_DIM = 64
MLSTM_V_DIM = 128
MLSTM_WIDTH = MLSTM_HEADS * MLSTM_V_DIM
CONV_WIDTH = 4
N_BUCKETS = 32
MAX_DISTANCE = 128
N_EXPERTS = 32
TOP_K = 4
SWIGLU_ALPHA = 1.702
SWIGLU_LIMIT = 7.0
LN_EPS = 1e-5
DEPTH = 1
DEEPNORM_ALPHA = (2 * DEPTH) ** 0.25

LANES = 128
SUBLANES = 8
VMEM_LIMIT_BYTES = 56 * 1024 * 1024
SC_CORES = 2
SC_SUBCORES = 16
SC_CHUNK = 128

MXU_DTYPE = jnp.bfloat16

INT_MIN = -(2 ** 31)
NEG_BIG = -1e30

QK_M = MLSTM_HEADS * MLSTM_QK_DIM
COLS_A = 2 * ATT_WIDTH + IDX_HEADS * IDX_HEAD_DIM
COLS_S = LANES
COLS_M = 2 * QK_M + 2 * MLSTM_WIDTH
COLS_K = 2 * IDX_HEAD_DIM
S_WI = IDX_HEAD_DIM
S_IM = S_WI + IDX_HEADS
S_FM = S_IM + MLSTM_HEADS


def _cparams(sem):
    return pltpu.CompilerParams(dimension_semantics=sem, vmem_limit_bytes=VMEM_LIMIT_BYTES)


def _proj_kernel(x_ref, w_ref, oa_ref, os_ref, om_ref, og_ref, ok_ref, ovt_ref, *, d_model):
    xb = x_ref[...].astype(MXU_DTYPE)
    step = 512

    def mm(lo, hi):
        return jnp.dot(xb, w_ref[:, lo:hi], preferred_element_type=jnp.float32)

    base = 0
    for j in range(0, COLS_A, step):
        oa_ref[:, j:j + step] = mm(base + j, base + j + step).astype(MXU_DTYPE)
    base += COLS_A
    os_ref[...] = mm(base, base + COLS_S)
    base += COLS_S
    for j in range(0, COLS_M, step):
        om_ref[:, j:j + step] = mm(base + j, base + j + step)
    base += COLS_M
    for j in range(0, 2 * d_model, step):
        og_ref[:, j:j + step] = mm(base + j, base + j + step)
    base += 2 * d_model
    ok_ref[...] = mm(base, base + COLS_K).astype(MXU_DTYPE)
    base += COLS_K
    v = mm(base, base + ATT_WIDTH)
    tq = ovt_ref.shape[2]
    for s in range(ovt_ref.shape[0]):
        ovt_ref[s] = v[s * tq:(s + 1) * tq, :].T.astype(MXU_DTYPE)


def _project(x2, w_p, tm, tq):
    T, D = x2.shape
    n_all = w_p.shape[1]
    assert tm % tq == 0
    return pl.pallas_call(
        functools.partial(_proj_kernel, d_model=D),
        grid=(T // tm,),
        in_specs=[pl.BlockSpec((tm, D), lambda i: (i, 0)),
                  pl.BlockSpec((D, n_all), lambda i: (0, 0))],
        out_specs=[pl.BlockSpec((tm, COLS_A), lambda i: (i, 0)),
                   pl.BlockSpec((tm, COLS_S), lambda i: (i, 0)),
                   pl.BlockSpec((tm, COLS_M), lambda i: (i, 0)),
                   pl.BlockSpec((tm, 2 * D), lambda i: (i, 0)),
                   pl.BlockSpec((tm, COLS_K), lambda i: (i, 0)),
                   pl.BlockSpec((tm // tq, ATT_WIDTH, tq), lambda i: (i, 0, 0))],
        out_shape=[jax.ShapeDtypeStruct((T, COLS_A), MXU_DTYPE),
                   jax.ShapeDtypeStruct((T, COLS_S), jnp.float32),
                   jax.ShapeDtypeStruct((T, COLS_M), jnp.float32),
                   jax.ShapeDtypeStruct((T, 2 * D), jnp.float32),
                   jax.ShapeDtypeStruct((T, COLS_K), MXU_DTYPE),
                   jax.ShapeDtypeStruct((T // tq, ATT_WIDTH, tq), MXU_DTYPE)],
        compiler_params=_cparams(("parallel",)),
    )(x2, w_p)


def _bias_kernel(rb_ref, o_ref, *, tq):
    h = pl.program_id(0)
    s = lax.broadcasted_iota(jnp.int32, (tq, tq), 0)
    t = lax.broadcasted_iota(jnp.int32, (tq, tq), 1)
    max_exact = N_BUCKETS // 2
    for d in range(3):
        n = jnp.maximum(t - s + d * tq, 0)
        n_f = jnp.maximum(n, 1).astype(jnp.float32)
        large = max_exact + (jnp.log(n_f / max_exact) / math.log(MAX_DISTANCE / max_exact)
                             * (N_BUCKETS - max_exact)).astype(jnp.int32)
        large = jnp.minimum(large, N_BUCKETS - 1)
        bucket = jnp.where(n < max_exact, n, large)
        acc = jnp.zeros((tq, tq), jnp.float32)
        for k in range(N_BUCKETS):
            acc = jnp.where(bucket == k, rb_ref[h, k], acc)
        o_ref[0, d] = acc


def _bias_tiles(rel_bias, tq):
    assert tq + 1 >= MAX_DISTANCE
    H = rel_bias.shape[0]
    return pl.pallas_call(
        functools.partial(_bias_kernel, tq=tq),
        grid=(H,),
        in_specs=[pl.BlockSpec(memory_space=pltpu.SMEM)],
        out_specs=pl.BlockSpec((1, 3, tq, tq), lambda h: (h, 0, 0, 0)),
        out_shape=jax.ShapeDtypeStruct((H, 3, tq, tq), jnp.float32),
        compiler_params=_cparams(("parallel",)),
    )(rel_bias)


def _dsa_kernel(qa_ref, ka_ref, vt_ref, qi_ref, kk_ref, wq_ref, bias_ref, o_ref,
                keys_ref, hi_ref, lo_ref, msk_ref, qz_ref, qiz_ref, s_ref, p_ref, acc_ref, m_ref, l_ref,
                *, tq, n_sel):
    i = pl.program_id(1)
    nch = i + 1
    t_pos = i * tq + lax.broadcasted_iota(jnp.int32, (1, tq), 1)
    s_loc = lax.broadcasted_iota(jnp.int32, (tq, 1), 0)
    hd = ATT_HEAD_DIM
    n_pairs = ATT_HEADS // 2
    assert 2 * hd == LANES and IDX_HEAD_DIM == hd and IDX_HEADS == ATT_HEADS

    top = lax.broadcasted_iota(jnp.int32, (LANES, tq), 0) < hd

    def pair_operand(blk):
        bt = blk.astype(jnp.float32).T
        return jnp.concatenate([jnp.where(top, bt, 0.0), jnp.where(top, 0.0, bt)],
                               axis=1).astype(MXU_DTYPE)

    for p in range(n_pairs):
        pair = slice(p * LANES, (p + 1) * LANES)
        qz_ref[p] = pair_operand(qa_ref[:, pair] * (hd ** -0.5))
        qiz_ref[p] = pair_operand(qi_ref[:, pair])

    ws = wq_ref[...].T[S_WI:S_WI + IDX_HEADS, :] * (IDX_HEADS ** -0.5)

    def score_tile(c):
        off = pl.multiple_of(c * tq, tq)
        kk = kk_ref[pl.ds(off, tq), :]
        sc = jnp.zeros((tq, tq), jnp.float32)
        for p in range(n_pairs):
            d = jnp.dot(kk, qiz_ref[p], preferred_element_type=jnp.float32)
            for j in range(2):
                h = 2 * p + j
                sc = sc + ws[h:h + 1, :] * jnp.maximum(d[:, j * tq:(j + 1) * tq], 0.0)
        sc = sc + 0.0
        b = pltpu.bitcast(sc, jnp.int32)
        sk = b ^ ((b >> 31) & jnp.int32(0x7FFFFFFF))
        sk = jnp.where(off + s_loc <= t_pos, sk, jnp.int32(INT_MIN))
        keys_ref[c] = sk
        hi_ref[c] = (sk >> 16).astype(jnp.int16)

    def score_pair(c2, carry):
        score_tile(2 * c2)
        score_tile(2 * c2 + 1)
        return carry

    def score_single(c, carry):
        score_tile(c)
        return carry

    lax.fori_loop(0, nch // 2, score_pair, 0)
    lax.fori_loop(2 * (nch // 2), nch, score_single, 0)

    pack = 2 * SUBLANES
    half_min = -(2 ** 15)

    def count16(ref, pred_fn):
        def tile_count(c):
            hit = jnp.where(pred_fn(ref[c]), jnp.int16(1), jnp.int16(0))
            parts = [hit[r * pack:(r + 1) * pack, :] for r in range(tq // pack)]
            while len(parts) > 1:
                parts = [a + b for a, b in zip(parts[::2], parts[1::2])]
            return parts[0]
        acc = lax.fori_loop(0, nch // 2,
                            lambda c2, a: a + (tile_count(2 * c2) + tile_count(2 * c2 + 1)),
                            jnp.zeros((pack, tq), jnp.int16))
        acc = lax.fori_loop(2 * (nch // 2), nch, lambda c, a: a + tile_count(c), acc)
        return jnp.sum(acc.astype(jnp.float32), axis=0, keepdims=True)

    def search16(ref, target):
        def bit_body(it, u):
            cand_u = u | lax.shift_left(jnp.int32(1), 15 - it)
            cand = (cand_u + half_min).astype(jnp.int16)
            return jnp.where(count16(ref, lambda x: x >= cand) >= target, cand_u, u)
        return lax.fori_loop(0, 16, bit_body, jnp.zeros((1, tq), jnp.int32))

    u_hi = search16(hi_ref, float(n_sel))
    thr_hi = (u_hi + half_min).astype(jnp.int16)
    need_lo = n_sel - count16(hi_ref, lambda x: x > thr_hi)

    def low_body(c, carry):
        lo = ((keys_ref[c] & 0xFFFF) + half_min).astype(jnp.int16)
        lo_ref[c] = jnp.where(hi_ref[c] == thr_hi, lo, jnp.int16(half_min))
        return carry

    lax.fori_loop(0, nch, low_body, 0)
    u_lo = search16(lo_ref, need_lo)
    thr_lo = (u_lo + half_min).astype(jnp.int16)
    thr = lax.shift_left(u_hi + half_min, 16) | u_lo
    need = jnp.where(thr == INT_MIN, 0.0, need_lo - count16(lo_ref, lambda x: x > thr_lo))
    n_eq = count16(lo_ref, lambda x: x == thr_lo)
    surplus = jnp.where((thr != INT_MIN) & (n_eq > need), 1.0, 0.0)
    has_surplus = jnp.max(surplus) > 0.0

    @pl.when(has_surplus)
    def _():
        tri = jnp.where(lax.broadcasted_iota(jnp.int32, (tq, tq), 0)
                        >= lax.broadcasted_iota(jnp.int32, (tq, tq), 1), 1.0, 0.0).astype(MXU_DTYPE)

        def mask_body(c, run):
            kc = keys_ref[c]
            eq = kc == thr
            eqf = jnp.where(eq, 1.0, 0.0)
            prefix = jnp.dot(tri, eqf.astype(MXU_DTYPE), preferred_element_type=jnp.float32) + run
            msk_ref[c] = jnp.where(kc > thr, 0.0,
                                   jnp.where(eq, jnp.where(prefix <= need, 0.0, NEG_BIG), NEG_BIG))
            return run + jnp.sum(eqf, axis=0, keepdims=True)

        lax.fori_loop(0, nch, mask_body, jnp.zeros((1, tq), jnp.float32))

    @pl.when(jnp.logical_not(has_surplus))
    def _():
        floor = jnp.maximum(thr, INT_MIN + 1)

        def mask_body(c, carry):
            msk_ref[c] = jnp.where(keys_ref[c] >= floor, 0.0, NEG_BIG)
            return carry

        lax.fori_loop(0, nch, mask_body, 0)

    m_ref[...] = jnp.full(m_ref.shape, NEG_BIG, jnp.float32)
    l_ref[...] = jnp.zeros(l_ref.shape, jnp.float32)
    acc_ref[...] = jnp.zeros(acc_ref.shape, jnp.float32)

    def att_step(cs, far):
        tile_max = [None] * ATT_HEADS
        for n, c in enumerate(cs):
            off = pl.multiple_of(c * tq, tq)
            mk = msk_ref[c]
            for p in range(n_pairs):
                pair = slice(p * LANES, (p + 1) * LANES)
                s2 = jnp.dot(ka_ref[pl.ds(off, tq), pair], qz_ref[p],
                             preferred_element_type=jnp.float32)
                for j in range(2):
                    h = 2 * p + j
                    s = s2[:, j * tq:(j + 1) * tq] + mk
                    if not far:
                        s = s + bias_ref[h, i - c]
                    s_ref[n, h] = s
                    mx = jnp.max(s, axis=0, keepdims=True)
                    tile_max[h] = mx if tile_max[h] is None else jnp.maximum(tile_max[h], mx)
        alphas = []
        for h in range(ATT_HEADS):
            m_prev = m_ref[h:h + 1, :]
            if far:
                b_far = bias_ref[h, 2, 0:1, 0:1]
                m_new = jnp.maximum(m_prev, tile_max[h] + b_far)
                shift = m_new - b_far
            else:
                m_new = jnp.maximum(m_prev, tile_max[h])
                shift = m_new
            alpha = jnp.exp(m_prev - m_new)
            for n in range(len(cs)):
                pe = jnp.exp(s_ref[n, h] - shift)
                p_ref[n, h // 2, :, (h % 2) * tq:(h % 2 + 1) * tq] = pe.astype(MXU_DTYPE)
            m_ref[h:h + 1, :] = m_new
            alphas.append(alpha)
        ones_rows = jnp.ones((2 * SUBLANES, tq), MXU_DTYPE)
        for p in range(n_pairs):
            pv = None
            for n, c in enumerate(cs):
                va = jnp.concatenate([vt_ref[c, p * LANES:(p + 1) * LANES, :], ones_rows], axis=0)
                d = jnp.dot(va, p_ref[n, p], preferred_element_type=jnp.float32)
                pv = d if pv is None else pv + d
            for j in range(2):
                h = 2 * p + j
                lanes = slice(j * tq, (j + 1) * tq)
                acc_ref[h] = alphas[h] * acc_ref[h] + pv[j * hd:(j + 1) * hd, lanes]
                l_ref[h:h + 1, :] = alphas[h] * l_ref[h:h + 1, :] + pv[2 * hd:2 * hd + 1, lanes]

    def far_pair(c2, carry):
        att_step([2 * c2, 2 * c2 + 1], True)
        return carry

    def far_single(c, carry):
        att_step([c], True)
        return carry

    n_far = jnp.maximum(i - 1, 0)
    lax.fori_loop(0, n_far // 2, far_pair, 0)
    lax.fori_loop(2 * (n_far // 2), n_far, far_single, 0)

    @pl.when(i > 0)
    def _():
        att_step([i - 1, i], False)

    @pl.when(i == 0)
    def _():
        att_step([i], False)
    y_t = jnp.concatenate([acc_ref[h] / l_ref[h:h + 1, :] for h in range(ATT_HEADS)], axis=0)
    o_ref[...] = y_t.T.astype(o_ref.dtype)


def _dsa_attention(oa, osm, okk, v_t, bias_t, B, S, tq):
    nq = S // tq
    n_sel = min(TOPK_MAX, S // 4)
    W = ATT_WIDTH
    assert v_t.shape == (B * nq, W, tq)
    return pl.pallas_call(
        functools.partial(_dsa_kernel, tq=tq, n_sel=n_sel),
        grid=(B, nq),
        in_specs=[pl.BlockSpec((tq, W), lambda b, i: (b * nq + i, 0)),
                  pl.BlockSpec((S, W), lambda b, i: (b, 1)),
                  pl.BlockSpec((nq, W, tq), lambda b, i: (b, 0, 0)),
                  pl.BlockSpec((tq, W), lambda b, i: (b * nq + i, 2)),
                  pl.BlockSpec((S, COLS_K), lambda b, i: (b, 0)),
                  pl.BlockSpec((tq, COLS_S), lambda b, i: (b * nq + i, 0)),
                  pl.BlockSpec((ATT_HEADS, 3, tq, tq), lambda b, i: (0, 0, 0, 0))],
        out_specs=pl.BlockSpec((tq, W), lambda b, i: (b * nq + i, 0)),
        out_shape=jax.ShapeDtypeStruct((B * S, W), MXU_DTYPE),
        scratch_shapes=[pltpu.VMEM((nq, tq, tq), jnp.int32),
                        pltpu.VMEM((nq, tq, tq), jnp.int16),
                        pltpu.VMEM((nq, tq, tq), jnp.int16),
                        pltpu.VMEM((nq, tq, tq), jnp.float32),
                        pltpu.VMEM((ATT_HEADS // 2, LANES, 2 * tq), MXU_DTYPE),
                        pltpu.VMEM((IDX_HEADS // 2, LANES, 2 * tq), MXU_DTYPE),
                        pltpu.VMEM((2, ATT_HEADS, tq, tq), jnp.float32),
                        pltpu.VMEM((2, ATT_HEADS // 2, tq, 2 * tq), MXU_DTYPE),
                        pltpu.VMEM((ATT_HEADS, ATT_HEAD_DIM, tq), jnp.float32),
                        pltpu.VMEM((ATT_HEADS, tq), jnp.float32),
                        pltpu.VMEM((ATT_HEADS, tq), jnp.float32)],
        compiler_params=pltpu.CompilerParams(
            dimension_semantics=("parallel", "arbitrary"), vmem_limit_bytes=VMEM_LIMIT_BYTES),
    )(oa, oa, v_t, oa, okk, osm, bias_t)


def _mlstm_kernel(qk_ref, v_ref, og_ref, s_ref, cw_ref, cb_ref, gb_ref, ng_ref, y_ref,
                  xbuf, cst, mst, qk_s, in_s, p_s, kw_s, *, L):
    c = pl.program_id(1)
    H, DK, DV = MLSTM_HEADS, MLSTM_QK_DIM, MLSTM_V_DIM
    tail = SUBLANES

    @pl.when(c == 0)
    def _():
        xbuf[0:tail, :] = jnp.zeros((tail, 2 * QK_M), jnp.float32)
        cst[...] = jnp.zeros(cst.shape, jnp.float32)
        mst[...] = jnp.zeros(mst.shape, jnp.float32)

    @pl.when(c > 0)
    def _():
        xbuf[0:tail, :] = xbuf[L:L + tail, :]

    xbuf[tail:tail + L, :] = qk_ref[...]
    conv = cb_ref[...] + jnp.zeros((L, 2 * QK_M), jnp.float32)
    for j in range(CONV_WIDTH):
        conv = conv + cw_ref[j:j + 1, :] * xbuf[pl.ds(tail - (CONV_WIDTH - 1) + j, L), :]
    qk = conv * jax.nn.sigmoid(conv)
    assert 2 * DK == LANES and DV == LANES
    qT = qk[:, :QK_M].T
    kb = (qk[:, QK_M:] * (DK ** -0.5)).astype(MXU_DTYPE)
    vT = v_ref[...].T

    g = s_ref[...] + gb_ref[...]
    logf = jnp.minimum(g, 0.0) - jnp.log(1.0 + jnp.exp(-jnp.abs(g)))
    row = lax.broadcasted_iota(jnp.int32, (L, L), 0)
    col = lax.broadcasted_iota(jnp.int32, (L, L), 1)
    bcum = jnp.dot(jnp.where(row >= col, 1.0, 0.0), logf, precision=lax.Precision.HIGHEST,
                   preferred_element_type=jnp.float32)
    gT = g.T
    bT = bcum.T
    visible = row <= col
    top = lax.broadcasted_iota(jnp.int32, (LANES, L), 0) < DK
    one_row = jnp.where(lax.broadcasted_iota(jnp.int32, (LANES, L), 0) == 0, 1.0, 0.0)

    for h in range(H):
        pair = slice((h // 2) * LANES, (h // 2 + 1) * LANES)
        qp = qT[pair, :]
        qz = jnp.where(top, qp, 0.0) if h % 2 == 0 else jnp.where(top, 0.0, qp)
        qz = qz.astype(MXU_DTYPE)
        qk_s[h] = jnp.dot(kb[:, pair], qz, preferred_element_type=jnp.float32)
        in_s[h] = jnp.dot(cst[h].astype(MXU_DTYPE), qz, preferred_element_type=jnp.float32)

    stats = []
    for h in range(H):
        acol = g[:, S_IM + h:S_IM + h + 1] - bcum[:, S_FM + h:S_FM + h + 1]
        br = bT[S_FM + h:S_FM + h + 1, :]
        lir = gT[S_IM + h:S_IM + h + 1, :]
        m0 = mst[h:h + 1, 0:1]
        d = jnp.where(visible, br + acol, -jnp.inf)
        m_inter = br + m0
        m_t = jnp.maximum(m_inter, jnp.max(d, axis=0, keepdims=True))
        p = qk_s[h] * jnp.exp(d - m_t)
        p_s[h] = p.astype(MXU_DTYPE)
        b_end = br[:, L - 1:L]
        a = b_end - br + lir
        m_new = jnp.maximum(b_end + m0, jnp.max(a, axis=1, keepdims=True))
        vaug = jnp.concatenate([vT[h * DV:(h + 1) * DV, :], one_row], axis=0)
        kw_s[h] = (vaug * jnp.exp(a - m_new)).astype(MXU_DTYPE)
        stats.append((jnp.sum(p, axis=0, keepdims=True), jnp.exp(m_inter - m_t), m_t,
                      jnp.exp(b_end + m0 - m_new), m_new))

    for h in range(H):
        den_intra, sc, m_t, decay, m_new = stats[h]
        pair = slice((h // 2) * LANES, (h // 2 + 1) * LANES)
        inter = in_s[h]
        num = jnp.dot(vT[h * DV:(h + 1) * DV, :].astype(MXU_DTYPE), p_s[h],
                      preferred_element_type=jnp.float32) + sc * inter[:DV, :]
        den = den_intra + sc * inter[DV:DV + 1, :]
        hh = num / jnp.maximum(jnp.abs(den), jnp.exp(-m_t))
        cst[h] = decay * cst[h] + jnp.dot(kw_s[h], kb[:, pair], preferred_element_type=jnp.float32)
        mst[h:h + 1, :] = jnp.broadcast_to(m_new, (1, LANES))
        mu = jnp.mean(hh, axis=0, keepdims=True)
        hc = hh - mu
        var = jnp.mean(hc * hc, axis=0, keepdims=True)
        hn = (hc * lax.rsqrt(var + LN_EPS)).T * ng_ref[:, h * DV:(h + 1) * DV]
        y_ref[:, h * DV:(h + 1) * DV] = (hn * jax.nn.sigmoid(og_ref[:, h * DV:(h + 1) * DV])).astype(y_ref.dtype)


def _mlstm(om, osm, conv_w, conv_b, gate_bias, norm_g, B, S, L):
    nc = S // L
    W = MLSTM_WIDTH
    assert 2 * QK_M == W
    return pl.pallas_call(
        functools.partial(_mlstm_kernel, L=L),
        grid=(B, nc),
        in_specs=[pl.BlockSpec((L, W), lambda b, c: (b * nc + c, 0)),
                  pl.BlockSpec((L, W), lambda b, c: (b * nc + c, 1)),
                  pl.BlockSpec((L, W), lambda b, c: (b * nc + c, 2)),
                  pl.BlockSpec((L, COLS_S), lambda b, c: (b * nc + c, 0)),
                  pl.BlockSpec((CONV_WIDTH, W), lambda b, c: (0, 0)),
                  pl.BlockSpec((1, W), lambda b, c: (0, 0)),
                  pl.BlockSpec((1, COLS_S), lambda b, c: (0, 0)),
                  pl.BlockSpec((1, W), lambda b, c: (0, 0))],
        out_specs=pl.BlockSpec((L, W), lambda b, c: (b * nc + c, 0)),
        out_shape=jax.ShapeDtypeStruct((B * S, W), MXU_DTYPE),
        scratch_shapes=[pltpu.VMEM((L + 2 * SUBLANES, W), jnp.float32),
                        pltpu.VMEM((MLSTM_HEADS, 2 * MLSTM_V_DIM, LANES), jnp.float32),
                        pltpu.VMEM((SUBLANES, LANES), jnp.float32),
                        pltpu.VMEM((MLSTM_HEADS, L, L), jnp.float32),
                        pltpu.VMEM((MLSTM_HEADS, 2 * MLSTM_V_DIM, L), jnp.float32),
                        pltpu.VMEM((MLSTM_HEADS, L, L), MXU_DTYPE),
                        pltpu.VMEM((MLSTM_HEADS, 2 * MLSTM_V_DIM, L), MXU_DTYPE)],
        compiler_params=_cparams(("parallel", "arbitrary")),
    )(om, om, om, osm, conv_w, conv_b, gate_bias, norm_g)


def _layer_norm(z, g, b):
    mu = jnp.mean(z, axis=1, keepdims=True)
    zc = z - mu
    var = jnp.mean(zc * zc, axis=1, keepdims=True)
    return zc * lax.rsqrt(var + LN_EPS) * g + b


def _merge_kernel(ya_ref, ym_ref, g_ref, x_ref, wa_ref, wm_ref, wo_ref, lg_ref, lb_ref,
                  wr_ref, br_ref, x1_ref, xp_ref, idx_ref, gate_ref, cnt_ref, carry_ref, *, d_model):
    D = d_model
    mix = (jax.nn.sigmoid(g_ref[:, :D]) * jnp.dot(ya_ref[...], wa_ref[...], preferred_element_type=jnp.float32)
           + jax.nn.sigmoid(g_ref[:, D:]) * jnp.dot(ym_ref[...], wm_ref[...], preferred_element_type=jnp.float32))
    y = jnp.dot(mix.astype(MXU_DTYPE), wo_ref[...], preferred_element_type=jnp.float32)
    x1 = _layer_norm(DEEPNORM_ALPHA * x_ref[...] + y, lg_ref[...], lb_ref[...])
    x1_ref[...] = x1
    xb = x1.astype(MXU_DTYPE)
    bits = pltpu.bitcast(x1.astype(jnp.bfloat16).astype(jnp.float32), jnp.uint32)
    xp_ref[...] = (bits[:, :D // 2] & jnp.uint32(0xFFFF0000)) | (bits[:, D // 2:] >> 16)

    logits = jnp.dot(xb, wr_ref[...], preferred_element_type=jnp.float32) + br_ref[...]
    tm = logits.shape[0]
    lane = lax.broadcasted_iota(jnp.int32, (tm, LANES), 1)
    lane_f = lane.astype(jnp.float32)
    vals, idxs = [], []
    for _ in range(TOP_K):
        mx = jnp.max(logits, axis=1, keepdims=True)
        ix = jnp.min(jnp.where(logits == mx, lane_f, float(LANES)), axis=1, keepdims=True)
        vals.append(mx)
        idxs.append(ix)
        logits = jnp.where(lane_f == ix, -jnp.inf, logits)
    es = [jnp.exp(v - vals[0]) for v in vals]
    tot = es[0]
    for e in es[1:]:
        tot = tot + e
    @pl.when(pl.program_id(0) == 0)
    def _():
        carry_ref[...] = jnp.zeros(carry_ref.shape, jnp.float32)

    hots = [lane_f == ix for ix in idxs]
    c = jnp.zeros((tm, LANES), jnp.float32)
    for hot in hots:
        c = c + jnp.where(hot, 1.0, 0.0)
    before = jnp.where(lax.broadcasted_iota(jnp.int32, (tm, tm), 0)
                       > lax.broadcasted_iota(jnp.int32, (tm, tm), 1), 1.0, 0.0).astype(MXU_DTYPE)
    carry = carry_ref[0:1, :]
    prior = jnp.dot(before, c.astype(MXU_DTYPE), preferred_element_type=jnp.float32) + carry
    total = carry + jnp.sum(c, axis=0, keepdims=True)
    carry_ref[...] = jnp.broadcast_to(total, carry_ref.shape)
    cnt_ref[...] = jnp.broadcast_to(total, cnt_ref.shape)

    idx_out = jnp.zeros((tm, LANES), jnp.float32)
    gate_out = jnp.zeros((tm, LANES), jnp.float32)
    for k in range(TOP_K):
        idx_out = jnp.where(lane == k, idxs[k], idx_out)
        rank_k = jnp.sum(jnp.where(hots[k], prior, 0.0), axis=1, keepdims=True)
        idx_out = jnp.where(lane == TOP_K + k, rank_k, idx_out)
        gate_out = jnp.where(lane == k, es[k] / tot, gate_out)
    idx_ref[...] = idx_out.T[:2 * TOP_K, :].astype(jnp.int32)
    gate_ref[...] = gate_out


def _merge(y_att, y_m, og, x2, wa, wm, wo, ln_g, ln_b, wr, br, tm):
    T, D = x2.shape
    full = lambda shape: pl.BlockSpec(shape, lambda i: (0, 0))
    return pl.pallas_call(
        functools.partial(_merge_kernel, d_model=D),
        grid=(T // tm,),
        in_specs=[pl.BlockSpec((tm, ATT_WIDTH), lambda i: (i, 0)),
                  pl.BlockSpec((tm, MLSTM_WIDTH), lambda i: (i, 0)),
                  pl.BlockSpec((tm, 2 * D), lambda i: (i, 0)),
                  pl.BlockSpec((tm, D), lambda i: (i, 0)),
                  full(wa.shape), full(wm.shape), full(wo.shape),
                  full((1, D)), full((1, D)), full(wr.shape), full((1, LANES))],
        out_specs=[pl.BlockSpec((tm, D), lambda i: (i, 0)),
                   pl.BlockSpec((tm, D // 2), lambda i: (i, 0)),
                   pl.BlockSpec((2 * TOP_K, tm), lambda i: (0, i)),
                   pl.BlockSpec((tm, LANES), lambda i: (i, 0)),
                   pl.BlockSpec((SUBLANES, LANES), lambda i: (0, 0))],
        out_shape=[jax.ShapeDtypeStruct((T, D), jnp.float32),
                   jax.ShapeDtypeStruct((T, D // 2), jnp.uint32),
                   jax.ShapeDtypeStruct((2 * TOP_K, T), jnp.int32),
                   jax.ShapeDtypeStruct((T, LANES), jnp.float32),
                   jax.ShapeDtypeStruct((SUBLANES, LANES), jnp.float32)],
        scratch_shapes=[pltpu.VMEM((SUBLANES, LANES), jnp.float32)],
        compiler_params=_cparams(("arbitrary",)),
    )(y_att, y_m, og, x2, wa, wm, wo, ln_g, ln_b, wr, br)


def _dispatch_sc(dest, xp, cap):
    T, W = xp.shape
    n_chunks = T // SC_CHUNK
    n_workers = SC_CORES * SC_SUBCORES
    assert T % SC_CHUNK == 0 and n_chunks % n_workers == 0
    per_worker = n_chunks // n_workers
    idx = dest.reshape(TOP_K, n_chunks, SC_CHUNK).transpose(1, 0, 2)
    mesh = plsc.VectorSubcoreMesh(core_axis_name="c", subcore_axis_name="s")

    @functools.partial(
        pl.kernel, mesh=mesh,
        out_type=jax.ShapeDtypeStruct((cap, W), xp.dtype),
        scratch_types=[pltpu.VMEM((SC_CHUNK, W), xp.dtype),
                       pltpu.VMEM((TOP_K, SC_CHUNK), jnp.int32)])
    def scatter_rows(x_hbm, idx_hbm, xs_hbm, rows_v, idx_v):
        worker = lax.axis_index("s") * SC_CORES + lax.axis_index("c")

        def body(j, carry):
            c = worker * per_worker + j
            pltpu.sync_copy(x_hbm.at[pl.ds(c * SC_CHUNK, SC_CHUNK)], rows_v)
            pltpu.sync_copy(idx_hbm.at[c], idx_v)
            for k in range(TOP_K):
                pltpu.sync_copy(rows_v, xs_hbm.at[idx_v.at[k]])
            return carry

        lax.fori_loop(0, per_worker, body, 0)

    return scatter_rows(xp, idx)


def _ffn_kernel(be_ref, nu_ref, nv_ref, xs_ref, wgu_ref, bgu_ref, wd_ref, bd_ref, y_ref, wgu_b,
                wd_b, act_s, *, d_ff):
    r = pl.program_id(0)
    e = be_ref[r]
    prev = be_ref[jnp.maximum(r - 1, 0)]

    @pl.when((r == 0) | (e != prev))
    def _():
        wgu_b[...] = wgu_ref[0].astype(MXU_DTYPE)
        wd_b[...] = wd_ref[0].astype(MXU_DTYPE)

    @pl.when(r < nu_ref[0])
    def _():
        live = lax.broadcasted_iota(jnp.int32, (xs_ref.shape[0], 1), 0) < nv_ref[r]
        w = jnp.where(live, xs_ref[...], jnp.uint32(0))
        half = w.shape[1]
        x_hi = pltpu.bitcast(w & jnp.uint32(0xFFFF0000), jnp.float32).astype(MXU_DTYPE)
        x_lo = pltpu.bitcast(w << 16, jnp.float32).astype(MXU_DTYPE)
        x = jnp.concatenate([x_hi, x_lo], axis=1)
        step = 512
        for j in range(0, d_ff, step):
            def gu(lo):
                return (jnp.dot(x, wgu_b[:, lo:lo + step], preferred_element_type=jnp.float32)
                        + bgu_ref[0, :, lo:lo + step])
            gate = jnp.minimum(gu(j), SWIGLU_LIMIT)
            up = jnp.clip(gu(d_ff + j), -SWIGLU_LIMIT, SWIGLU_LIMIT)
            act = (up + 1.0) * (gate * jax.nn.sigmoid(SWIGLU_ALPHA * gate))
            act_s[:, j:j + step] = act.astype(MXU_DTYPE)
        y = jnp.dot(act_s[...], wd_b[...], preferred_element_type=jnp.float32) + bd_ref[0]
        bits = pltpu.bitcast(y.astype(jnp.bfloat16).astype(jnp.float32), jnp.uint32)
        y_ref[...] = (bits[:, :half] & jnp.uint32(0xFFFF0000)) | (bits[:, half:] >> 16)

    @pl.when(r >= nu_ref[0])
    def _():
        y_ref[...] = jnp.zeros(y_ref.shape, y_ref.dtype)


def _expert_ffn(block_expert, n_used, n_valid, xs, w_gate_up, b_gate_up, w_down, b_down, bm):
    cap, half = xs.shape
    E, D, F2 = w_gate_up.shape
    d_ff = F2 // 2
    grid_spec = pltpu.PrefetchScalarGridSpec(
        num_scalar_prefetch=3,
        grid=(cap // bm,),
        in_specs=[pl.BlockSpec((bm, half), lambda r, be, nu, nv: (jnp.minimum(r, nu[0] - 1), 0)),
                  pl.BlockSpec((1, D, F2), lambda r, be, nu, nv: (be[r], 0, 0)),
                  pl.BlockSpec((1, 1, F2), lambda r, be, nu, nv: (be[r], 0, 0)),
                  pl.BlockSpec((1, d_ff, D), lambda r, be, nu, nv: (be[r], 0, 0)),
                  pl.BlockSpec((1, 1, D), lambda r, be, nu, nv: (be[r], 0, 0))],
        out_specs=pl.BlockSpec((bm, half), lambda r, be, nu, nv: (r, 0)),
        scratch_shapes=[pltpu.VMEM((D, F2), MXU_DTYPE),
                        pltpu.VMEM((d_ff, D), MXU_DTYPE),
                        pltpu.VMEM((bm, d_ff), MXU_DTYPE)],
    )
    return pl.pallas_call(
        functools.partial(_ffn_kernel, d_ff=d_ff),
        grid_spec=grid_spec,
        out_shape=jax.ShapeDtypeStruct((cap, half), jnp.uint32),
        compiler_params=_cparams(("arbitrary",)),
    )(block_expert, n_used, n_valid, xs, w_gate_up, b_gate_up.reshape(E, 1, F2), w_down,
      b_down.reshape(E, 1, D))


def _gather_sc(dest, ybuf):
    cap, D = ybuf.shape
    T = dest.shape[1]
    chunk = SC_CHUNK
    n_chunks = T // chunk
    n_workers = SC_CORES * SC_SUBCORES
    assert T % chunk == 0 and n_chunks % n_workers == 0
    per_worker = n_chunks // n_workers
    idx = dest.reshape(TOP_K, n_chunks, chunk).transpose(1, 0, 2)
    mesh = plsc.VectorSubcoreMesh(core_axis_name="c", subcore_axis_name="s")

    @functools.partial(
        pl.kernel, mesh=mesh,
        out_type=jax.ShapeDtypeStruct((TOP_K, T, D), ybuf.dtype),
        scratch_types=[pltpu.VMEM((chunk, D), ybuf.dtype),
                       pltpu.VMEM((TOP_K, chunk), jnp.int32)])
    def gather_rows(y_hbm, idx_hbm, out_hbm, rows_v, idx_v):
        worker = lax.axis_index("s") * SC_CORES + lax.axis_index("c")

        def body(j, carry):
            c = worker * per_worker + j
            pltpu.sync_copy(idx_hbm.at[c], idx_v)
            for k in range(TOP_K):
                pltpu.sync_copy(y_hbm.at[idx_v.at[k]], rows_v)
                pltpu.sync_copy(rows_v, out_hbm.at[k, pl.ds(c * chunk, chunk)])
            return carry

        lax.fori_loop(0, per_worker, body, 0)

    return gather_rows(ybuf, idx)


def _combine_dense_kernel(yk_ref, gate_ref, x1_ref, lg_ref, lb_ref, *rest):
    o_ref = rest[-1]
    left = right = None
    for k in range(TOP_K):
        w = yk_ref[k]
        g = gate_ref[:, k:k + 1]
        hi = g * pltpu.bitcast(w & jnp.uint32(0xFFFF0000), jnp.float32)
        lo = g * pltpu.bitcast(w << 16, jnp.float32)
        left = hi if left is None else left + hi
        right = lo if right is None else right + lo
    y = jnp.concatenate([left, right], axis=1)
    o_ref[...] = _layer_norm(DEEPNORM_ALPHA * x1_ref[...] + y, lg_ref[...], lb_ref[...])


def _combine_dense(yk, first_tile, gates, x1, ln_g, ln_b, prev, tm):
    T, D = x1.shape
    n = yk.shape[1] // tm
    in_specs = [pl.BlockSpec((TOP_K, tm, D // 2), lambda i: (0, i, 0)),
                pl.BlockSpec((tm, LANES), lambda i: (i + first_tile, 0)),
                pl.BlockSpec((tm, D), lambda i: (i + first_tile, 0)),
                pl.BlockSpec((1, D), lambda i: (0, 0)),
                pl.BlockSpec((1, D), lambda i: (0, 0))]
    args = [yk, gates, x1, ln_g, ln_b]
    aliases = {}
    if prev is not None:
        in_specs.append(pl.BlockSpec(memory_space=pl.ANY))
        args.append(prev)
        aliases = {len(args) - 1: 0}
    return pl.pallas_call(
        _combine_dense_kernel,
        grid=(n,),
        in_specs=in_specs,
        out_specs=pl.BlockSpec((tm, D), lambda i: (i + first_tile, 0)),
        out_shape=jax.ShapeDtypeStruct((T, D), jnp.float32),
        input_output_aliases=aliases,
        compiler_params=_cparams(("parallel",)),
    )(*args)


def _tile(n, pref):
    t = min(n, pref)
    assert n % t == 0
    return t


def _relayout_w_in(w_in, d_model):
    sizes = (ATT_WIDTH, ATT_WIDTH, ATT_WIDTH, IDX_HEADS * IDX_HEAD_DIM, IDX_HEAD_DIM, IDX_HEADS,
             QK_M, QK_M, MLSTM_WIDTH, MLSTM_HEADS, MLSTM_HEADS, MLSTM_WIDTH, d_model, d_model)
    offs = [0]
    for s in sizes:
        offs.append(offs[-1] + s)
    seg = lambda k: w_in[:, offs[k]:offs[k + 1]]
    (q_a, k_a, v_a, q_i, k_i, w_i, q_m, k_m, v_m, i_m, f_m, o_m, g_a, g_m) = [seg(k) for k in range(14)]
    pad = jnp.zeros((w_in.shape[0], COLS_S - (IDX_HEAD_DIM + IDX_HEADS + 2 * MLSTM_HEADS)), w_in.dtype)
    cols = [q_a, k_a, q_i, k_i, w_i, i_m, f_m, pad, q_m, k_m, v_m, o_m, g_a, g_m, k_i, k_i, v_a]
    return jnp.concatenate(cols, axis=1).astype(MXU_DTYPE)


def _layer(x2, B, S, w_in, conv_w, conv_b, i_bias, f_bias, norm_g, w_branch_attn, w_branch_mlstm,
           w_out, ln1_g, ln1_b, w_router, b_router, w_gate_up, b_gate_up, w_down, b_down,
           ln2_g, ln2_b, rel_bias):
    T, D = x2.shape
    bf = MXU_DTYPE
    tq = _tile(S, 256)
    L = _tile(S, 256)
    tm = _tile(T, 256)

    oa, osm, om, og, okk, v_t = _project(x2, _relayout_w_in(w_in, D), _tile(T, 512), tq)
    bias_t = _bias_tiles(rel_bias, tq)
    y_att = _dsa_attention(oa, osm, okk, v_t, bias_t, B, S, tq)

    gate_bias = jnp.zeros((1, COLS_S), jnp.float32)
    gate_bias = gate_bias.at[0, S_IM:S_IM + MLSTM_HEADS].set(i_bias)
    gate_bias = gate_bias.at[0, S_FM:S_FM + MLSTM_HEADS].set(f_bias)
    y_m = _mlstm(om, osm, conv_w, conv_b.reshape(1, -1), gate_bias, norm_g.reshape(1, -1), B, S, L)

    wr = jnp.zeros((D, LANES), bf).at[:, :N_EXPERTS].set(w_router.astype(bf))
    br = jnp.full((1, LANES), NEG_BIG, jnp.float32).at[0, :N_EXPERTS].set(b_router)
    x1, xp, idx, gates, cnt = _merge(y_att, y_m, og, x2, w_branch_attn.astype(bf),
                                     w_branch_mlstm.astype(bf), w_out.astype(bf), ln1_g.reshape(1, D),
                                     ln1_b.reshape(1, D), wr, br, _tile(T, 512))

    bm = 1024
    counts = cnt[0, :N_EXPERTS].astype(jnp.int32)
    padded = ((counts + bm - 1) // bm) * bm
    pend = jnp.cumsum(padded)
    pstart = pend - padded
    cap = ((T * TOP_K + bm - 1) // bm) * bm + N_EXPERTS * bm
    n_blocks = cap // bm
    experts = jnp.arange(N_EXPERTS, dtype=jnp.int32)[:, None, None]
    dest = jnp.sum(jnp.where(idx[None, :TOP_K, :] == experts, pstart[:, None, None], 0), axis=0) \
        + idx[TOP_K:, :]
    block_row = jnp.arange(n_blocks, dtype=jnp.int32) * bm
    block_expert = jnp.minimum(jnp.sum((pend[None, :] <= block_row[:, None]).astype(jnp.int32), axis=1),
                               N_EXPERTS - 1)
    n_used = (pend[-1:] // bm).astype(jnp.int32)
    n_valid = jnp.clip((pstart + counts)[block_expert] - block_row, 0, bm).astype(jnp.int32)

    xs = _dispatch_sc(dest, xp, cap)
    ybuf = _expert_ffn(block_expert, n_used, n_valid, xs, w_gate_up, b_gate_up, w_down, b_down, bm)
    quantum = SC_CORES * SC_SUBCORES * SC_CHUNK
    parts = (1, 1, 2, 4) if T % (8 * quantum) == 0 else (1,)
    bounds = [0]
    for p in parts:
        bounds.append(bounds[-1] + p * T // sum(parts))
    slabs = [_gather_sc(dest[:, lo:hi], ybuf) for lo, hi in zip(bounds[:-1], bounds[1:])]
    out = None
    tc = _tile(T // sum(parts), 1024)
    for lo, yk in zip(bounds[:-1], slabs):
        out = _combine_dense(yk, lo // tc, gates, x1, ln2_g.reshape(1, D), ln2_b.reshape(1, D),
                             out, tc)
    return out


def kernel(x, w_in, conv_w, conv_b, mlstm_i_bias, mlstm_f_bias, mlstm_norm_g, w_branch_attn,
           w_branch_mlstm, w_out, ln1_g, ln1_b, w_router, b_router, w_gate_up, b_gate_up,
           w_down, b_down, ln2_g, ln2_b, rel_bias):
    B, S, D = x.shape
    x2 = x.reshape(B * S, D)
    for l in range(w_in.shape[0]):
        x2 = _layer(x2, B, S, w_in[l], conv_w[l], conv_b[l], mlstm_i_bias[l], mlstm_f_bias[l],
                    mlstm_norm_g[l], w_branch_attn[l], w_branch_mlstm[l], w_out[l], ln1_g[l], ln1_b[l],
                    w_router[l], b_router[l], w_gate_up[l], b_gate_up[l], w_down[l], b_down[l],
                    ln2_g[l], ln2_b[l], rel_bias)
    return x2.reshape(B, S, D)
```

```python
import functools
import math

import jax
import jax.numpy as jnp
from jax import lax
from jax.experimental import pallas as pl
from jax.experimental.pallas import tpu as pltpu
from jax.experimental.pallas import tpu_sc as plsc

ATT_HEADS = 8
ATT_HEAD_DIM = 64
ATT_WIDTH = ATT_HEADS * ATT_HEAD_DIM
IDX_HEADS = 8
IDX_HEAD_DIM = 64
TOPK_MAX = 256
MLSTM_HEADS = 4
MLSTM_QK_DIM = 64
MLSTM_V_DIM = 128
MLSTM_WIDTH = MLSTM_HEADS * MLSTM_V_DIM
CONV_WIDTH = 4
N_BUCKETS = 32
MAX_DISTANCE = 128
N_EXPERTS = 32
TOP_K = 4
SWIGLU_ALPHA = 1.702
SWIGLU_LIMIT = 7.0
LN_EPS = 1e-5
DEPTH = 1
DEEPNORM_ALPHA = (2 * DEPTH) ** 0.25

LANES = 128
SUBLANES = 8
VMEM_LIMIT_BYTES = 56 * 1024 * 1024
SC_CORES = 2
SC_SUBCORES = 16
SC_CHUNK = 128

MXU_DTYPE = jnp.bfloat16

INT_MIN = -(2 ** 31)
NEG_BIG = -1e30

QK_M = MLSTM_HEADS * MLSTM_QK_DIM
COLS_A = 2 * ATT_WIDTH + IDX_HEADS * IDX_HEAD_DIM
COLS_S = LANES
COLS_M = 2 * QK_M + 2 * MLSTM_WIDTH
COLS_K = 2 * IDX_HEAD_DIM
S_WI = IDX_HEAD_DIM
S_IM = S_WI + IDX_HEADS
S_FM = S_IM + MLSTM_HEADS


def _cparams(sem):
    return pltpu.CompilerParams(dimension_semantics=sem, vmem_limit_bytes=VMEM_LIMIT_BYTES)


def _proj_kernel(x_ref, w_ref, oa_ref, os_ref, om_ref, og_ref, ok_ref, ovt_ref, *, d_model):
    xb = x_ref[...].astype(MXU_DTYPE)
    step = 512

    def mm(lo, hi):
        return jnp.dot(xb, w_ref[:, lo:hi], preferred_element_type=jnp.float32)

    base = 0
    for j in range(0, COLS_A, step):
        oa_ref[:, j:j + step] = mm(base + j, base + j + step).astype(MXU_DTYPE)
    base += COLS_A
    os_ref[...] = mm(base, base + COLS_S)
    base += COLS_S
    for j in range(0, COLS_M, step):
        om_ref[:, j:j + step] = mm(base + j, base + j + step)
    base += COLS_M
    for j in range(0, 2 * d_model, step):
        og_ref[:, j:j + step] = mm(base + j, base + j + step)
    base += 2 * d_model
    ok_ref[...] = mm(base, base + COLS_K).astype(MXU_DTYPE)
    base += COLS_K
    v = mm(base, base + ATT_WIDTH)
    tq = ovt_ref.shape[2]
    for s in range(ovt_ref.shape[0]):
        ovt_ref[s] = v[s * tq:(s + 1) * tq, :].T.astype(MXU_DTYPE)


def _project(x2, w_p, tm, tq):
    T, D = x2.shape
    n_all = w_p.shape[1]
    assert tm % tq == 0
    return pl.pallas_call(
        functools.partial(_proj_kernel, d_model=D),
        grid=(T // tm,),
        in_specs=[pl.BlockSpec((tm, D), lambda i: (i, 0)),
                  pl.BlockSpec((D, n_all), lambda i: (0, 0))],
        out_specs=[pl.BlockSpec((tm, COLS_A), lambda i: (i, 0)),
                   pl.BlockSpec((tm, COLS_S), lambda i: (i, 0)),
                   pl.BlockSpec((tm, COLS_M), lambda i: (i, 0)),
                   pl.BlockSpec((tm, 2 * D), lambda i: (i, 0)),
                   pl.BlockSpec((tm, COLS_K), lambda i: (i, 0)),
                   pl.BlockSpec((tm // tq, ATT_WIDTH, tq), lambda i: (i, 0, 0))],
        out_shape=[jax.ShapeDtypeStruct((T, COLS_A), MXU_DTYPE),
                   jax.ShapeDtypeStruct((T, COLS_S), jnp.float32),
                   jax.ShapeDtypeStruct((T, COLS_M), jnp.float32),
                   jax.ShapeDtypeStruct((T, 2 * D), jnp.float32),
                   jax.ShapeDtypeStruct((T, COLS_K), MXU_DTYPE),
                   jax.ShapeDtypeStruct((T // tq, ATT_WIDTH, tq), MXU_DTYPE)],
        compiler_params=_cparams(("parallel",)),
    )(x2, w_p)


def _bias_kernel(rb_ref, o_ref, *, tq):
    h = pl.program_id(0)
    s = lax.broadcasted_iota(jnp.int32, (tq, tq), 0)
    t = lax.broadcasted_iota(jnp.int32, (tq, tq), 1)
    max_exact = N_BUCKETS // 2
    for d in range(3):
        n = jnp.maximum(t - s + d * tq, 0)
        n_f = jnp.maximum(n, 1).astype(jnp.float32)
        large = max_exact + (jnp.log(n_f / max_exact) / math.log(MAX_DISTANCE / max_exact)
                             * (N_BUCKETS - max_exact)).astype(jnp.int32)
        large = jnp.minimum(large, N_BUCKETS - 1)
        bucket = jnp.where(n < max_exact, n, large)
        acc = jnp.zeros((tq, tq), jnp.float32)
        for k in range(N_BUCKETS):
            acc = jnp.where(bucket == k, rb_ref[h, k], acc)
        o_ref[0, d] = acc


def _bias_tiles(rel_bias, tq):
    assert tq + 1 >= MAX_DISTANCE
    H = rel_bias.shape[0]
    return pl.pallas_call(
        functools.partial(_bias_kernel, tq=tq),
        grid=(H,),
        in_specs=[pl.BlockSpec(memory_space=pltpu.SMEM)],
        out_specs=pl.BlockSpec((1, 3, tq, tq), lambda h: (h, 0, 0, 0)),
        out_shape=jax.ShapeDtypeStruct((H, 3, tq, tq), jnp.float32),
        compiler_params=_cparams(("parallel",)),
    )(rel_bias)


def _dsa_kernel(qa_ref, ka_ref, vt_ref, qi_ref, kk_ref, wq_ref, bias_ref, o_ref,
                keys_ref, hi_ref, lo_ref, msk_ref, qz_ref, qiz_ref, s_ref, p_ref, acc_ref, m_ref, l_ref,
                *, tq, n_sel):
    i = pl.program_id(1)
    nch = i + 1
    t_pos = i * tq + lax.broadcasted_iota(jnp.int32, (1, tq), 1)
    s_loc = lax.broadcasted_iota(jnp.int32, (tq, 1), 0)
    hd = ATT_HEAD_DIM
    n_pairs = ATT_HEADS // 2
    assert 2 * hd == LANES and IDX_HEAD_DIM == hd and IDX_HEADS == ATT_HEADS

    top = lax.broadcasted_iota(jnp.int32, (LANES, tq), 0) < hd

    def pair_operand(blk):
        bt = blk.astype(jnp.float32).T
        return jnp.concatenate([jnp.where(top, bt, 0.0), jnp.where(top, 0.0, bt)],
                               axis=1).astype(MXU_DTYPE)

    for p in range(n_pairs):
        pair = slice(p * LANES, (p + 1) * LANES)
        qz_ref[p] = pair_operand(qa_ref[:, pair] * (hd ** -0.5))
        qiz_ref[p] = pair_operand(qi_ref[:, pair])

    ws = wq_ref[...].T[S_WI:S_WI + IDX_HEADS, :] * (IDX_HEADS ** -0.5)

    def score_tile(c):
        off = pl.multiple_of(c * tq, tq)
        kk = kk_ref[pl.ds(off, tq), :]
        sc = jnp.zeros((tq, tq), jnp.float32)
        for p in range(n_pairs):
            d = jnp.dot(kk, qiz_ref[p], preferred_element_type=jnp.float32)
            for j in range(2):
                h = 2 * p + j
                sc = sc + ws[h:h + 1, :] * jnp.maximum(d[:, j * tq:(j + 1) * tq], 0.0)
        sc = sc + 0.0
        b = pltpu.bitcast(sc, jnp.int32)
        sk = b ^ ((b >> 31) & jnp.int32(0x7FFFFFFF))
        sk = jnp.where(off + s_loc <= t_pos, sk, jnp.int32(INT_MIN))
        keys_ref[c] = sk
        hi_ref[c] = (sk >> 16).astype(jnp.int16)

    def score_pair(c2, carry):
        score_tile(2 * c2)
        score_tile(2 * c2 + 1)
        return carry

    def score_single(c, carry):
        score_tile(c)
        return carry

    lax.fori_loop(0, nch // 2, score_pair, 0)
    lax.fori_loop(2 * (nch // 2), nch, score_single, 0)

    pack = 2 * SUBLANES
    half_min = -(2 ** 15)

    def count16(ref, pred_fn):
        def tile_count(c):
            hit = jnp.where(pred_fn(ref[c]), jnp.int16(1), jnp.int16(0))
            parts = [hit[r * pack:(r + 1) * pack, :] for r in range(tq // pack)]
            while len(parts) > 1:
                parts = [a + b for a, b in zip(parts[::2], parts[1::2])]
            return parts[0]
        acc = lax.fori_loop(0, nch // 2,
                            lambda c2, a: a + (tile_count(2 * c2) + tile_count(2 * c2 + 1)),
                            jnp.zeros((pack, tq), jnp.int16))
        acc = lax.fori_loop(2 * (nch // 2), nch, lambda c, a: a + tile_count(c), acc)
        return jnp.sum(acc.astype(jnp.float32), axis=0, keepdims=True)

    def search16(ref, target):
        def bit_body(it, u):
            cand_u = u | lax.shift_left(jnp.int32(1), 15 - it)
            cand = (cand_u + half_min).astype(jnp.int16)
            return jnp.where(count16(ref, lambda x: x >= cand) >= target, cand_u, u)
        return lax.fori_loop(0, 16, bit_body, jnp.zeros((1, tq), jnp.int32))

    u_hi = search16(hi_ref, float(n_sel))
    thr_hi = (u_hi + half_min).astype(jnp.int16)
    need_lo = n_sel - count16(hi_ref, lambda x: x > thr_hi)

    def low_body(c, carry):
        lo = ((keys_ref[c] & 0xFFFF) + half_min).astype(jnp.int16)
        lo_ref[c] = jnp.where(hi_ref[c] == thr_hi, lo, jnp.int16(half_min))
        return carry

    lax.fori_loop(0, nch, low_body, 0)
    u_lo = search16(lo_ref, need_lo)
    thr_lo = (u_lo + half_min).astype(jnp.int16)
    thr = lax.shift_left(u_hi + half_min, 16) | u_lo
    need = jnp.where(thr == INT_MIN, 0.0, need_lo - count16(lo_ref, lambda x: x > thr_lo))
    n_eq = count16(lo_ref, lambda x: x == thr_lo)
    surplus = jnp.where((thr != INT_MIN) & (n_eq > need), 1.0, 0.0)
    has_surplus = jnp.max(surplus) > 0.0

    @pl.when(has_surplus)
    def _():
        tri = jnp.where(lax.broadcasted_iota(jnp.int32, (tq, tq), 0)
                        >= lax.broadcasted_iota(jnp.int32, (tq, tq), 1), 1.0, 0.0).astype(MXU_DTYPE)

        def mask_body(c, run):
            kc = keys_ref[c]
            eq = kc == thr
            eqf = jnp.where(eq, 1.0, 0.0)
            prefix = jnp.dot(tri, eqf.astype(MXU_DTYPE), preferred_element_type=jnp.float32) + run
            msk_ref[c] = jnp.where(kc > thr, 0.0,
                                   jnp.where(eq, jnp.where(prefix <= need, 0.0, NEG_BIG), NEG_BIG))
            return run + jnp.sum(eqf, axis=0, keepdims=True)

        lax.fori_loop(0, nch, mask_body, jnp.zeros((1, tq), jnp.float32))

    @pl.when(jnp.logical_not(has_surplus))
    def _():
        floor = jnp.maximum(thr, INT_MIN + 1)

        def mask_body(c, carry):
            msk_ref[c] = jnp.where(keys_ref[c] >= floor, 0.0, NEG_BIG)
            return carry

        lax.fori_loop(0, nch, mask_body, 0)

    m_ref[...] = jnp.full(m_ref.shape, NEG_BIG, jnp.float32)
    l_ref[...] = jnp.zeros(l_ref.shape, jnp.float32)
    acc_ref[...] = jnp.zeros(acc_ref.shape, jnp.float32)

    def att_step(cs, far):
        tile_max = [None] * ATT_HEADS
        for n, c in enumerate(cs):
            off = pl.multiple_of(c * tq, tq)
            mk = msk_ref[c]
            for p in range(n_pairs):
                pair = slice(p * LANES, (p + 1) * LANES)
                s2 = jnp.dot(ka_ref[pl.ds(off, tq), pair], qz_ref[p],
                             preferred_element_type=jnp.float32)
                for j in range(2):
                    h = 2 * p + j
                    s = s2[:, j * tq:(j + 1) * tq] + mk
                    if not far:
                        s = s + bias_ref[h, i - c]
                    s_ref[n, h] = s
                    mx = jnp.max(s, axis=0, keepdims=True)
                    tile_max[h] = mx if tile_max[h] is None else jnp.maximum(tile_max[h], mx)
        alphas = []
        for h in range(ATT_HEADS):
            m_prev = m_ref[h:h + 1, :]
            if far:
                b_far = bias_ref[h, 2, 0:1, 0:1]
                m_new = jnp.maximum(m_prev, tile_max[h] + b_far)
                shift = m_new - b_far
            else:
                m_new = jnp.maximum(m_prev, tile_max[h])
                shift = m_new
            alpha = jnp.exp(m_prev - m_new)
            for n in range(len(cs)):
                pe = jnp.exp(s_ref[n, h] - shift)
                p_ref[n, h // 2, :, (h % 2) * tq:(h % 2 + 1) * tq] = pe.astype(MXU_DTYPE)
            m_ref[h:h + 1, :] = m_new
            alphas.append(alpha)
        ones_rows = jnp.ones((2 * SUBLANES, tq), MXU_DTYPE)
        for h in range(ATT_HEADS):
            pv = None
            for n, c in enumerate(cs):
                va = jnp.concatenate([vt_ref[c, h * hd:(h + 1) * hd, :], ones_rows], axis=0)
                d = jnp.dot(va, p_ref[n, h // 2, :, (h % 2) * tq:(h % 2 + 1) * tq],
                            preferred_element_type=jnp.float32)
                pv = d if pv is None else pv + d
            acc_ref[h] = alphas[h] * acc_ref[h] + pv[:hd]
            l_ref[h:h + 1, :] = alphas[h] * l_ref[h:h + 1, :] + pv[hd:hd + 1]

    def far_pair(c2, carry):
        att_step([2 * c2, 2 * c2 + 1], True)
        return carry

    def far_single(c, carry):
        att_step([c], True)
        return carry

    n_far = jnp.maximum(i - 1, 0)
    lax.fori_loop(0, n_far // 2, far_pair, 0)
    lax.fori_loop(2 * (n_far // 2), n_far, far_single, 0)

    @pl.when(i > 0)
    def _():
        att_step([i - 1, i], False)

    @pl.when(i == 0)
    def _():
        att_step([i], False)
    y_t = jnp.concatenate([acc_ref[h] / l_ref[h:h + 1, :] for h in range(ATT_HEADS)], axis=0)
    o_ref[...] = y_t.T.astype(o_ref.dtype)


def _dsa_attention(oa, osm, okk, v_t, bias_t, B, S, tq):
    nq = S // tq
    n_sel = min(TOPK_MAX, S // 4)
    W = ATT_WIDTH
    assert v_t.shape == (B * nq, W, tq)
    return pl.pallas_call(
        functools.partial(_dsa_kernel, tq=tq, n_sel=n_sel),
        grid=(B, nq),
        in_specs=[pl.BlockSpec((tq, W), lambda b, i: (b * nq + i, 0)),
                  pl.BlockSpec((S, W), lambda b, i: (b, 1)),
                  pl.BlockSpec((nq, W, tq), lambda b, i: (b, 0, 0)),
                  pl.BlockSpec((tq, W), lambda b, i: (b * nq + i, 2)),
                  pl.BlockSpec((S, COLS_K), lambda b, i: (b, 0)),
                  pl.BlockSpec((tq, COLS_S), lambda b, i: (b * nq + i, 0)),
                  pl.BlockSpec((ATT_HEADS, 3, tq, tq), lambda b, i: (0, 0, 0, 0))],
        out_specs=pl.BlockSpec((tq, W), lambda b, i: (b * nq + i, 0)),
        out_shape=jax.ShapeDtypeStruct((B * S, W), MXU_DTYPE),
        scratch_shapes=[pltpu.VMEM((nq, tq, tq), jnp.int32),
                        pltpu.VMEM((nq, tq, tq), jnp.int16),
                        pltpu.VMEM((nq, tq, tq), jnp.int16),
                        pltpu.VMEM((nq, tq, tq), jnp.float32),
                        pltpu.VMEM((ATT_HEADS // 2, LANES, 2 * tq), MXU_DTYPE),
                        pltpu.VMEM((IDX_HEADS // 2, LANES, 2 * tq), MXU_DTYPE),
                        pltpu.VMEM((2, ATT_HEADS, tq, tq), jnp.float32),
                        pltpu.VMEM((2, ATT_HEADS // 2, tq, 2 * tq), MXU_DTYPE),
                        pltpu.VMEM((ATT_HEADS, ATT_HEAD_DIM, tq), jnp.float32),
                        pltpu.VMEM((ATT_HEADS, tq), jnp.float32),
                        pltpu.VMEM((ATT_HEADS, tq), jnp.float32)],
        compiler_params=pltpu.CompilerParams(
            dimension_semantics=("parallel", "arbitrary"), vmem_limit_bytes=VMEM_LIMIT_BYTES),
    )(oa, oa, v_t, oa, okk, osm, bias_t)


def _mlstm_kernel(qk_ref, v_ref, og_ref, s_ref, cw_ref, cb_ref, gb_ref, ng_ref, y_ref,
                  xbuf, cst, mst, qk_s, in_s, p_s, kw_s, *, L):
    c = pl.program_id(1)
    H, DK, DV = MLSTM_HEADS, MLSTM_QK_DIM, MLSTM_V_DIM
    tail = SUBLANES

    @pl.when(c == 0)
    def _():
        xbuf[0:tail, :] = jnp.zeros((tail, 2 * QK_M), jnp.float32)
        cst[...] = jnp.zeros(cst.shape, jnp.float32)
        mst[...] = jnp.zeros(mst.shape, jnp.float32)

    @pl.when(c > 0)
    def _():
        xbuf[0:tail, :] = xbuf[L:L + tail, :]

    xbuf[tail:tail + L, :] = qk_ref[...]
    conv = cb_ref[...] + jnp.zeros((L, 2 * QK_M), jnp.float32)
    for j in range(CONV_WIDTH):
        conv = conv + cw_ref[j:j + 1, :] * xbuf[pl.ds(tail - (CONV_WIDTH - 1) + j, L), :]
    qk = conv * jax.nn.sigmoid(conv)
    assert 2 * DK == LANES and DV == LANES
    qT = qk[:, :QK_M].T
    kb = (qk[:, QK_M:] * (DK ** -0.5)).astype(MXU_DTYPE)
    vT = v_ref[...].T

    g = s_ref[...] + gb_ref[...]
    logf = jnp.minimum(g, 0.0) - jnp.log(1.0 + jnp.exp(-jnp.abs(g)))
    row = lax.broadcasted_iota(jnp.int32, (L, L), 0)
    col = lax.broadcasted_iota(jnp.int32, (L, L), 1)
    bcum = jnp.dot(jnp.where(row >= col, 1.0, 0.0), logf, precision=lax.Precision.HIGHEST,
                   preferred_element_type=jnp.float32)
    gT = g.T
    bT = bcum.T
    visible = row <= col
    top = lax.broadcasted_iota(jnp.int32, (LANES, L), 0) < DK
    one_row = jnp.where(lax.broadcasted_iota(jnp.int32, (LANES, L), 0) == 0, 1.0, 0.0)

    for h in range(H):
        pair = slice((h // 2) * LANES, (h // 2 + 1) * LANES)
        qp = qT[pair, :]
        qz = jnp.where(top, qp, 0.0) if h % 2 == 0 else jnp.where(top, 0.0, qp)
        qz = qz.astype(MXU_DTYPE)
        qk_s[h] = jnp.dot(kb[:, pair], qz, preferred_element_type=jnp.float32)
        in_s[h] = jnp.dot(cst[h].astype(MXU_DTYPE), qz, preferred_element_type=jnp.float32)

    stats = []
    for h in range(H):
        acol = g[:, S_IM + h:S_IM + h + 1] - bcum[:, S_FM + h:S_FM + h + 1]
        br = bT[S_FM + h:S_FM + h + 1, :]
        lir = gT[S_IM + h:S_IM + h + 1, :]
        m0 = mst[h:h + 1, 0:1]
        d = jnp.where(visible, br + acol, -jnp.inf)
        m_inter = br + m0
        m_t = jnp.maximum(m_inter, jnp.max(d, axis=0, keepdims=True))
        p = qk_s[h] * jnp.exp(d - m_t)
        p_s[h] = p.astype(MXU_DTYPE)
        b_end = br[:, L - 1:L]
        a = b_end - br + lir
        m_new = jnp.maximum(b_end + m0, jnp.max(a, axis=1, keepdims=True))
        vaug = jnp.concatenate([vT[h * DV:(h + 1) * DV, :], one_row], axis=0)
        kw_s[h] = (vaug * jnp.exp(a - m_new)).astype(MXU_DTYPE)
        stats.append((jnp.sum(p, axis=0, keepdims=True), jnp.exp(m_inter - m_t), m_t,
                      jnp.exp(b_end + m0 - m_new), m_new))

    for h in range(H):
        den_intra, sc, m_t, decay, m_new = stats[h]
        pair = slice((h // 2) * LANES, (h // 2 + 1) * LANES)
        inter = in_s[h]
        num = jnp.dot(vT[h * DV:(h + 1) * DV, :].astype(MXU_DTYPE), p_s[h],
                      preferred_element_type=jnp.float32) + sc * inter[:DV, :]
        den = den_intra + sc * inter[DV:DV + 1, :]
        hh = num / jnp.maximum(jnp.abs(den), jnp.exp(-m_t))
        cst[h] = decay * cst[h] + jnp.dot(kw_s[h], kb[:, pair], preferred_element_type=jnp.float32)
        mst[h:h + 1, :] = jnp.broadcast_to(m_new, (1, LANES))
        mu = jnp.mean(hh, axis=0, keepdims=True)
        hc = hh - mu
        var = jnp.mean(hc * hc, axis=0, keepdims=True)
        hn = (hc * lax.rsqrt(var + LN_EPS)).T * ng_ref[:, h * DV:(h + 1) * DV]
        y_ref[:, h * DV:(h + 1) * DV] = (hn * jax.nn.sigmoid(og_ref[:, h * DV:(h + 1) * DV])).astype(y_ref.dtype)


def _mlstm(om, osm, conv_w, conv_b, gate_bias, norm_g, B, S, L):
    nc = S // L
    W = MLSTM_WIDTH
    assert 2 * QK_M == W
    return pl.pallas_call(
        functools.partial(_mlstm_kernel, L=L),
        grid=(B, nc),
        in_specs=[pl.BlockSpec((L, W), lambda b, c: (b * nc + c, 0)),
                  pl.BlockSpec((L, W), lambda b, c: (b * nc + c, 1)),
                  pl.BlockSpec((L, W), lambda b, c: (b * nc + c, 2)),
                  pl.BlockSpec((L, COLS_S), lambda b, c: (b * nc + c, 0)),
                  pl.BlockSpec((CONV_WIDTH, W), lambda b, c: (0, 0)),
                  pl.BlockSpec((1, W), lambda b, c: (0, 0)),
                  pl.BlockSpec((1, COLS_S), lambda b, c: (0, 0)),
                  pl.BlockSpec((1, W), lambda b, c: (0, 0))],
        out_specs=pl.BlockSpec((L, W), lambda b, c: (b * nc + c, 0)),
        out_shape=jax.ShapeDtypeStruct((B * S, W), MXU_DTYPE),
        scratch_shapes=[pltpu.VMEM((L + 2 * SUBLANES, W), jnp.float32),
                        pltpu.VMEM((MLSTM_HEADS, 2 * MLSTM_V_DIM, LANES), jnp.float32),
                        pltpu.VMEM((SUBLANES, LANES), jnp.float32),
                        pltpu.VMEM((MLSTM_HEADS, L, L), jnp.float32),
                        pltpu.VMEM((MLSTM_HEADS, 2 * MLSTM_V_DIM, L), jnp.float32),
                        pltpu.VMEM((MLSTM_HEADS, L, L), MXU_DTYPE),
                        pltpu.VMEM((MLSTM_HEADS, 2 * MLSTM_V_DIM, L), MXU_DTYPE)],
        compiler_params=_cparams(("parallel", "arbitrary")),
    )(om, om, om, osm, conv_w, conv_b, gate_bias, norm_g)


def _layer_norm(z, g, b):
    mu = jnp.mean(z, axis=1, keepdims=True)
    zc = z - mu
    var = jnp.mean(zc * zc, axis=1, keepdims=True)
    return zc * lax.rsqrt(var + LN_EPS) * g + b


def _merge_kernel(ya_ref, ym_ref, g_ref, x_ref, wa_ref, wm_ref, wo_ref, lg_ref, lb_ref,
                  wr_ref, br_ref, x1_ref, xp_ref, idx_ref, gate_ref, cnt_ref, carry_ref, *, d_model):
    D = d_model
    mix = (jax.nn.sigmoid(g_ref[:, :D]) * jnp.dot(ya_ref[...], wa_ref[...], preferred_element_type=jnp.float32)
           + jax.nn.sigmoid(g_ref[:, D:]) * jnp.dot(ym_ref[...], wm_ref[...], preferred_element_type=jnp.float32))
    y = jnp.dot(mix.astype(MXU_DTYPE), wo_ref[...], preferred_element_type=jnp.float32)
    x1 = _layer_norm(DEEPNORM_ALPHA * x_ref[...] + y, lg_ref[...], lb_ref[...])
    x1_ref[...] = x1
    xb = x1.astype(MXU_DTYPE)
    bits = pltpu.bitcast(x1.astype(jnp.bfloat16).astype(jnp.float32), jnp.uint32)
    xp_ref[...] = (bits[:, :D // 2] & jnp.uint32(0xFFFF0000)) | (bits[:, D // 2:] >> 16)

    logits = jnp.dot(xb, wr_ref[...], preferred_element_type=jnp.float32) + br_ref[...]
    tm = logits.shape[0]
    lane = lax.broadcasted_iota(jnp.int32, (tm, LANES), 1)
    lane_f = lane.astype(jnp.float32)
    vals, idxs = [], []
    for _ in range(TOP_K):
        mx = jnp.max(logits, axis=1, keepdims=True)
        ix = jnp.min(jnp.where(logits == mx, lane_f, float(LANES)), axis=1, keepdims=True)
        vals.append(mx)
        idxs.append(ix)
        logits = jnp.where(lane_f == ix, -jnp.inf, logits)
    es = [jnp.exp(v - vals[0]) for v in vals]
    tot = es[0]
    for e in es[1:]:
        tot = tot + e
    @pl.when(pl.program_id(0) == 0)
    def _():
        carry_ref[...] = jnp.zeros(carry_ref.shape, jnp.float32)

    hots = [lane_f == ix for ix in idxs]
    c = jnp.zeros((tm, LANES), jnp.float32)
    for hot in hots:
        c = c + jnp.where(hot, 1.0, 0.0)
    before = jnp.where(lax.broadcasted_iota(jnp.int32, (tm, tm), 0)
                       > lax.broadcasted_iota(jnp.int32, (tm, tm), 1), 1.0, 0.0).astype(MXU_DTYPE)
    carry = carry_ref[0:1, :]
    prior = jnp.dot(before, c.astype(MXU_DTYPE), preferred_element_type=jnp.float32) + carry
    total = carry + jnp.sum(c, axis=0, keepdims=True)
    carry_ref[...] = jnp.broadcast_to(total, carry_ref.shape)
    cnt_ref[...] = jnp.broadcast_to(total, cnt_ref.shape)

    idx_out = jnp.zeros((tm, LANES), jnp.float32)
    gate_out = jnp.zeros((tm, LANES), jnp.float32)
    for k in range(TOP_K):
        idx_out = jnp.where(lane == k, idxs[k], idx_out)
        rank_k = jnp.sum(jnp.where(hots[k], prior, 0.0), axis=1, keepdims=True)
        idx_out = jnp.where(lane == TOP_K + k, rank_k, idx_out)
        gate_out = jnp.where(lane == k, es[k] / tot, gate_out)
    idx_ref[...] = idx_out.T[:2 * TOP_K, :].astype(jnp.int32)
    gate_ref[...] = gate_out


def _merge(y_att, y_m, og, x2, wa, wm, wo, ln_g, ln_b, wr, br, tm):
    T, D = x2.shape
    full = lambda shape: pl.BlockSpec(shape, lambda i: (0, 0))
    return pl.pallas_call(
        functools.partial(_merge_kernel, d_model=D),
        grid=(T // tm,),
        in_specs=[pl.BlockSpec((tm, ATT_WIDTH), lambda i: (i, 0)),
                  pl.BlockSpec((tm, MLSTM_WIDTH), lambda i: (i, 0)),
                  pl.BlockSpec((tm, 2 * D), lambda i: (i, 0)),
                  pl.BlockSpec((tm, D), lambda i: (i, 0)),
                  full(wa.shape), full(wm.shape), full(wo.shape),
                  full((1, D)), full((1, D)), full(wr.shape), full((1, LANES))],
        out_specs=[pl.BlockSpec((tm, D), lambda i: (i, 0)),
                   pl.BlockSpec((tm, D // 2), lambda i: (i, 0)),
                   pl.BlockSpec((2 * TOP_K, tm), lambda i: (0, i)),
                   pl.BlockSpec((tm, LANES), lambda i: (i, 0)),
                   pl.BlockSpec((SUBLANES, LANES), lambda i: (0, 0))],
        out_shape=[jax.ShapeDtypeStruct((T, D), jnp.float32),
                   jax.ShapeDtypeStruct((T, D // 2), jnp.uint32),
                   jax.ShapeDtypeStruct((2 * TOP_K, T), jnp.int32),
                   jax.ShapeDtypeStruct((T, LANES), jnp.float32),
                   jax.ShapeDtypeStruct((SUBLANES, LANES), jnp.float32)],
        scratch_shapes=[pltpu.VMEM((SUBLANES, LANES), jnp.float32)],
        compiler_params=_cparams(("arbitrary",)),
    )(y_att, y_m, og, x2, wa, wm, wo, ln_g, ln_b, wr, br)


def _dispatch_sc(dest, xp, cap):
    T, W = xp.shape
    n_chunks = T // SC_CHUNK
    n_workers = SC_CORES * SC_SUBCORES
    assert T % SC_CHUNK == 0 and n_chunks % n_workers == 0
    per_worker = n_chunks // n_workers
    idx = dest.reshape(TOP_K, n_chunks, SC_CHUNK).transpose(1, 0, 2)
    mesh = plsc.VectorSubcoreMesh(core_axis_name="c", subcore_axis_name="s")

    @functools.partial(
        pl.kernel, mesh=mesh,
        out_type=jax.ShapeDtypeStruct((cap, W), xp.dtype),
        scratch_types=[pltpu.VMEM((SC_CHUNK, W), xp.dtype),
                       pltpu.VMEM((TOP_K, SC_CHUNK), jnp.int32)])
    def scatter_rows(x_hbm, idx_hbm, xs_hbm, rows_v, idx_v):
        worker = lax.axis_index("s") * SC_CORES + lax.axis_index("c")

        def body(j, carry):
            c = worker * per_worker + j
            pltpu.sync_copy(x_hbm.at[pl.ds(c * SC_CHUNK, SC_CHUNK)], rows_v)
            pltpu.sync_copy(idx_hbm.at[c], idx_v)
            for k in range(TOP_K):
                pltpu.sync_copy(rows_v, xs_hbm.at[idx_v.at[k]])
            return carry

        lax.fori_loop(0, per_worker, body, 0)

    return scatter_rows(xp, idx)


def _ffn_kernel(be_ref, nu_ref, nv_ref, xs_ref, wgu_ref, bgu_ref, wd_ref, bd_ref, y_ref, wgu_b,
                wd_b, act_s, *, d_ff):
    r = pl.program_id(0)
    e = be_ref[r]
    prev = be_ref[jnp.maximum(r - 1, 0)]

    @pl.when((r == 0) | (e != prev))
    def _():
        wgu_b[...] = wgu_ref[0].astype(MXU_DTYPE)
        wd_b[...] = wd_ref[0].astype(MXU_DTYPE)

    @pl.when(r < nu_ref[0])
    def _():
        live = lax.broadcasted_iota(jnp.int32, (xs_ref.shape[0], 1), 0) < nv_ref[r]
        w = jnp.where(live, xs_ref[...], jnp.uint32(0))
        half = w.shape[1]
        x_hi = pltpu.bitcast(w & jnp.uint32(0xFFFF0000), jnp.float32).astype(MXU_DTYPE)
        x_lo = pltpu.bitcast(w << 16, jnp.float32).astype(MXU_DTYPE)
        x = jnp.concatenate([x_hi, x_lo], axis=1)
        step = 512
        for j in range(0, d_ff, step):
            def gu(lo):
                return (jnp.dot(x, wgu_b[:, lo:lo + step], preferred_element_type=jnp.float32)
                        + bgu_ref[0, :, lo:lo + step])
            gate = jnp.minimum(gu(j), SWIGLU_LIMIT)
            up = jnp.clip(gu(d_ff + j), -SWIGLU_LIMIT, SWIGLU_LIMIT)
            act = (up + 1.0) * (gate * jax.nn.sigmoid(SWIGLU_ALPHA * gate))
            act_s[:, j:j + step] = act.astype(MXU_DTYPE)
        y = jnp.dot(act_s[...], wd_b[...], preferred_element_type=jnp.float32) + bd_ref[0]
        bits = pltpu.bitcast(y.astype(jnp.bfloat16).astype(jnp.float32), jnp.uint32)
        y_ref[...] = (bits[:, :half] & jnp.uint32(0xFFFF0000)) | (bits[:, half:] >> 16)

    @pl.when(r >= nu_ref[0])
    def _():
        y_ref[...] = jnp.zeros(y_ref.shape, y_ref.dtype)


def _expert_ffn(block_expert, n_used, n_valid, xs, w_gate_up, b_gate_up, w_down, b_down, bm):
    cap, half = xs.shape
    E, D, F2 = w_gate_up.shape
    d_ff = F2 // 2
    grid_spec = pltpu.PrefetchScalarGridSpec(
        num_scalar_prefetch=3,
        grid=(cap // bm,),
        in_specs=[pl.BlockSpec((bm, half), lambda r, be, nu, nv: (jnp.minimum(r, nu[0] - 1), 0)),
                  pl.BlockSpec((1, D, F2), lambda r, be, nu, nv: (be[r], 0, 0)),
                  pl.BlockSpec((1, 1, F2), lambda r, be, nu, nv: (be[r], 0, 0)),
                  pl.BlockSpec((1, d_ff, D), lambda r, be, nu, nv: (be[r], 0, 0)),
                  pl.BlockSpec((1, 1, D), lambda r, be, nu, nv: (be[r], 0, 0))],
        out_specs=pl.BlockSpec((bm, half), lambda r, be, nu, nv: (r, 0)),
        scratch_shapes=[pltpu.VMEM((D, F2), MXU_DTYPE),
                        pltpu.VMEM((d_ff, D), MXU_DTYPE),
                        pltpu.VMEM((bm, d_ff), MXU_DTYPE)],
    )
    return pl.pallas_call(
        functools.partial(_ffn_kernel, d_ff=d_ff),
        grid_spec=grid_spec,
        out_shape=jax.ShapeDtypeStruct((cap, half), jnp.uint32),
        compiler_params=_cparams(("arbitrary",)),
    )(block_expert, n_used, n_valid, xs, w_gate_up, b_gate_up.reshape(E, 1, F2), w_down,
      b_down.reshape(E, 1, D))


def _gather_sc(dest, ybuf):
    cap, D = ybuf.shape
    T = dest.shape[1]
    chunk = SC_CHUNK
    n_chunks = T // chunk
    n_workers = SC_CORES * SC_SUBCORES
    assert T % chunk == 0 and n_chunks % n_workers == 0
    per_worker = n_chunks // n_workers
    idx = dest.reshape(TOP_K, n_chunks, chunk).transpose(1, 0, 2)
    mesh = plsc.VectorSubcoreMesh(core_axis_name="c", subcore_axis_name="s")

    @functools.partial(
        pl.kernel, mesh=mesh,
        out_type=jax.ShapeDtypeStruct((TOP_K, T, D), ybuf.dtype),
        scratch_types=[pltpu.VMEM((chunk, D), ybuf.dtype),
                       pltpu.VMEM((TOP_K, chunk), jnp.int32)])
    def gather_rows(y_hbm, idx_hbm, out_hbm, rows_v, idx_v):
        worker = lax.axis_index("s") * SC_CORES + lax.axis_index("c")

        def body(j, carry):
            c = worker * per_worker + j
            pltpu.sync_copy(idx_hbm.at[c], idx_v)
            for k in range(TOP_K):
                pltpu.sync_copy(y_hbm.at[idx_v.at[k]], rows_v)
                pltpu.sync_copy(rows_v, out_hbm.at[k, pl.ds(c * chunk, chunk)])
            return carry

        lax.fori_loop(0, per_worker, body, 0)

    return gather_rows(ybuf, idx)


def _combine_dense_kernel(yk_ref, gate_ref, x1_ref, lg_ref, lb_ref, *rest):
    o_ref = rest[-1]
    left = right = None
    for k in range(TOP_K):
        w = yk_ref[k]
        g = gate_ref[:, k:k + 1]
        hi = g * pltpu.bitcast(w & jnp.uint32(0xFFFF0000), jnp.float32)
        lo = g * pltpu.bitcast(w << 16, jnp.float32)
        left = hi if left is None else left + hi
        right = lo if right is None else right + lo
    y = jnp.concatenate([left, right], axis=1)
    o_ref[...] = _layer_norm(DEEPNORM_ALPHA * x1_ref[...] + y, lg_ref[...], lb_ref[...])


def _combine_dense(yk, first_tile, gates, x1, ln_g, ln_b, prev, tm):
    T, D = x1.shape
    n = yk.shape[1] // tm
    in_specs = [pl.BlockSpec((TOP_K, tm, D // 2), lambda i: (0, i, 0)),
                pl.BlockSpec((tm, LANES), lambda i: (i + first_tile, 0)),
                pl.BlockSpec((tm, D), lambda i: (i + first_tile, 0)),
                pl.BlockSpec((1, D), lambda i: (0, 0)),
                pl.BlockSpec((1, D), lambda i: (0, 0))]
    args = [yk, gates, x1, ln_g, ln_b]
    aliases = {}
    if prev is not None:
        in_specs.append(pl.BlockSpec(memory_space=pl.ANY))
        args.append(prev)
        aliases = {len(args) - 1: 0}
    return pl.pallas_call(
        _combine_dense_kernel,
        grid=(n,),
        in_specs=in_specs,
        out_specs=pl.BlockSpec((tm, D), lambda i: (i + first_tile, 0)),
        out_shape=jax.ShapeDtypeStruct((T, D), jnp.float32),
        input_output_aliases=aliases,
        compiler_params=_cparams(("parallel",)),
    )(*args)


def _tile(n, pref):
    t = min(n, pref)
    assert n % t == 0
    return t


def _relayout_w_in(w_in, d_model):
    sizes = (ATT_WIDTH, ATT_WIDTH, ATT_WIDTH, IDX_HEADS * IDX_HEAD_DIM, IDX_HEAD_DIM, IDX_HEADS,
             QK_M, QK_M, MLSTM_WIDTH, MLSTM_HEADS, MLSTM_HEADS, MLSTM_WIDTH, d_model, d_model)
    offs = [0]
    for s in sizes:
        offs.append(offs[-1] + s)
    seg = lambda k: w_in[:, offs[k]:offs[k + 1]]
    (q_a, k_a, v_a, q_i, k_i, w_i, q_m, k_m, v_m, i_m, f_m, o_m, g_a, g_m) = [seg(k) for k in range(14)]
    pad = jnp.zeros((w_in.shape[0], COLS_S - (IDX_HEAD_DIM + IDX_HEADS + 2 * MLSTM_HEADS)), w_in.dtype)
    cols = [q_a, k_a, q_i, k_i, w_i, i_m, f_m, pad, q_m, k_m, v_m, o_m, g_a, g_m, k_i, k_i, v_a]
    return jnp.concatenate(cols, axis=1).astype(MXU_DTYPE)


def _layer(x2, B, S, w_in, conv_w, conv_b, i_bias, f_bias, norm_g, w_branch_attn, w_branch_mlstm,
           w_out, ln1_g, ln1_b, w_router, b_router, w_gate_up, b_gate_up, w_down, b_down,
           ln2_g, ln2_b, rel_bias):
    T, D = x2.shape
    bf = MXU_DTYPE
    tq = _tile(S, 256)
    L = _tile(S, 256)
    tm = _tile(T, 256)

    oa, osm, om, og, okk, v_t = _project(x2, _relayout_w_in(w_in, D), _tile(T, 512), tq)
    bias_t = _bias_tiles(rel_bias, tq)
    y_att = _dsa_attention(oa, osm, okk, v_t, bias_t, B, S, tq)

    gate_bias = jnp.zeros((1, COLS_S), jnp.float32)
    gate_bias = gate_bias.at[0, S_IM:S_IM + MLSTM_HEADS].set(i_bias)
    gate_bias = gate_bias.at[0, S_FM:S_FM + MLSTM_HEADS].set(f_bias)
    y_m = _mlstm(om, osm, conv_w, conv_b.reshape(1, -1), gate_bias, norm_g.reshape(1, -1), B, S, L)

    wr = jnp.zeros((D, LANES), bf).at[:, :N_EXPERTS].set(w_router.astype(bf))
    br = jnp.full((1, LANES), NEG_BIG, jnp.float32).at[0, :N_EXPERTS].set(b_router)
    x1, xp, idx, gates, cnt = _merge(y_att, y_m, og, x2, w_branch_attn.astype(bf),
                                     w_branch_mlstm.astype(bf), w_out.astype(bf), ln1_g.reshape(1, D),
                                     ln1_b.reshape(1, D), wr, br, _tile(T, 512))

    bm = 1024
    counts = cnt[0, :N_EXPERTS].astype(jnp.int32)
    padded = ((counts + bm - 1) // bm) * bm
    pend = jnp.cumsum(padded)
    pstart = pend - padded
    cap = ((T * TOP_K + bm - 1) // bm) * bm + N_EXPERTS * bm
    n_blocks = cap // bm
    experts = jnp.arange(N_EXPERTS, dtype=jnp.int32)[:, None, None]
    dest = jnp.sum(jnp.where(idx[None, :TOP_K, :] == experts, pstart[:, None, None], 0), axis=0) \
        + idx[TOP_K:, :]
    block_row = jnp.arange(n_blocks, dtype=jnp.int32) * bm
    block_expert = jnp.minimum(jnp.sum((pend[None, :] <= block_row[:, None]).astype(jnp.int32), axis=1),
                               N_EXPERTS - 1)
    n_used = (pend[-1:] // bm).astype(jnp.int32)
    n_valid = jnp.clip((pstart + counts)[block_expert] - block_row, 0, bm).astype(jnp.int32)

    xs = _dispatch_sc(dest, xp, cap)
    ybuf = _expert_ffn(block_expert, n_used, n_valid, xs, w_gate_up, b_gate_up, w_down, b_down, bm)
    quantum = SC_CORES * SC_SUBCORES * SC_CHUNK
    parts = (1, 1, 2, 4) if T % (8 * quantum) == 0 else (1,)
    bounds = [0]
    for p in parts:
        bounds.append(bounds[-1] + p * T // sum(parts))
    slabs = [_gather_sc(dest[:, lo:hi], ybuf) for lo, hi in zip(bounds[:-1], bounds[1:])]
    out = None
    tc = _tile(T // sum(parts), 1024)
    for lo, yk in zip(bounds[:-1], slabs):
        out = _combine_dense(yk, lo // tc, gates, x1, ln2_g.reshape(1, D), ln2_b.reshape(1, D),
                             out, tc)
    return out


def kernel(x, w_in, conv_w, conv_b, mlstm_i_bias, mlstm_f_bias, mlstm_norm_g, w_branch_attn,
           w_branch_mlstm, w_out, ln1_g, ln1_b, w_router, b_router, w_gate_up, b_gate_up,
           w_down, b_down, ln2_g, ln2_b, rel_bias):
    B, S, D = x.shape
    x2 = x.reshape(B * S, D)
    for l in range(w_in.shape[0]):
        x2 = _layer(x2, B, S, w_in[l], conv_w[l], conv_b[l], mlstm_i_bias[l], mlstm_f_bias[l],
                    mlstm_norm_g[l], w_branch_attn[l], w_branch_mlstm[l], w_out[l], ln1_g[l], ln1_b[l],
                    w_router[l], b_router[l], w_gate_up[l], b_gate_up[l], w_down[l], b_down[l],
                    ln2_g[l], ln2_b[l], rel_bias)
    return x2.reshape(B, S, D)
```

```python
import functools
import math

import jax
import jax.numpy as jnp
from jax import lax
from jax.experimental import pallas as pl
from jax.experimental.pallas import tpu as pltpu
from jax.experimental.pallas import tpu_sc as plsc

ATT_HEADS = 8
ATT_HEAD_DIM = 64
ATT_WIDTH = ATT_HEADS * ATT_HEAD_DIM
IDX_HEADS = 8
IDX_HEAD_DIM = 64
TOPK_MAX = 256
MLSTM_HEADS = 4
MLSTM_QK_DIM = 64
MLSTM_V_DIM = 128
MLSTM_WIDTH = MLSTM_HEADS * MLSTM_V_DIM
CONV_WIDTH = 4
N_BUCKETS = 32
MAX_DISTANCE = 128
N_EXPERTS = 32
TOP_K = 4
SWIGLU_ALPHA = 1.702
SWIGLU_LIMIT = 7.0
LN_EPS = 1e-5
DEPTH = 1
DEEPNORM_ALPHA = (2 * DEPTH) ** 0.25

LANES = 128
SUBLANES = 8
VMEM_LIMIT_BYTES = 56 * 1024 * 1024
SC_CORES = 2
SC_SUBCORES = 16
SC_CHUNK = 128

MXU_DTYPE = jnp.bfloat16

INT_MIN = -(2 ** 31)
NEG_BIG = -1e30

QK_M = MLSTM_HEADS * MLSTM_QK_DIM
COLS_A = 2 * ATT_WIDTH + IDX_HEADS * IDX_HEAD_DIM
COLS_S = LANES
COLS_M = 2 * QK_M + 2 * MLSTM_WIDTH
COLS_K = 2 * IDX_HEAD_DIM
S_WI = IDX_HEAD_DIM
S_IM = S_WI + IDX_HEADS
S_FM = S_IM + MLSTM_HEADS


def _cparams(sem):
    return pltpu.CompilerParams(dimension_semantics=sem, vmem_limit_bytes=VMEM_LIMIT_BYTES)


def _proj_kernel(x_ref, w_ref, oa_ref, os_ref, om_ref, og_ref, ok_ref, ovt_ref, *, d_model):
    xb = x_ref[...].astype(MXU_DTYPE)
    step = 512

    def mm(lo, hi):
        return jnp.dot(xb, w_ref[:, lo:hi], preferred_element_type=jnp.float32)

    base = 0
    for j in range(0, COLS_A, step):
        oa_ref[:, j:j + step] = mm(base + j, base + j + step).astype(MXU_DTYPE)
    base += COLS_A
    os_ref[...] = mm(base, base + COLS_S)
    base += COLS_S
    for j in range(0, COLS_M, step):
        om_ref[:, j:j + step] = mm(base + j, base + j + step)
    base += COLS_M
    for j in range(0, 2 * d_model, step):
        og_ref[:, j:j + step] = mm(base + j, base + j + step)
    base += 2 * d_model
    ok_ref[...] = mm(base, base + COLS_K).astype(MXU_DTYPE)
    base += COLS_K
    v = mm(base, base + ATT_WIDTH)
    tq = ovt_ref.shape[2]
    for s in range(ovt_ref.shape[0]):
        ovt_ref[s] = v[s * tq:(s + 1) * tq, :].T.astype(MXU_DTYPE)


def _project(x2, w_p, tm, tq):
    T, D = x2.shape
    n_all = w_p.shape[1]
    assert tm % tq == 0
    return pl.pallas_call(
        functools.partial(_proj_kernel, d_model=D),
        grid=(T // tm,),
        in_specs=[pl.BlockSpec((tm, D), lambda i: (i, 0)),
                  pl.BlockSpec((D, n_all), lambda i: (0, 0))],
        out_specs=[pl.BlockSpec((tm, COLS_A), lambda i: (i, 0)),
                   pl.BlockSpec((tm, COLS_S), lambda i: (i, 0)),
                   pl.BlockSpec((tm, COLS_M), lambda i: (i, 0)),
                   pl.BlockSpec((tm, 2 * D), lambda i: (i, 0)),
                   pl.BlockSpec((tm, COLS_K), lambda i: (i, 0)),
                   pl.BlockSpec((tm // tq, ATT_WIDTH, tq), lambda i: (i, 0, 0))],
        out_shape=[jax.ShapeDtypeStruct((T, COLS_A), MXU_DTYPE),
                   jax.ShapeDtypeStruct((T, COLS_S), jnp.float32),
                   jax.ShapeDtypeStruct((T, COLS_M), jnp.float32),
                   jax.ShapeDtypeStruct((T, 2 * D), jnp.float32),
                   jax.ShapeDtypeStruct((T, COLS_K), MXU_DTYPE),
                   jax.ShapeDtypeStruct((T // tq, ATT_WIDTH, tq), MXU_DTYPE)],
        compiler_params=_cparams(("parallel",)),
    )(x2, w_p)


def _bias_kernel(rb_ref, o_ref, *, tq):
    h = pl.program_id(0)
    s = lax.broadcasted_iota(jnp.int32, (tq, tq), 0)
    t = lax.broadcasted_iota(jnp.int32, (tq, tq), 1)
    max_exact = N_BUCKETS // 2
    for d in range(3):
        n = jnp.maximum(t - s + d * tq, 0)
        n_f = jnp.maximum(n, 1).astype(jnp.float32)
        large = max_exact + (jnp.log(n_f / max_exact) / math.log(MAX_DISTANCE / max_exact)
                             * (N_BUCKETS - max_exact)).astype(jnp.int32)
        large = jnp.minimum(large, N_BUCKETS - 1)
        bucket = jnp.where(n < max_exact, n, large)
        acc = jnp.zeros((tq, tq), jnp.float32)
        for k in range(N_BUCKETS):
            acc = jnp.where(bucket == k, rb_ref[h, k], acc)
        o_ref[0, d] = acc


def _bias_tiles(rel_bias, tq):
    assert tq + 1 >= MAX_DISTANCE
    H = rel_bias.shape[0]
    return pl.pallas_call(
        functools.partial(_bias_kernel, tq=tq),
        grid=(H,),
        in_specs=[pl.BlockSpec(memory_space=pltpu.SMEM)],
        out_specs=pl.BlockSpec((1, 3, tq, tq), lambda h: (h, 0, 0, 0)),
        out_shape=jax.ShapeDtypeStruct((H, 3, tq, tq), jnp.float32),
        compiler_params=_cparams(("parallel",)),
    )(rel_bias)


def _dsa_kernel(qa_ref, ka_ref, vt_ref, qi_ref, kk_ref, wq_ref, bias_ref, o_ref,
                keys_ref, hi_ref, lo_ref, msk_ref, qz_ref, qiz_ref, s_ref, sb_ref, p_ref, acc_ref, m_ref,
                l_ref,
                *, tq, n_sel):
    i = pl.program_id(1)
    nch = i + 1
    t_pos = i * tq + lax.broadcasted_iota(jnp.int32, (1, tq), 1)
    s_loc = lax.broadcasted_iota(jnp.int32, (tq, 1), 0)
    hd = ATT_HEAD_DIM
    n_pairs = ATT_HEADS // 2
    assert 2 * hd == LANES and IDX_HEAD_DIM == hd and IDX_HEADS == ATT_HEADS

    top = lax.broadcasted_iota(jnp.int32, (LANES, tq), 0) < hd

    def pair_operand(blk):
        bt = blk.astype(jnp.float32).T
        return jnp.concatenate([jnp.where(top, bt, 0.0), jnp.where(top, 0.0, bt)],
                               axis=1).astype(MXU_DTYPE)

    for p in range(n_pairs):
        pair = slice(p * LANES, (p + 1) * LANES)
        qz_ref[p] = pair_operand(qa_ref[:, pair] * (hd ** -0.5))
        qiz_ref[p] = pair_operand(qi_ref[:, pair])

    ws = wq_ref[...].T[S_WI:S_WI + IDX_HEADS, :] * (IDX_HEADS ** -0.5)

    def score_tile(c):
        off = pl.multiple_of(c * tq, tq)
        kk = kk_ref[pl.ds(off, tq), :]
        sc = jnp.zeros((tq, tq), jnp.float32)
        for p in range(n_pairs):
            d = jnp.dot(kk, qiz_ref[p], preferred_element_type=jnp.float32)
            for j in range(2):
                h = 2 * p + j
                sc = sc + ws[h:h + 1, :] * jnp.maximum(d[:, j * tq:(j + 1) * tq], 0.0)
        sc = sc + 0.0
        b = pltpu.bitcast(sc, jnp.int32)
        sk = b ^ ((b >> 31) & jnp.int32(0x7FFFFFFF))
        sk = jnp.where(off + s_loc <= t_pos, sk, jnp.int32(INT_MIN))
        keys_ref[c] = sk
        hi_ref[c] = (sk >> 16).astype(jnp.int16)

    def score_pair(c2, carry):
        score_tile(2 * c2)
        score_tile(2 * c2 + 1)
        return carry

    def score_single(c, carry):
        score_tile(c)
        return carry

    lax.fori_loop(0, nch // 2, score_pair, 0)
    lax.fori_loop(2 * (nch // 2), nch, score_single, 0)

    pack = 2 * SUBLANES
    half_min = -(2 ** 15)

    def count16(ref, pred_fn):
        def tile_count(c):
            hit = jnp.where(pred_fn(ref[c]), jnp.int16(1), jnp.int16(0))
            parts = [hit[r * pack:(r + 1) * pack, :] for r in range(tq // pack)]
            while len(parts) > 1:
                parts = [a + b for a, b in zip(parts[::2], parts[1::2])]
            return parts[0]
        acc = lax.fori_loop(0, nch // 2,
                            lambda c2, a: a + (tile_count(2 * c2) + tile_count(2 * c2 + 1)),
                            jnp.zeros((pack, tq), jnp.int16))
        acc = lax.fori_loop(2 * (nch // 2), nch, lambda c, a: a + tile_count(c), acc)
        return jnp.sum(acc.astype(jnp.float32), axis=0, keepdims=True)

    def search16(ref, target):
        def bit_body(it, u):
            cand_u = u | lax.shift_left(jnp.int32(1), 15 - it)
            cand = (cand_u + half_min).astype(jnp.int16)
            return jnp.where(count16(ref, lambda x: x >= cand) >= target, cand_u, u)
        return lax.fori_loop(0, 16, bit_body, jnp.zeros((1, tq), jnp.int32))

    u_hi = search16(hi_ref, float(n_sel))
    thr_hi = (u_hi + half_min).astype(jnp.int16)
    need_lo = n_sel - count16(hi_ref, lambda x: x > thr_hi)

    def low_body(c, carry):
        lo = ((keys_ref[c] & 0xFFFF) + half_min).astype(jnp.int16)
        lo_ref[c] = jnp.where(hi_ref[c] == thr_hi, lo, jnp.int16(half_min))
        return carry

    lax.fori_loop(0, nch, low_body, 0)
    u_lo = search16(lo_ref, need_lo)
    thr_lo = (u_lo + half_min).astype(jnp.int16)
    thr = lax.shift_left(u_hi + half_min, 16) | u_lo
    need = jnp.where(thr == INT_MIN, 0.0, need_lo - count16(lo_ref, lambda x: x > thr_lo))
    n_eq = count16(lo_ref, lambda x: x == thr_lo)
    surplus = jnp.where((thr != INT_MIN) & (n_eq > need), 1.0, 0.0)
    has_surplus = jnp.max(surplus) > 0.0

    @pl.when(has_surplus)
    def _():
        tri = jnp.where(lax.broadcasted_iota(jnp.int32, (tq, tq), 0)
                        >= lax.broadcasted_iota(jnp.int32, (tq, tq), 1), 1.0, 0.0).astype(MXU_DTYPE)

        def mask_body(c, run):
            kc = keys_ref[c]
            eq = kc == thr
            eqf = jnp.where(eq, 1.0, 0.0)
            prefix = jnp.dot(tri, eqf.astype(MXU_DTYPE), preferred_element_type=jnp.float32) + run
            msk_ref[c] = jnp.where(kc > thr, 0.0,
                                   jnp.where(eq, jnp.where(prefix <= need, 0.0, NEG_BIG), NEG_BIG))
            return run + jnp.sum(eqf, axis=0, keepdims=True)

        lax.fori_loop(0, nch, mask_body, jnp.zeros((1, tq), jnp.float32))

    @pl.when(jnp.logical_not(has_surplus))
    def _():
        floor = jnp.maximum(thr, INT_MIN + 1)

        def mask_body(c, carry):
            msk_ref[c] = jnp.where(keys_ref[c] >= floor, 0.0, NEG_BIG)
            return carry

        lax.fori_loop(0, nch, mask_body, 0)

    m_ref[...] = jnp.full(m_ref.shape, NEG_BIG, jnp.float32)
    l_ref[...] = jnp.zeros(l_ref.shape, jnp.float32)
    acc_ref[...] = jnp.zeros(acc_ref.shape, jnp.float32)

    def logit_stage(cs, far, sr):
        tile_max = [None] * ATT_HEADS
        for n, c in enumerate(cs):
            off = pl.multiple_of(c * tq, tq)
            mk = msk_ref[c]
            for p in range(n_pairs):
                pair = slice(p * LANES, (p + 1) * LANES)
                s2 = jnp.dot(ka_ref[pl.ds(off, tq), pair], qz_ref[p],
                             preferred_element_type=jnp.float32)
                for j in range(2):
                    h = 2 * p + j
                    s = s2[:, j * tq:(j + 1) * tq] + mk
                    if not far:
                        s = s + bias_ref[h, i - c]
                    sr[n, h] = s
                    mx = jnp.max(s, axis=0, keepdims=True)
                    tile_max[h] = mx if tile_max[h] is None else jnp.maximum(tile_max[h], mx)
        return tile_max

    def value_stage(cs, far, sr, tile_max):
        alphas = []
        for h in range(ATT_HEADS):
            m_prev = m_ref[h:h + 1, :]
            if far:
                b_far = bias_ref[h, 2, 0:1, 0:1]
                m_new = jnp.maximum(m_prev, tile_max[h] + b_far)
                shift = m_new - b_far
            else:
                m_new = jnp.maximum(m_prev, tile_max[h])
                shift = m_new
            alpha = jnp.exp(m_prev - m_new)
            for n in range(len(cs)):
                pe = jnp.exp(sr[n, h] - shift)
                p_ref[n, h // 2, :, (h % 2) * tq:(h % 2 + 1) * tq] = pe.astype(MXU_DTYPE)
            m_ref[h:h + 1, :] = m_new
            alphas.append(alpha)
        ones_rows = jnp.ones((2 * SUBLANES, tq), MXU_DTYPE)
        for h in range(ATT_HEADS):
            pv = None
            for n, c in enumerate(cs):
                va = jnp.concatenate([vt_ref[c, h * hd:(h + 1) * hd, :], ones_rows], axis=0)
                d = jnp.dot(va, p_ref[n, h // 2, :, (h % 2) * tq:(h % 2 + 1) * tq],
                            preferred_element_type=jnp.float32)
                pv = d if pv is None else pv + d
            acc_ref[h] = alphas[h] * acc_ref[h] + pv[:hd]
            l_ref[h:h + 1, :] = alphas[h] * l_ref[h:h + 1, :] + pv[hd:hd + 1]

    def att_step(cs, far):
        value_stage(cs, far, s_ref, logit_stage(cs, far, s_ref))

    def far_single(c, carry):
        att_step([c], True)
        return carry

    n_far = jnp.maximum(i - 1, 0)
    n_fp = n_far // 2

    tiles = lambda q: [2 * q, 2 * q + 1]

    @pl.when(n_fp > 0)
    def _():
        def two_pairs(m, max_a):
            max_b = logit_stage(tiles(2 * m + 1), True, sb_ref)
            value_stage(tiles(2 * m), True, s_ref, max_a)
            nxt_a = logit_stage(tiles(2 * m + 2), True, s_ref)
            value_stage(tiles(2 * m + 1), True, sb_ref, max_b)
            return nxt_a

        n_dbl = (n_fp - 1) // 2
        max_a = lax.fori_loop(0, n_dbl, two_pairs, logit_stage(tiles(0), True, s_ref))
        q = 2 * n_dbl

        @pl.when(n_fp - q == 2)
        def _():
            max_b = logit_stage(tiles(q + 1), True, sb_ref)
            value_stage(tiles(q), True, s_ref, max_a)
            value_stage(tiles(q + 1), True, sb_ref, max_b)

        @pl.when(n_fp - q == 1)
        def _():
            value_stage(tiles(q), True, s_ref, max_a)

    lax.fori_loop(2 * n_fp, n_far, far_single, 0)

    @pl.when(i > 0)
    def _():
        att_step([i - 1, i], False)

    @pl.when(i == 0)
    def _():
        att_step([i], False)
    y_t = jnp.concatenate([acc_ref[h] / l_ref[h:h + 1, :] for h in range(ATT_HEADS)], axis=0)
    o_ref[...] = y_t.T.astype(o_ref.dtype)


def _dsa_attention(oa, osm, okk, v_t, bias_t, B, S, tq):
    nq = S // tq
    n_sel = min(TOPK_MAX, S // 4)
    W = ATT_WIDTH
    assert v_t.shape == (B * nq, W, tq)
    return pl.pallas_call(
        functools.partial(_dsa_kernel, tq=tq, n_sel=n_sel),
        grid=(B, nq),
        in_specs=[pl.BlockSpec((tq, W), lambda b, i: (b * nq + i, 0)),
                  pl.BlockSpec((S, W), lambda b, i: (b, 1)),
                  pl.BlockSpec((nq, W, tq), lambda b, i: (b, 0, 0)),
                  pl.BlockSpec((tq, W), lambda b, i: (b * nq + i, 2)),
                  pl.BlockSpec((S, COLS_K), lambda b, i: (b, 0)),
                  pl.BlockSpec((tq, COLS_S), lambda b, i: (b * nq + i, 0)),
                  pl.BlockSpec((ATT_HEADS, 3, tq, tq), lambda b, i: (0, 0, 0, 0))],
        out_specs=pl.BlockSpec((tq, W), lambda b, i: (b * nq + i, 0)),
        out_shape=jax.ShapeDtypeStruct((B * S, W), MXU_DTYPE),
        scratch_shapes=[pltpu.VMEM((nq, tq, tq), jnp.int32),
                        pltpu.VMEM((nq, tq, tq), jnp.int16),
                        pltpu.VMEM((nq, tq, tq), jnp.int16),
                        pltpu.VMEM((nq, tq, tq), jnp.float32),
                        pltpu.VMEM((ATT_HEADS // 2, LANES, 2 * tq), MXU_DTYPE),
                        pltpu.VMEM((IDX_HEADS // 2, LANES, 2 * tq), MXU_DTYPE),
                        pltpu.VMEM((2, ATT_HEADS, tq, tq), jnp.float32),
                        pltpu.VMEM((2, ATT_HEADS, tq, tq), jnp.float32),
                        pltpu.VMEM((2, ATT_HEADS // 2, tq, 2 * tq), MXU_DTYPE),
                        pltpu.VMEM((ATT_HEADS, ATT_HEAD_DIM, tq), jnp.float32),
                        pltpu.VMEM((ATT_HEADS, tq), jnp.float32),
                        pltpu.VMEM((ATT_HEADS, tq), jnp.float32)],
        compiler_params=pltpu.CompilerParams(
            dimension_semantics=("parallel", "arbitrary"), vmem_limit_bytes=VMEM_LIMIT_BYTES),
    )(oa, oa, v_t, oa, okk, osm, bias_t)


def _mlstm_kernel(qk_ref, v_ref, og_ref, s_ref, cw_ref, cb_ref, gb_ref, ng_ref, y_ref,
                  xbuf, cst, mst, qk_s, in_s, p_s, kw_s, *, L):
    c = pl.program_id(1)
    H, DK, DV = MLSTM_HEADS, MLSTM_QK_DIM, MLSTM_V_DIM
    tail = SUBLANES

    @pl.when(c == 0)
    def _():
        xbuf[0:tail, :] = jnp.zeros((tail, 2 * QK_M), jnp.float32)
        cst[...] = jnp.zeros(cst.shape, jnp.float32)
        mst[...] = jnp.zeros(mst.shape, jnp.float32)

    @pl.when(c > 0)
    def _():
        xbuf[0:tail, :] = xbuf[L:L + tail, :]

    xbuf[tail:tail + L, :] = qk_ref[...]
    conv = cb_ref[...] + jnp.zeros((L, 2 * QK_M), jnp.float32)
    for j in range(CONV_WIDTH):
        conv = conv + cw_ref[j:j + 1, :] * xbuf[pl.ds(tail - (CONV_WIDTH - 1) + j, L), :]
    qk = conv * jax.nn.sigmoid(conv)
    assert 2 * DK == LANES and DV == LANES
    qT = qk[:, :QK_M].T
    kb = (qk[:, QK_M:] * (DK ** -0.5)).astype(MXU_DTYPE)
    vT = v_ref[...].T

    g = s_ref[...] + gb_ref[...]
    logf = jnp.minimum(g, 0.0) - jnp.log(1.0 + jnp.exp(-jnp.abs(g)))
    row = lax.broadcasted_iota(jnp.int32, (L, L), 0)
    col = lax.broadcasted_iota(jnp.int32, (L, L), 1)
    bcum = jnp.dot(jnp.where(row >= col, 1.0, 0.0), logf, precision=lax.Precision.HIGHEST,
                   preferred_element_type=jnp.float32)
    gT = g.T
    bT = bcum.T
    visible = row <= col
    top = lax.broadcasted_iota(jnp.int32, (LANES, L), 0) < DK
    one_row = jnp.where(lax.broadcasted_iota(jnp.int32, (LANES, L), 0) == 0, 1.0, 0.0)

    for h in range(H):
        pair = slice((h // 2) * LANES, (h // 2 + 1) * LANES)
        qp = qT[pair, :]
        qz = jnp.where(top, qp, 0.0) if h % 2 == 0 else jnp.where(top, 0.0, qp)
        qz = qz.astype(MXU_DTYPE)
        qk_s[h] = jnp.dot(kb[:, pair], qz, preferred_element_type=jnp.float32)
        in_s[h] = jnp.dot(cst[h].astype(MXU_DTYPE), qz, preferred_element_type=jnp.float32)

    stats = []
    for h in range(H):
        acol = g[:, S_IM + h:S_IM + h + 1] - bcum[:, S_FM + h:S_FM + h + 1]
        br = bT[S_FM + h:S_FM + h + 1, :]
        lir = gT[S_IM + h:S_IM + h + 1, :]
        m0 = mst[h:h + 1, 0:1]
        d = jnp.where(visible, br + acol, -jnp.inf)
        m_inter = br + m0
        m_t = jnp.maximum(m_inter, jnp.max(d, axis=0, keepdims=True))
        p = qk_s[h] * jnp.exp(d - m_t)
        p_s[h] = p.astype(MXU_DTYPE)
        b_end = br[:, L - 1:L]
        a = b_end - br + lir
        m_new = jnp.maximum(b_end + m0, jnp.max(a, axis=1, keepdims=True))
        vaug = jnp.concatenate([vT[h * DV:(h + 1) * DV, :], one_row], axis=0)
        kw_s[h] = (vaug * jnp.exp(a - m_new)).astype(MXU_DTYPE)
        stats.append((jnp.sum(p, axis=0, keepdims=True), jnp.exp(m_inter - m_t), m_t,
                      jnp.exp(b_end + m0 - m_new), m_new))

    for h in range(H):
        den_intra, sc, m_t, decay, m_new = stats[h]
        pair = slice((h // 2) * LANES, (h // 2 + 1) * LANES)
        inter = in_s[h]
        num = jnp.dot(vT[h * DV:(h + 1) * DV, :].astype(MXU_DTYPE), p_s[h],
                      preferred_element_type=jnp.float32) + sc * inter[:DV, :]
        den = den_intra + sc * inter[DV:DV + 1, :]
        hh = num / jnp.maximum(jnp.abs(den), jnp.exp(-m_t))
        cst[h] = decay * cst[h] + jnp.dot(kw_s[h], kb[:, pair], preferred_element_type=jnp.float32)
        mst[h:h + 1, :] = jnp.broadcast_to(m_new, (1, LANES))
        mu = jnp.mean(hh, axis=0, keepdims=True)
        hc = hh - mu
        var = jnp.mean(hc * hc, axis=0, keepdims=True)
        hn = (hc * lax.rsqrt(var + LN_EPS)).T * ng_ref[:, h * DV:(h + 1) * DV]
        y_ref[:, h * DV:(h + 1) * DV] = (hn * jax.nn.sigmoid(og_ref[:, h * DV:(h + 1) * DV])).astype(y_ref.dtype)


def _mlstm(om, osm, conv_w, conv_b, gate_bias, norm_g, B, S, L):
    nc = S // L
    W = MLSTM_WIDTH
    assert 2 * QK_M == W
    return pl.pallas_call(
        functools.partial(_mlstm_kernel, L=L),
        grid=(B, nc),
        in_specs=[pl.BlockSpec((L, W), lambda b, c: (b * nc + c, 0)),
                  pl.BlockSpec((L, W), lambda b, c: (b * nc + c, 1)),
                  pl.BlockSpec((L, W), lambda b, c: (b * nc + c, 2)),
                  pl.BlockSpec((L, COLS_S), lambda b, c: (b * nc + c, 0)),
                  pl.BlockSpec((CONV_WIDTH, W), lambda b, c: (0, 0)),
                  pl.BlockSpec((1, W), lambda b, c: (0, 0)),
                  pl.BlockSpec((1, COLS_S), lambda b, c: (0, 0)),
                  pl.BlockSpec((1, W), lambda b, c: (0, 0))],
        out_specs=pl.BlockSpec((L, W), lambda b, c: (b * nc + c, 0)),
        out_shape=jax.ShapeDtypeStruct((B * S, W), MXU_DTYPE),
        scratch_shapes=[pltpu.VMEM((L + 2 * SUBLANES, W), jnp.float32),
                        pltpu.VMEM((MLSTM_HEADS, 2 * MLSTM_V_DIM, LANES), jnp.float32),
                        pltpu.VMEM((SUBLANES, LANES), jnp.float32),
                        pltpu.VMEM((MLSTM_HEADS, L, L), jnp.float32),
                        pltpu.VMEM((MLSTM_HEADS, 2 * MLSTM_V_DIM, L), jnp.float32),
                        pltpu.VMEM((MLSTM_HEADS, L, L), MXU_DTYPE),
                        pltpu.VMEM((MLSTM_HEADS, 2 * MLSTM_V_DIM, L), MXU_DTYPE)],
        compiler_params=_cparams(("parallel", "arbitrary")),
    )(om, om, om, osm, conv_w, conv_b, gate_bias, norm_g)


def _layer_norm(z, g, b):
    mu = jnp.mean(z, axis=1, keepdims=True)
    zc = z - mu
    var = jnp.mean(zc * zc, axis=1, keepdims=True)
    return zc * lax.rsqrt(var + LN_EPS) * g + b


def _merge_kernel(ya_ref, ym_ref, g_ref, x_ref, wa_ref, wm_ref, wo_ref, lg_ref, lb_ref,
                  wr_ref, br_ref, x1_ref, xp_ref, idx_ref, gate_ref, cnt_ref, carry_ref, *, d_model):
    D = d_model
    mix = (jax.nn.sigmoid(g_ref[:, :D]) * jnp.dot(ya_ref[...], wa_ref[...], preferred_element_type=jnp.float32)
           + jax.nn.sigmoid(g_ref[:, D:]) * jnp.dot(ym_ref[...], wm_ref[...], preferred_element_type=jnp.float32))
    y = jnp.dot(mix.astype(MXU_DTYPE), wo_ref[...], preferred_element_type=jnp.float32)
    x1 = _layer_norm(DEEPNORM_ALPHA * x_ref[...] + y, lg_ref[...], lb_ref[...])
    x1_ref[...] = x1
    xb = x1.astype(MXU_DTYPE)
    bits = pltpu.bitcast(x1.astype(jnp.bfloat16).astype(jnp.float32), jnp.uint32)
    xp_ref[...] = (bits[:, :D // 2] & jnp.uint32(0xFFFF0000)) | (bits[:, D // 2:] >> 16)

    logits = jnp.dot(xb, wr_ref[...], preferred_element_type=jnp.float32) + br_ref[...]
    tm = logits.shape[0]
    lane = lax.broadcasted_iota(jnp.int32, (tm, LANES), 1)
    lane_f = lane.astype(jnp.float32)
    vals, idxs = [], []
    for _ in range(TOP_K):
        mx = jnp.max(logits, axis=1, keepdims=True)
        ix = jnp.min(jnp.where(logits == mx, lane_f, float(LANES)), axis=1, keepdims=True)
        vals.append(mx)
        idxs.append(ix)
        logits = jnp.where(lane_f == ix, -jnp.inf, logits)
    es = [jnp.exp(v - vals[0]) for v in vals]
    tot = es[0]
    for e in es[1:]:
        tot = tot + e
    @pl.when(pl.program_id(0) == 0)
    def _():
        carry_ref[...] = jnp.zeros(carry_ref.shape, jnp.float32)

    hots = [lane_f == ix for ix in idxs]
    c = jnp.zeros((tm, LANES), jnp.float32)
    for hot in hots:
        c = c + jnp.where(hot, 1.0, 0.0)
    before = jnp.where(lax.broadcasted_iota(jnp.int32, (tm, tm), 0)
                       > lax.broadcasted_iota(jnp.int32, (tm, tm), 1), 1.0, 0.0).astype(MXU_DTYPE)
    carry = carry_ref[0:1, :]
    prior = jnp.dot(before, c.astype(MXU_DTYPE), preferred_element_type=jnp.float32) + carry
    total = carry + jnp.sum(c, axis=0, keepdims=True)
    carry_ref[...] = jnp.broadcast_to(total, carry_ref.shape)
    cnt_ref[...] = jnp.broadcast_to(total, cnt_ref.shape)

    idx_out = jnp.zeros((tm, LANES), jnp.float32)
    gate_out = jnp.zeros((tm, LANES), jnp.float32)
    for k in range(TOP_K):
        idx_out = jnp.where(lane == k, idxs[k], idx_out)
        rank_k = jnp.sum(jnp.where(hots[k], prior, 0.0), axis=1, keepdims=True)
        idx_out = jnp.where(lane == TOP_K + k, rank_k, idx_out)
        gate_out = jnp.where(lane == k, es[k] / tot, gate_out)
    idx_ref[...] = idx_out.T[:2 * TOP_K, :].astype(jnp.int32)
    gate_ref[...] = gate_out


def _merge(y_att, y_m, og, x2, wa, wm, wo, ln_g, ln_b, wr, br, tm):
    T, D = x2.shape
    full = lambda shape: pl.BlockSpec(shape, lambda i: (0, 0))
    return pl.pallas_call(
        functools.partial(_merge_kernel, d_model=D),
        grid=(T // tm,),
        in_specs=[pl.BlockSpec((tm, ATT_WIDTH), lambda i: (i, 0)),
                  pl.BlockSpec((tm, MLSTM_WIDTH), lambda i: (i, 0)),
                  pl.BlockSpec((tm, 2 * D), lambda i: (i, 0)),
                  pl.BlockSpec((tm, D), lambda i: (i, 0)),
                  full(wa.shape), full(wm.shape), full(wo.shape),
                  full((1, D)), full((1, D)), full(wr.shape), full((1, LANES))],
        out_specs=[pl.BlockSpec((tm, D), lambda i: (i, 0)),
                   pl.BlockSpec((tm, D // 2), lambda i: (i, 0)),
                   pl.BlockSpec((2 * TOP_K, tm), lambda i: (0, i)),
                   pl.BlockSpec((tm, LANES), lambda i: (i, 0)),
                   pl.BlockSpec((SUBLANES, LANES), lambda i: (0, 0))],
        out_shape=[jax.ShapeDtypeStruct((T, D), jnp.float32),
                   jax.ShapeDtypeStruct((T, D // 2), jnp.uint32),
                   jax.ShapeDtypeStruct((2 * TOP_K, T), jnp.int32),
                   jax.ShapeDtypeStruct((T, LANES), jnp.float32),
                   jax.ShapeDtypeStruct((SUBLANES, LANES), jnp.float32)],
        scratch_shapes=[pltpu.VMEM((SUBLANES, LANES), jnp.float32)],
        compiler_params=_cparams(("arbitrary",)),
    )(y_att, y_m, og, x2, wa, wm, wo, ln_g, ln_b, wr, br)


def _dispatch_sc(dest, xp, cap):
    T, W = xp.shape
    n_chunks = T // SC_CHUNK
    n_workers = SC_CORES * SC_SUBCORES
    assert T % SC_CHUNK == 0 and n_chunks % n_workers == 0
    per_worker = n_chunks // n_workers
    idx = dest.reshape(TOP_K, n_chunks, SC_CHUNK).transpose(1, 0, 2)
    mesh = plsc.VectorSubcoreMesh(core_axis_name="c", subcore_axis_name="s")

    @functools.partial(
        pl.kernel, mesh=mesh,
        out_type=jax.ShapeDtypeStruct((cap, W), xp.dtype),
        scratch_types=[pltpu.VMEM((SC_CHUNK, W), xp.dtype),
                       pltpu.VMEM((TOP_K, SC_CHUNK), jnp.int32)])
    def scatter_rows(x_hbm, idx_hbm, xs_hbm, rows_v, idx_v):
        worker = lax.axis_index("s") * SC_CORES + lax.axis_index("c")

        def body(j, carry):
            c = worker * per_worker + j
            pltpu.sync_copy(x_hbm.at[pl.ds(c * SC_CHUNK, SC_CHUNK)], rows_v)
            pltpu.sync_copy(idx_hbm.at[c], idx_v)
            for k in range(TOP_K):
                pltpu.sync_copy(rows_v, xs_hbm.at[idx_v.at[k]])
            return carry

        lax.fori_loop(0, per_worker, body, 0)

    return scatter_rows(xp, idx)


def _ffn_kernel(be_ref, nu_ref, nv_ref, xs_ref, wgu_ref, bgu_ref, wd_ref, bd_ref, y_ref, wgu_b,
                wd_b, act_s, *, d_ff):
    r = pl.program_id(0)
    e = be_ref[r]
    prev = be_ref[jnp.maximum(r - 1, 0)]

    @pl.when((r == 0) | (e != prev))
    def _():
        wgu_b[...] = wgu_ref[0].astype(MXU_DTYPE)
        wd_b[...] = wd_ref[0].astype(MXU_DTYPE)

    @pl.when(r < nu_ref[0])
    def _():
        live = lax.broadcasted_iota(jnp.int32, (xs_ref.shape[0], 1), 0) < nv_ref[r]
        w = jnp.where(live, xs_ref[...], jnp.uint32(0))
        half = w.shape[1]
        x_hi = pltpu.bitcast(w & jnp.uint32(0xFFFF0000), jnp.float32).astype(MXU_DTYPE)
        x_lo = pltpu.bitcast(w << 16, jnp.float32).astype(MXU_DTYPE)
        x = jnp.concatenate([x_hi, x_lo], axis=1)
        step = 512
        for j in range(0, d_ff, step):
            def gu(lo):
                return (jnp.dot(x, wgu_b[:, lo:lo + step], preferred_element_type=jnp.float32)
                        + bgu_ref[0, :, lo:lo + step])
            gate = jnp.minimum(gu(j), SWIGLU_LIMIT)
            up = jnp.clip(gu(d_ff + j), -SWIGLU_LIMIT, SWIGLU_LIMIT)
            act = (up + 1.0) * (gate * jax.nn.sigmoid(SWIGLU_ALPHA * gate))
            act_s[:, j:j + step] = act.astype(MXU_DTYPE)
        y = jnp.dot(act_s[...], wd_b[...], preferred_element_type=jnp.float32) + bd_ref[0]
        bits = pltpu.bitcast(y.astype(jnp.bfloat16).astype(jnp.float32), jnp.uint32)
        y_ref[...] = (bits[:, :half] & jnp.uint32(0xFFFF0000)) | (bits[:, half:] >> 16)

    @pl.when(r >= nu_ref[0])
    def _():
        y_ref[...] = jnp.zeros(y_ref.shape, y_ref.dtype)


def _expert_ffn(block_expert, n_used, n_valid, xs, w_gate_up, b_gate_up, w_down, b_down, bm):
    cap, half = xs.shape
    E, D, F2 = w_gate_up.shape
    d_ff = F2 // 2
    grid_spec = pltpu.PrefetchScalarGridSpec(
        num_scalar_prefetch=3,
        grid=(cap // bm,),
        in_specs=[pl.BlockSpec((bm, half), lambda r, be, nu, nv: (jnp.minimum(r, nu[0] - 1), 0)),
                  pl.BlockSpec((1, D, F2), lambda r, be, nu, nv: (be[r], 0, 0)),
                  pl.BlockSpec((1, 1, F2), lambda r, be, nu, nv: (be[r], 0, 0)),
                  pl.BlockSpec((1, d_ff, D), lambda r, be, nu, nv: (be[r], 0, 0)),
                  pl.BlockSpec((1, 1, D), lambda r, be, nu, nv: (be[r], 0, 0))],
        out_specs=pl.BlockSpec((bm, half), lambda r, be, nu, nv: (r, 0)),
        scratch_shapes=[pltpu.VMEM((D, F2), MXU_DTYPE),
                        pltpu.VMEM((d_ff, D), MXU_DTYPE),
                        pltpu.VMEM((bm, d_ff), MXU_DTYPE)],
    )
    return pl.pallas_call(
        functools.partial(_ffn_kernel, d_ff=d_ff),
        grid_spec=grid_spec,
        out_shape=jax.ShapeDtypeStruct((cap, half), jnp.uint32),
        compiler_params=_cparams(("arbitrary",)),
    )(block_expert, n_used, n_valid, xs, w_gate_up, b_gate_up.reshape(E, 1, F2), w_down,
      b_down.reshape(E, 1, D))


def _gather_sc(dest, ybuf):
    cap, D = ybuf.shape
    T = dest.shape[1]
    chunk = SC_CHUNK
    n_chunks = T // chunk
    n_workers = SC_CORES * SC_SUBCORES
    assert T % chunk == 0 and n_chunks % n_workers == 0
    per_worker = n_chunks // n_workers
    idx = dest.reshape(TOP_K, n_chunks, chunk).transpose(1, 0, 2)
    mesh = plsc.VectorSubcoreMesh(core_axis_name="c", subcore_axis_name="s")

    @functools.partial(
        pl.kernel, mesh=mesh,
        out_type=jax.ShapeDtypeStruct((TOP_K, T, D), ybuf.dtype),
        scratch_types=[pltpu.VMEM((chunk, D), ybuf.dtype),
                       pltpu.VMEM((TOP_K, chunk), jnp.int32)])
    def gather_rows(y_hbm, idx_hbm, out_hbm, rows_v, idx_v):
        worker = lax.axis_index("s") * SC_CORES + lax.axis_index("c")

        def body(j, carry):
            c = worker * per_worker + j
            pltpu.sync_copy(idx_hbm.at[c], idx_v)
            for k in range(TOP_K):
                pltpu.sync_copy(y_hbm.at[idx_v.at[k]], rows_v)
                pltpu.sync_copy(rows_v, out_hbm.at[k, pl.ds(c * chunk, chunk)])
            return carry

        lax.fori_loop(0, per_worker, body, 0)

    return gather_rows(ybuf, idx)


def _combine_dense_kernel(yk_ref, gate_ref, x1_ref, lg_ref, lb_ref, *rest):
    o_ref = rest[-1]
    left = right = None
    for k in range(TOP_K):
        w = yk_ref[k]
        g = gate_ref[:, k:k + 1]
        hi = g * pltpu.bitcast(w & jnp.uint32(0xFFFF0000), jnp.float32)
        lo = g * pltpu.bitcast(w << 16, jnp.float32)
        left = hi if left is None else left + hi
        right = lo if right is None else right + lo
    y = jnp.concatenate([left, right], axis=1)
    o_ref[...] = _layer_norm(DEEPNORM_ALPHA * x1_ref[...] + y, lg_ref[...], lb_ref[...])


def _combine_dense(yk, first_tile, gates, x1, ln_g, ln_b, prev, tm):
    T, D = x1.shape
    n = yk.shape[1] // tm
    in_specs = [pl.BlockSpec((TOP_K, tm, D // 2), lambda i: (0, i, 0)),
                pl.BlockSpec((tm, LANES), lambda i: (i + first_tile, 0)),
                pl.BlockSpec((tm, D), lambda i: (i + first_tile, 0)),
                pl.BlockSpec((1, D), lambda i: (0, 0)),
                pl.BlockSpec((1, D), lambda i: (0, 0))]
    args = [yk, gates, x1, ln_g, ln_b]
    aliases = {}
    if prev is not None:
        in_specs.append(pl.BlockSpec(memory_space=pl.ANY))
        args.append(prev)
        aliases = {len(args) - 1: 0}
    return pl.pallas_call(
        _combine_dense_kernel,
        grid=(n,),
        in_specs=in_specs,
        out_specs=pl.BlockSpec((tm, D), lambda i: (i + first_tile, 0)),
        out_shape=jax.ShapeDtypeStruct((T, D), jnp.float32),
        input_output_aliases=aliases,
        compiler_params=_cparams(("parallel",)),
    )(*args)


def _tile(n, pref):
    t = min(n, pref)
    assert n % t == 0
    return t


def _relayout_w_in(w_in, d_model):
    sizes = (ATT_WIDTH, ATT_WIDTH, ATT_WIDTH, IDX_HEADS * IDX_HEAD_DIM, IDX_HEAD_DIM, IDX_HEADS,
             QK_M, QK_M, MLSTM_WIDTH, MLSTM_HEADS, MLSTM_HEADS, MLSTM_WIDTH, d_model, d_model)
    offs = [0]
    for s in sizes:
        offs.append(offs[-1] + s)
    seg = lambda k: w_in[:, offs[k]:offs[k + 1]]
    (q_a, k_a, v_a, q_i, k_i, w_i, q_m, k_m, v_m, i_m, f_m, o_m, g_a, g_m) = [seg(k) for k in range(14)]
    pad = jnp.zeros((w_in.shape[0], COLS_S - (IDX_HEAD_DIM + IDX_HEADS + 2 * MLSTM_HEADS)), w_in.dtype)
    cols = [q_a, k_a, q_i, k_i, w_i, i_m, f_m, pad, q_m, k_m, v_m, o_m, g_a, g_m, k_i, k_i, v_a]
    return jnp.concatenate(cols, axis=1).astype(MXU_DTYPE)


def _layer(x2, B, S, w_in, conv_w, conv_b, i_bias, f_bias, norm_g, w_branch_attn, w_branch_mlstm,
           w_out, ln1_g, ln1_b, w_router, b_router, w_gate_up, b_gate_up, w_down, b_down,
           ln2_g, ln2_b, rel_bias):
    T, D = x2.shape
    bf = MXU_DTYPE
    tq = _tile(S, 256)
    L = _tile(S, 256)
    tm = _tile(T, 256)

    oa, osm, om, og, okk, v_t = _project(x2, _relayout_w_in(w_in, D), _tile(T, 512), tq)
    bias_t = _bias_tiles(rel_bias, tq)
    y_att = _dsa_attention(oa, osm, okk, v_t, bias_t, B, S, tq)

    gate_bias = jnp.zeros((1, COLS_S), jnp.float32)
    gate_bias = gate_bias.at[0, S_IM:S_IM + MLSTM_HEADS].set(i_bias)
    gate_bias = gate_bias.at[0, S_FM:S_FM + MLSTM_HEADS].set(f_bias)
    y_m = _mlstm(om, osm, conv_w, conv_b.reshape(1, -1), gate_bias, norm_g.reshape(1, -1), B, S, L)

    wr = jnp.zeros((D, LANES), bf).at[:, :N_EXPERTS].set(w_router.astype(bf))
    br = jnp.full((1, LANES), NEG_BIG, jnp.float32).at[0, :N_EXPERTS].set(b_router)
    x1, xp, idx, gates, cnt = _merge(y_att, y_m, og, x2, w_branch_attn.astype(bf),
                                     w_branch_mlstm.astype(bf), w_out.astype(bf), ln1_g.reshape(1, D),
                                     ln1_b.reshape(1, D), wr, br, _tile(T, 512))

    bm = 1024
    counts = cnt[0, :N_EXPERTS].astype(jnp.int32)
    padded = ((counts + bm - 1) // bm) * bm
    pend = jnp.cumsum(padded)
    pstart = pend - padded
    cap = ((T * TOP_K + bm - 1) // bm) * bm + N_EXPERTS * bm
    n_blocks = cap // bm
    experts = jnp.arange(N_EXPERTS, dtype=jnp.int32)[:, None, None]
    dest = jnp.sum(jnp.where(idx[None, :TOP_K, :] == experts, pstart[:, None, None], 0), axis=0) \
        + idx[TOP_K:, :]
    block_row = jnp.arange(n_blocks, dtype=jnp.int32) * bm
    block_expert = jnp.minimum(jnp.sum((pend[None, :] <= block_row[:, None]).astype(jnp.int32), axis=1),
                               N_EXPERTS - 1)
    n_used = (pend[-1:] // bm).astype(jnp.int32)
    n_valid = jnp.clip((pstart + counts)[block_expert] - block_row, 0, bm).astype(jnp.int32)

    xs = _dispatch_sc(dest, xp, cap)
    ybuf = _expert_ffn(block_expert, n_used, n_valid, xs, w_gate_up, b_gate_up, w_down, b_down, bm)
    quantum = SC_CORES * SC_SUBCORES * SC_CHUNK
    parts = (1, 1, 2, 4) if T % (8 * quantum) == 0 else (1,)
    bounds = [0]
    for p in parts:
        bounds.append(bounds[-1] + p * T // sum(parts))
    slabs = [_gather_sc(dest[:, lo:hi], ybuf) for lo, hi in zip(bounds[:-1], bounds[1:])]
    out = None
    tc = _tile(T // sum(parts), 1024)
    for lo, yk in zip(bounds[:-1], slabs):
        out = _combine_dense(yk, lo // tc, gates, x1, ln2_g.reshape(1, D), ln2_b.reshape(1, D),
                             out, tc)
    return out


def kernel(x, w_in, conv_w, conv_b, mlstm_i_bias, mlstm_f_bias, mlstm_norm_g, w_branch_attn,
           w_branch_mlstm, w_out, ln1_g, ln1_b, w_router, b_router, w_gate_up, b_gate_up,
           w_down, b_down, ln2_g, ln2_b, rel_bias):
    B, S, D = x.shape
    x2 = x.reshape(B * S, D)
    for l in range(w_in.shape[0]):
        x2 = _layer(x2, B, S, w_in[l], conv_w[l], conv_b[l], mlstm_i_bias[l], mlstm_f_bias[l],
                    mlstm_norm_g[l], w_branch_attn[l], w_branch_mlstm[l], w_out[l], ln1_g[l], ln1_b[l],
                    w_router[l], b_router[l], w_gate_up[l], b_gate_up[l], w_down[l], b_down[l],
                    ln2_g[l], ln2_b[l], rel_bias)
    return x2.reshape(B, S, D)
```

```python
import functools
import math

import jax
import jax.numpy as jnp
from jax import lax
from jax.experimental import pallas as pl
from jax.experimental.pallas import tpu as pltpu
from jax.experimental.pallas import tpu_sc as plsc

ATT_HEADS = 8
ATT_HEAD_DIM = 64
ATT_WIDTH = ATT_HEADS * ATT_HEAD_DIM
IDX_HEADS = 8
IDX_HEAD_DIM = 64
TOPK_MAX = 256
MLSTM_HEADS = 4
MLSTM_QK_DIM = 64
MLSTM_V_DIM = 128
MLSTM_WIDTH = MLSTM_HEADS * MLSTM_V_DIM
CONV_WIDTH = 4
N_BUCKETS = 32
MAX_DISTANCE = 128
N_EXPERTS = 32
TOP_K = 4
SWIGLU_ALPHA = 1.702
SWIGLU_LIMIT = 7.0
LN_EPS = 1e-5
DEPTH = 1
DEEPNORM_ALPHA = (2 * DEPTH) ** 0.25

LANES = 128
SUBLANES = 8
VMEM_LIMIT_BYTES = 56 * 1024 * 1024
SC_CORES = 2
SC_SUBCORES = 16
SC_CHUNK = 128

MXU_DTYPE = jnp.bfloat16

INT_MIN = -(2 ** 31)
NEG_BIG = -1e30

QK_M = MLSTM_HEADS * MLSTM_QK_DIM
COLS_A = 2 * ATT_WIDTH + IDX_HEADS * IDX_HEAD_DIM
COLS_S = LANES
COLS_M = 2 * QK_M + 2 * MLSTM_WIDTH
COLS_K = 2 * IDX_HEAD_DIM
S_WI = IDX_HEAD_DIM
S_IM = S_WI + IDX_HEADS
S_FM = S_IM + MLSTM_HEADS


def _cparams(sem):
    return pltpu.CompilerParams(dimension_semantics=sem, vmem_limit_bytes=VMEM_LIMIT_BYTES)


def _proj_kernel(x_ref, w_ref, oa_ref, os_ref, om_ref, og_ref, ok_ref, ovt_ref, *, d_model):
    xb = x_ref[...].astype(MXU_DTYPE)
    step = 512

    def mm(lo, hi):
        return jnp.dot(xb, w_ref[:, lo:hi], preferred_element_type=jnp.float32)

    base = 0
    for j in range(0, COLS_A, step):
        oa_ref[:, j:j + step] = mm(base + j, base + j + step).astype(MXU_DTYPE)
    base += COLS_A
    os_ref[...] = mm(base, base + COLS_S)
    base += COLS_S
    for j in range(0, COLS_M, step):
        om_ref[:, j:j + step] = mm(base + j, base + j + step)
    base += COLS_M
    for j in range(0, 2 * d_model, step):
        og_ref[:, j:j + step] = mm(base + j, base + j + step)
    base += 2 * d_model
    ok_ref[...] = mm(base, base + COLS_K).astype(MXU_DTYPE)
    base += COLS_K
    v = mm(base, base + ATT_WIDTH)
    tq = ovt_ref.shape[2]
    for s in range(ovt_ref.shape[0]):
        ovt_ref[s] = v[s * tq:(s + 1) * tq, :].T.astype(MXU_DTYPE)


def _project(x2, w_p, tm, tq):
    T, D = x2.shape
    n_all = w_p.shape[1]
    assert tm % tq == 0
    return pl.pallas_call(
        functools.partial(_proj_kernel, d_model=D),
        grid=(T // tm,),
        in_specs=[pl.BlockSpec((tm, D), lambda i: (i, 0)),
                  pl.BlockSpec((D, n_all), lambda i: (0, 0))],
        out_specs=[pl.BlockSpec((tm, COLS_A), lambda i: (i, 0)),
                   pl.BlockSpec((tm, COLS_S), lambda i: (i, 0)),
                   pl.BlockSpec((tm, COLS_M), lambda i: (i, 0)),
                   pl.BlockSpec((tm, 2 * D), lambda i: (i, 0)),
                   pl.BlockSpec((tm, COLS_K), lambda i: (i, 0)),
                   pl.BlockSpec((tm // tq, ATT_WIDTH, tq), lambda i: (i, 0, 0))],
        out_shape=[jax.ShapeDtypeStruct((T, COLS_A), MXU_DTYPE),
                   jax.ShapeDtypeStruct((T, COLS_S), jnp.float32),
                   jax.ShapeDtypeStruct((T, COLS_M), jnp.float32),
                   jax.ShapeDtypeStruct((T, 2 * D), jnp.float32),
                   jax.ShapeDtypeStruct((T, COLS_K), MXU_DTYPE),
                   jax.ShapeDtypeStruct((T // tq, ATT_WIDTH, tq), MXU_DTYPE)],
        compiler_params=_cparams(("parallel",)),
    )(x2, w_p)


def _bias_kernel(rb_ref, o_ref, *, tq):
    h = pl.program_id(0)
    s = lax.broadcasted_iota(jnp.int32, (tq, tq), 0)
    t = lax.broadcasted_iota(jnp.int32, (tq, tq), 1)
    max_exact = N_BUCKETS // 2
    for d in range(3):
        n = jnp.maximum(t - s + d * tq, 0)
        n_f = jnp.maximum(n, 1).astype(jnp.float32)
        large = max_exact + (jnp.log(n_f / max_exact) / math.log(MAX_DISTANCE / max_exact)
                             * (N_BUCKETS - max_exact)).astype(jnp.int32)
        large = jnp.minimum(large, N_BUCKETS - 1)
        bucket = jnp.where(n < max_exact, n, large)
        acc = jnp.zeros((tq, tq), jnp.float32)
        for k in range(N_BUCKETS):
            acc = jnp.where(bucket == k, rb_ref[h, k], acc)
        o_ref[0, d] = acc


def _bias_tiles(rel_bias, tq):
    assert tq + 1 >= MAX_DISTANCE
    H = rel_bias.shape[0]
    return pl.pallas_call(
        functools.partial(_bias_kernel, tq=tq),
        grid=(H,),
        in_specs=[pl.BlockSpec(memory_space=pltpu.SMEM)],
        out_specs=pl.BlockSpec((1, 3, tq, tq), lambda h: (h, 0, 0, 0)),
        out_shape=jax.ShapeDtypeStruct((H, 3, tq, tq), jnp.float32),
        compiler_params=_cparams(("parallel",)),
    )(rel_bias)


def _dsa_kernel(qa_ref, ka_ref, vt_ref, qi_ref, kk_ref, wq_ref, bias_ref, o_ref,
                keys_ref, hi_ref, lo_ref, msk_ref, qz_ref, qiz_ref, s_ref, sb_ref, p_ref, acc_ref, m_ref,
                l_ref,
                *, tq, n_sel):
    i = pl.program_id(1)
    nch = i + 1
    t_pos = i * tq + lax.broadcasted_iota(jnp.int32, (1, tq), 1)
    s_loc = lax.broadcasted_iota(jnp.int32, (tq, 1), 0)
    hd = ATT_HEAD_DIM
    n_pairs = ATT_HEADS // 2
    assert 2 * hd == LANES and IDX_HEAD_DIM == hd and IDX_HEADS == ATT_HEADS

    top = lax.broadcasted_iota(jnp.int32, (LANES, tq), 0) < hd

    def pair_operand(blk):
        bt = blk.astype(jnp.float32).T
        return jnp.concatenate([jnp.where(top, bt, 0.0), jnp.where(top, 0.0, bt)],
                               axis=1).astype(MXU_DTYPE)

    for p in range(n_pairs):
        pair = slice(p * LANES, (p + 1) * LANES)
        qz_ref[p] = pair_operand(qa_ref[:, pair] * (hd ** -0.5))
        qiz_ref[p] = pair_operand(qi_ref[:, pair])

    ws = wq_ref[...].T[S_WI:S_WI + IDX_HEADS, :] * (IDX_HEADS ** -0.5)

    def score_tile(c):
        off = pl.multiple_of(c * tq, tq)
        kk = kk_ref[pl.ds(off, tq), :]
        sc = jnp.zeros((tq, tq), jnp.float32)
        for p in range(n_pairs):
            d = jnp.dot(kk, qiz_ref[p], preferred_element_type=jnp.float32)
            for j in range(2):
                h = 2 * p + j
                sc = sc + ws[h:h + 1, :] * jnp.maximum(d[:, j * tq:(j + 1) * tq], 0.0)
        sc = sc + 0.0
        b = pltpu.bitcast(sc, jnp.int32)
        sk = b ^ ((b >> 31) & jnp.int32(0x7FFFFFFF))
        sk = jnp.where(off + s_loc <= t_pos, sk, jnp.int32(INT_MIN))
        keys_ref[c] = sk
        hi_ref[c] = (sk >> 16).astype(jnp.int16)

    def score_pair(c2, carry):
        score_tile(2 * c2)
        score_tile(2 * c2 + 1)
        return carry

    def score_single(c, carry):
        score_tile(c)
        return carry

    lax.fori_loop(0, nch // 2, score_pair, 0)
    lax.fori_loop(2 * (nch // 2), nch, score_single, 0)

    pack = 2 * SUBLANES
    half_min = -(2 ** 15)

    def count16(ref, pred_fn):
        def tile_count(c):
            hit = jnp.where(pred_fn(ref[c]), jnp.int16(1), jnp.int16(0))
            parts = [hit[r * pack:(r + 1) * pack, :] for r in range(tq // pack)]
            while len(parts) > 1:
                parts = [a + b for a, b in zip(parts[::2], parts[1::2])]
            return parts[0]
        acc = lax.fori_loop(0, nch // 2,
                            lambda c2, a: a + (tile_count(2 * c2) + tile_count(2 * c2 + 1)),
                            jnp.zeros((pack, tq), jnp.int16))
        acc = lax.fori_loop(2 * (nch // 2), nch, lambda c, a: a + tile_count(c), acc)
        return jnp.sum(acc.astype(jnp.float32), axis=0, keepdims=True)

    def search16(ref, target):
        def bit_body(it, u):
            cand_u = u | lax.shift_left(jnp.int32(1), 15 - it)
            cand = (cand_u + half_min).astype(jnp.int16)
            return jnp.where(count16(ref, lambda x: x >= cand) >= target, cand_u, u)
        return lax.fori_loop(0, 16, bit_body, jnp.zeros((1, tq), jnp.int32))

    u_hi = search16(hi_ref, float(n_sel))
    thr_hi = (u_hi + half_min).astype(jnp.int16)
    need_lo = n_sel - count16(hi_ref, lambda x: x > thr_hi)

    def low_body(c, carry):
        lo = ((keys_ref[c] & 0xFFFF) + half_min).astype(jnp.int16)
        lo_ref[c] = jnp.where(hi_ref[c] == thr_hi, lo, jnp.int16(half_min))
        return carry

    lax.fori_loop(0, nch, low_body, 0)
    u_lo = search16(lo_ref, need_lo)
    thr_lo = (u_lo + half_min).astype(jnp.int16)
    thr = lax.shift_left(u_hi + half_min, 16) | u_lo
    need = jnp.where(thr == INT_MIN, 0.0, need_lo - count16(lo_ref, lambda x: x > thr_lo))
    n_eq = count16(lo_ref, lambda x: x == thr_lo)
    surplus = jnp.where((thr != INT_MIN) & (n_eq > need), 1.0, 0.0)
    has_surplus = jnp.max(surplus) > 0.0

    @pl.when(has_surplus)
    def _():
        tri = jnp.where(lax.broadcasted_iota(jnp.int32, (tq, tq), 0)
                        >= lax.broadcasted_iota(jnp.int32, (tq, tq), 1), 1.0, 0.0).astype(MXU_DTYPE)

        def mask_body(c, run):
            kc = keys_ref[c]
            eq = kc == thr
            eqf = jnp.where(eq, 1.0, 0.0)
            prefix = jnp.dot(tri, eqf.astype(MXU_DTYPE), preferred_element_type=jnp.float32) + run
            msk_ref[c] = jnp.where(kc > thr, 0.0,
                                   jnp.where(eq, jnp.where(prefix <= need, 0.0, NEG_BIG), NEG_BIG))
            return run + jnp.sum(eqf, axis=0, keepdims=True)

        lax.fori_loop(0, nch, mask_body, jnp.zeros((1, tq), jnp.float32))

    @pl.when(jnp.logical_not(has_surplus))
    def _():
        floor = jnp.maximum(thr, INT_MIN + 1)

        def mask_body(c, carry):
            msk_ref[c] = jnp.where(keys_ref[c] >= floor, 0.0, NEG_BIG)
            return carry

        lax.fori_loop(0, nch, mask_body, 0)

    m_ref[...] = jnp.full(m_ref.shape, NEG_BIG, jnp.float32)
    l_ref[...] = jnp.zeros(l_ref.shape, jnp.float32)
    acc_ref[...] = jnp.zeros(acc_ref.shape, jnp.float32)

    def logit_stage(cs, far, sr):
        tile_max = [None] * ATT_HEADS
        for n, c in enumerate(cs):
            off = pl.multiple_of(c * tq, tq)
            mk = msk_ref[c]
            for p in range(n_pairs):
                pair = slice(p * LANES, (p + 1) * LANES)
                s2 = jnp.dot(ka_ref[pl.ds(off, tq), pair], qz_ref[p],
                             preferred_element_type=jnp.float32)
                for j in range(2):
                    h = 2 * p + j
                    s = s2[:, j * tq:(j + 1) * tq] + mk
                    if not far:
                        s = s + bias_ref[h, i - c]
                    sr[n, h] = s
                    mx = jnp.max(s, axis=0, keepdims=True)
                    tile_max[h] = mx if tile_max[h] is None else jnp.maximum(tile_max[h], mx)
        return tile_max

    def value_stage(cs, far, sr, tile_max):
        alphas = []
        for h in range(ATT_HEADS):
            m_prev = m_ref[h:h + 1, :]
            if far:
                b_far = bias_ref[h, 2, 0:1, 0:1]
                m_new = jnp.maximum(m_prev, tile_max[h] + b_far)
                shift = m_new - b_far
            else:
                m_new = jnp.maximum(m_prev, tile_max[h])
                shift = m_new
            alpha = jnp.exp(m_prev - m_new)
            for n in range(len(cs)):
                pe = jnp.exp(sr[n, h] - shift)
                p_ref[n, h // 2, :, (h % 2) * tq:(h % 2 + 1) * tq] = pe.astype(MXU_DTYPE)
            m_ref[h:h + 1, :] = m_new
            alphas.append(alpha)
        ones_rows = jnp.ones((2 * SUBLANES, tq), MXU_DTYPE)
        for h in range(ATT_HEADS):
            pv = None
            for n, c in enumerate(cs):
                va = jnp.concatenate([vt_ref[c, h * hd:(h + 1) * hd, :], ones_rows], axis=0)
                d = jnp.dot(va, p_ref[n, h // 2, :, (h % 2) * tq:(h % 2 + 1) * tq],
                            preferred_element_type=jnp.float32)
                pv = d if pv is None else pv + d
            acc_ref[h] = alphas[h] * acc_ref[h] + pv[:hd]
            l_ref[h:h + 1, :] = alphas[h] * l_ref[h:h + 1, :] + pv[hd:hd + 1]

    def att_step(cs, far):
        value_stage(cs, far, s_ref, logit_stage(cs, far, s_ref))

    def far_single(c, carry):
        att_step([c], True)
        return carry

    n_far = jnp.maximum(i - 1, 0)
    n_fp = n_far // 2

    tiles = lambda q: [2 * q, 2 * q + 1]

    @pl.when(n_fp > 0)
    def _():
        def two_pairs(m, max_a):
            max_b = logit_stage(tiles(2 * m + 1), True, sb_ref)
            value_stage(tiles(2 * m), True, s_ref, max_a)
            nxt_a = logit_stage(tiles(2 * m + 2), True, s_ref)
            value_stage(tiles(2 * m + 1), True, sb_ref, max_b)
            return nxt_a

        n_dbl = (n_fp - 1) // 2
        max_a = lax.fori_loop(0, n_dbl, two_pairs, logit_stage(tiles(0), True, s_ref))
        q = 2 * n_dbl

        @pl.when(n_fp - q == 2)
        def _():
            max_b = logit_stage(tiles(q + 1), True, sb_ref)
            value_stage(tiles(q), True, s_ref, max_a)
            value_stage(tiles(q + 1), True, sb_ref, max_b)

        @pl.when(n_fp - q == 1)
        def _():
            value_stage(tiles(q), True, s_ref, max_a)

    lax.fori_loop(2 * n_fp, n_far, far_single, 0)

    @pl.when(i > 0)
    def _():
        max_a = logit_stage([i - 1], False, s_ref)
        max_b = logit_stage([i], False, sb_ref)
        value_stage([i - 1], False, s_ref, max_a)
        value_stage([i], False, sb_ref, max_b)

    @pl.when(i == 0)
    def _():
        att_step([i], False)
    y_t = jnp.concatenate([acc_ref[h] / l_ref[h:h + 1, :] for h in range(ATT_HEADS)], axis=0)
    o_ref[...] = y_t.T.astype(o_ref.dtype)


def _dsa_attention(oa, osm, okk, v_t, bias_t, B, S, tq):
    nq = S // tq
    n_sel = min(TOPK_MAX, S // 4)
    W = ATT_WIDTH
    assert v_t.shape == (B * nq, W, tq)
    return pl.pallas_call(
        functools.partial(_dsa_kernel, tq=tq, n_sel=n_sel),
        grid=(B, nq),
        in_specs=[pl.BlockSpec((tq, W), lambda b, i: (b * nq + i, 0)),
                  pl.BlockSpec((S, W), lambda b, i: (b, 1)),
                  pl.BlockSpec((nq, W, tq), lambda b, i: (b, 0, 0)),
                  pl.BlockSpec((tq, W), lambda b, i: (b * nq + i, 2)),
                  pl.BlockSpec((S, COLS_K), lambda b, i: (b, 0)),
                  pl.BlockSpec((tq, COLS_S), lambda b, i: (b * nq + i, 0)),
                  pl.BlockSpec((ATT_HEADS, 3, tq, tq), lambda b, i: (0, 0, 0, 0))],
        out_specs=pl.BlockSpec((tq, W), lambda b, i: (b * nq + i, 0)),
        out_shape=jax.ShapeDtypeStruct((B * S, W), MXU_DTYPE),
        scratch_shapes=[pltpu.VMEM((nq, tq, tq), jnp.int32),
                        pltpu.VMEM((nq, tq, tq), jnp.int16),
                        pltpu.VMEM((nq, tq, tq), jnp.int16),
                        pltpu.VMEM((nq, tq, tq), jnp.float32),
                        pltpu.VMEM((ATT_HEADS // 2, LANES, 2 * tq), MXU_DTYPE),
                        pltpu.VMEM((IDX_HEADS // 2, LANES, 2 * tq), MXU_DTYPE),
                        pltpu.VMEM((2, ATT_HEADS, tq, tq), jnp.float32),
                        pltpu.VMEM((2, ATT_HEADS, tq, tq), jnp.float32),
                        pltpu.VMEM((2, ATT_HEADS // 2, tq, 2 * tq), MXU_DTYPE),
                        pltpu.VMEM((ATT_HEADS, ATT_HEAD_DIM, tq), jnp.float32),
                        pltpu.VMEM((ATT_HEADS, tq), jnp.float32),
                        pltpu.VMEM((ATT_HEADS, tq), jnp.float32)],
        compiler_params=pltpu.CompilerParams(
            dimension_semantics=("parallel", "arbitrary"), vmem_limit_bytes=VMEM_LIMIT_BYTES),
    )(oa, oa, v_t, oa, okk, osm, bias_t)


def _mlstm_kernel(qk_ref, v_ref, og_ref, s_ref, cw_ref, cb_ref, gb_ref, ng_ref, y_ref,
                  xbuf, cst, mst, qk_s, in_s, p_s, kw_s, *, L):
    c = pl.program_id(1)
    H, DK, DV = MLSTM_HEADS, MLSTM_QK_DIM, MLSTM_V_DIM
    tail = SUBLANES

    @pl.when(c == 0)
    def _():
        xbuf[0:tail, :] = jnp.zeros((tail, 2 * QK_M), jnp.float32)
        cst[...] = jnp.zeros(cst.shape, jnp.float32)
        mst[...] = jnp.zeros(mst.shape, jnp.float32)

    @pl.when(c > 0)
    def _():
        xbuf[0:tail, :] = xbuf[L:L + tail, :]

    xbuf[tail:tail + L, :] = qk_ref[...]
    conv = cb_ref[...] + jnp.zeros((L, 2 * QK_M), jnp.float32)
    for j in range(CONV_WIDTH):
        conv = conv + cw_ref[j:j + 1, :] * xbuf[pl.ds(tail - (CONV_WIDTH - 1) + j, L), :]
    qk = conv * jax.nn.sigmoid(conv)
    assert 2 * DK == LANES and DV == LANES
    qT = qk[:, :QK_M].T
    kb = (qk[:, QK_M:] * (DK ** -0.5)).astype(MXU_DTYPE)
    vT = v_ref[...].T

    g = s_ref[...] + gb_ref[...]
    logf = jnp.minimum(g, 0.0) - jnp.log(1.0 + jnp.exp(-jnp.abs(g)))
    row = lax.broadcasted_iota(jnp.int32, (L, L), 0)
    col = lax.broadcasted_iota(jnp.int32, (L, L), 1)
    bcum = jnp.dot(jnp.where(row >= col, 1.0, 0.0), logf, precision=lax.Precision.HIGHEST,
                   preferred_element_type=jnp.float32)
    gT = g.T
    bT = bcum.T
    visible = row <= col
    top = lax.broadcasted_iota(jnp.int32, (LANES, L), 0) < DK
    one_row = jnp.where(lax.broadcasted_iota(jnp.int32, (LANES, L), 0) == 0, 1.0, 0.0)

    for h in range(H):
        pair = slice((h // 2) * LANES, (h // 2 + 1) * LANES)
        qp = qT[pair, :]
        qz = jnp.where(top, qp, 0.0) if h % 2 == 0 else jnp.where(top, 0.0, qp)
        qz = qz.astype(MXU_DTYPE)
        qk_s[h] = jnp.dot(kb[:, pair], qz, preferred_element_type=jnp.float32)
        in_s[h] = jnp.dot(cst[h].astype(MXU_DTYPE), qz, preferred_element_type=jnp.float32)

    stats = []
    for h in range(H):
        acol = g[:, S_IM + h:S_IM + h + 1] - bcum[:, S_FM + h:S_FM + h + 1]
        br = bT[S_FM + h:S_FM + h + 1, :]
        lir = gT[S_IM + h:S_IM + h + 1, :]
        m0 = mst[h:h + 1, 0:1]
        d = jnp.where(visible, br + acol, -jnp.inf)
        m_inter = br + m0
        m_t = jnp.maximum(m_inter, jnp.max(d, axis=0, keepdims=True))
        p = qk_s[h] * jnp.exp(d - m_t)
        p_s[h] = p.astype(MXU_DTYPE)
        b_end = br[:, L - 1:L]
        a = b_end - br + lir
        m_new = jnp.maximum(b_end + m0, jnp.max(a, axis=1, keepdims=True))
        vaug = jnp.concatenate([vT[h * DV:(h + 1) * DV, :], one_row], axis=0)
        kw_s[h] = (vaug * jnp.exp(a - m_new)).astype(MXU_DTYPE)
        stats.append((jnp.sum(p, axis=0, keepdims=True), jnp.exp(m_inter - m_t), m_t,
                      jnp.exp(b_end + m0 - m_new), m_new))

    for h in range(H):
        den_intra, sc, m_t, decay, m_new = stats[h]
        pair = slice((h // 2) * LANES, (h // 2 + 1) * LANES)
        inter = in_s[h]
        num = jnp.dot(vT[h * DV:(h + 1) * DV, :].astype(MXU_DTYPE), p_s[h],
                      preferred_element_type=jnp.float32) + sc * inter[:DV, :]
        den = den_intra + sc * inter[DV:DV + 1, :]
        hh = num / jnp.maximum(jnp.abs(den), jnp.exp(-m_t))
        cst[h] = decay * cst[h] + jnp.dot(kw_s[h], kb[:, pair], preferred_element_type=jnp.float32)
        mst[h:h + 1, :] = jnp.broadcast_to(m_new, (1, LANES))
        mu = jnp.mean(hh, axis=0, keepdims=True)
        hc = hh - mu
        var = jnp.mean(hc * hc, axis=0, keepdims=True)
        hn = (hc * lax.rsqrt(var + LN_EPS)).T * ng_ref[:, h * DV:(h + 1) * DV]
        y_ref[:, h * DV:(h + 1) * DV] = (hn * jax.nn.sigmoid(og_ref[:, h * DV:(h + 1) * DV])).astype(y_ref.dtype)


def _mlstm(om, osm, conv_w, conv_b, gate_bias, norm_g, B, S, L):
    nc = S // L
    W = MLSTM_WIDTH
    assert 2 * QK_M == W
    return pl.pallas_call(
        functools.partial(_mlstm_kernel, L=L),
        grid=(B, nc),
        in_specs=[pl.BlockSpec((L, W), lambda b, c: (b * nc + c, 0)),
                  pl.BlockSpec((L, W), lambda b, c: (b * nc + c, 1)),
                  pl.BlockSpec((L, W), lambda b, c: (b * nc + c, 2)),
                  pl.BlockSpec((L, COLS_S), lambda b, c: (b * nc + c, 0)),
                  pl.BlockSpec((CONV_WIDTH, W), lambda b, c: (0, 0)),
                  pl.BlockSpec((1, W), lambda b, c: (0, 0)),
                  pl.BlockSpec((1, COLS_S), lambda b, c: (0, 0)),
                  pl.BlockSpec((1, W), lambda b, c: (0, 0))],
        out_specs=pl.BlockSpec((L, W), lambda b, c: (b * nc + c, 0)),
        out_shape=jax.ShapeDtypeStruct((B * S, W), MXU_DTYPE),
        scratch_shapes=[pltpu.VMEM((L + 2 * SUBLANES, W), jnp.float32),
                        pltpu.VMEM((MLSTM_HEADS, 2 * MLSTM_V_DIM, LANES), jnp.float32),
                        pltpu.VMEM((SUBLANES, LANES), jnp.float32),
                        pltpu.VMEM((MLSTM_HEADS, L, L), jnp.float32),
                        pltpu.VMEM((MLSTM_HEADS, 2 * MLSTM_V_DIM, L), jnp.float32),
                        pltpu.VMEM((MLSTM_HEADS, L, L), MXU_DTYPE),
                        pltpu.VMEM((MLSTM_HEADS, 2 * MLSTM_V_DIM, L), MXU_DTYPE)],
        compiler_params=_cparams(("parallel", "arbitrary")),
    )(om, om, om, osm, conv_w, conv_b, gate_bias, norm_g)


def _layer_norm(z, g, b):
    mu = jnp.mean(z, axis=1, keepdims=True)
    zc = z - mu
    var = jnp.mean(zc * zc, axis=1, keepdims=True)
    return zc * lax.rsqrt(var + LN_EPS) * g + b


def _merge_kernel(ya_ref, ym_ref, g_ref, x_ref, wa_ref, wm_ref, wo_ref, lg_ref, lb_ref,
                  wr_ref, br_ref, x1_ref, xp_ref, idx_ref, gate_ref, cnt_ref, carry_ref, *, d_model):
    D = d_model
    mix = (jax.nn.sigmoid(g_ref[:, :D]) * jnp.dot(ya_ref[...], wa_ref[...], preferred_element_type=jnp.float32)
           + jax.nn.sigmoid(g_ref[:, D:]) * jnp.dot(ym_ref[...], wm_ref[...], preferred_element_type=jnp.float32))
    y = jnp.dot(mix.astype(MXU_DTYPE), wo_ref[...], preferred_element_type=jnp.float32)
    x1 = _layer_norm(DEEPNORM_ALPHA * x_ref[...] + y, lg_ref[...], lb_ref[...])
    x1_ref[...] = x1
    xb = x1.astype(MXU_DTYPE)
    bits = pltpu.bitcast(x1.astype(jnp.bfloat16).astype(jnp.float32), jnp.uint32)
    xp_ref[...] = (bits[:, :D // 2] & jnp.uint32(0xFFFF0000)) | (bits[:, D // 2:] >> 16)

    logits = jnp.dot(xb, wr_ref[...], preferred_element_type=jnp.float32) + br_ref[...]
    tm = logits.shape[0]
    lane = lax.broadcasted_iota(jnp.int32, (tm, LANES), 1)
    lane_f = lane.astype(jnp.float32)
    vals, idxs = [], []
    for _ in range(TOP_K):
        mx = jnp.max(logits, axis=1, keepdims=True)
        ix = jnp.min(jnp.where(logits == mx, lane_f, float(LANES)), axis=1, keepdims=True)
        vals.append(mx)
        idxs.append(ix)
        logits = jnp.where(lane_f == ix, -jnp.inf, logits)
    es = [jnp.exp(v - vals[0]) for v in vals]
    tot = es[0]
    for e in es[1:]:
        tot = tot + e
    @pl.when(pl.program_id(0) == 0)
    def _():
        carry_ref[...] = jnp.zeros(carry_ref.shape, jnp.float32)

    hots = [lane_f == ix for ix in idxs]
    c = jnp.zeros((tm, LANES), jnp.float32)
    for hot in hots:
        c = c + jnp.where(hot, 1.0, 0.0)
    before = jnp.where(lax.broadcasted_iota(jnp.int32, (tm, tm), 0)
                       > lax.broadcasted_iota(jnp.int32, (tm, tm), 1), 1.0, 0.0).astype(MXU_DTYPE)
    carry = carry_ref[0:1, :]
    prior = jnp.dot(before, c.astype(MXU_DTYPE), preferred_element_type=jnp.float32) + carry
    total = carry + jnp.sum(c, axis=0, keepdims=True)
    carry_ref[...] = jnp.broadcast_to(total, carry_ref.shape)
    cnt_ref[...] = jnp.broadcast_to(total, cnt_ref.shape)

    idx_out = jnp.zeros((tm, LANES), jnp.float32)
    gate_out = jnp.zeros((tm, LANES), jnp.float32)
    for k in range(TOP_K):
        idx_out = jnp.where(lane == k, idxs[k], idx_out)
        rank_k = jnp.sum(jnp.where(hots[k], prior, 0.0), axis=1, keepdims=True)
        idx_out = jnp.where(lane == TOP_K + k, rank_k, idx_out)
        gate_out = jnp.where(lane == k, es[k] / tot, gate_out)
    idx_ref[...] = idx_out.T[:2 * TOP_K, :].astype(jnp.int32)
    gate_ref[...] = gate_out


def _merge(y_att, y_m, og, x2, wa, wm, wo, ln_g, ln_b, wr, br, tm):
    T, D = x2.shape
    full = lambda shape: pl.BlockSpec(shape, lambda i: (0, 0))
    return pl.pallas_call(
        functools.partial(_merge_kernel, d_model=D),
        grid=(T // tm,),
        in_specs=[pl.BlockSpec((tm, ATT_WIDTH), lambda i: (i, 0)),
                  pl.BlockSpec((tm, MLSTM_WIDTH), lambda i: (i, 0)),
                  pl.BlockSpec((tm, 2 * D), lambda i: (i, 0)),
                  pl.BlockSpec((tm, D), lambda i: (i, 0)),
                  full(wa.shape), full(wm.shape), full(wo.shape),
                  full((1, D)), full((1, D)), full(wr.shape), full((1, LANES))],
        out_specs=[pl.BlockSpec((tm, D), lambda i: (i, 0)),
                   pl.BlockSpec((tm, D // 2), lambda i: (i, 0)),
                   pl.BlockSpec((2 * TOP_K, tm), lambda i: (0, i)),
                   pl.BlockSpec((tm, LANES), lambda i: (i, 0)),
                   pl.BlockSpec((SUBLANES, LANES), lambda i: (0, 0))],
        out_shape=[jax.ShapeDtypeStruct((T, D), jnp.float32),
                   jax.ShapeDtypeStruct((T, D // 2), jnp.uint32),
                   jax.ShapeDtypeStruct((2 * TOP_K, T), jnp.int32),
                   jax.ShapeDtypeStruct((T, LANES), jnp.float32),
                   jax.ShapeDtypeStruct((SUBLANES, LANES), jnp.float32)],
        scratch_shapes=[pltpu.VMEM((SUBLANES, LANES), jnp.float32)],
        compiler_params=_cparams(("arbitrary",)),
    )(y_att, y_m, og, x2, wa, wm, wo, ln_g, ln_b, wr, br)


def _dispatch_sc(dest, xp, cap):
    T, W = xp.shape
    n_chunks = T // SC_CHUNK
    n_workers = SC_CORES * SC_SUBCORES
    assert T % SC_CHUNK == 0 and n_chunks % n_workers == 0
    per_worker = n_chunks // n_workers
    idx = dest.reshape(TOP_K, n_chunks, SC_CHUNK).transpose(1, 0, 2)
    mesh = plsc.VectorSubcoreMesh(core_axis_name="c", subcore_axis_name="s")

    @functools.partial(
        pl.kernel, mesh=mesh,
        out_type=jax.ShapeDtypeStruct((cap, W), xp.dtype),
        scratch_types=[pltpu.VMEM((SC_CHUNK, W), xp.dtype),
                       pltpu.VMEM((TOP_K, SC_CHUNK), jnp.int32)])
    def scatter_rows(x_hbm, idx_hbm, xs_hbm, rows_v, idx_v):
        worker = lax.axis_index("s") * SC_CORES + lax.axis_index("c")

        def body(j, carry):
            c = worker * per_worker + j
            pltpu.sync_copy(x_hbm.at[pl.ds(c * SC_CHUNK, SC_CHUNK)], rows_v)
            pltpu.sync_copy(idx_hbm.at[c], idx_v)
            for k in range(TOP_K):
                pltpu.sync_copy(rows_v, xs_hbm.at[idx_v.at[k]])
            return carry

        lax.fori_loop(0, per_worker, body, 0)

    return scatter_rows(xp, idx)


def _ffn_kernel(be_ref, nu_ref, nv_ref, xs_ref, wgu_ref, bgu_ref, wd_ref, bd_ref, y_ref, wgu_b,
                wd_b, act_s, *, d_ff):
    r = pl.program_id(0)
    e = be_ref[r]
    prev = be_ref[jnp.maximum(r - 1, 0)]

    @pl.when((r == 0) | (e != prev))
    def _():
        wgu_b[...] = wgu_ref[0].astype(MXU_DTYPE)
        wd_b[...] = wd_ref[0].astype(MXU_DTYPE)

    @pl.when(r < nu_ref[0])
    def _():
        live = lax.broadcasted_iota(jnp.int32, (xs_ref.shape[0], 1), 0) < nv_ref[r]
        w = jnp.where(live, xs_ref[...], jnp.uint32(0))
        half = w.shape[1]
        x_hi = pltpu.bitcast(w & jnp.uint32(0xFFFF0000), jnp.float32).astype(MXU_DTYPE)
        x_lo = pltpu.bitcast(w << 16, jnp.float32).astype(MXU_DTYPE)
        x = jnp.concatenate([x_hi, x_lo], axis=1)
        step = 512
        for j in range(0, d_ff, step):
            def gu(lo):
                return (jnp.dot(x, wgu_b[:, lo:lo + step], preferred_element_type=jnp.float32)
                        + bgu_ref[0, :, lo:lo + step])
            gate = jnp.minimum(gu(j), SWIGLU_LIMIT)
            up = jnp.clip(gu(d_ff + j), -SWIGLU_LIMIT, SWIGLU_LIMIT)
            act = (up + 1.0) * (gate * jax.nn.sigmoid(SWIGLU_ALPHA * gate))
            act_s[:, j:j + step] = act.astype(MXU_DTYPE)
        y = jnp.dot(act_s[...], wd_b[...], preferred_element_type=jnp.float32) + bd_ref[0]
        bits = pltpu.bitcast(y.astype(jnp.bfloat16).astype(jnp.float32), jnp.uint32)
        y_ref[...] = (bits[:, :half] & jnp.uint32(0xFFFF0000)) | (bits[:, half:] >> 16)

    @pl.when(r >= nu_ref[0])
    def _():
        y_ref[...] = jnp.zeros(y_ref.shape, y_ref.dtype)


def _expert_ffn(block_expert, n_used, n_valid, xs, w_gate_up, b_gate_up, w_down, b_down, bm):
    cap, half = xs.shape
    E, D, F2 = w_gate_up.shape
    d_ff = F2 // 2
    grid_spec = pltpu.PrefetchScalarGridSpec(
        num_scalar_prefetch=3,
        grid=(cap // bm,),
        in_specs=[pl.BlockSpec((bm, half), lambda r, be, nu, nv: (jnp.minimum(r, nu[0] - 1), 0)),
                  pl.BlockSpec((1, D, F2), lambda r, be, nu, nv: (be[r], 0, 0)),
                  pl.BlockSpec((1, 1, F2), lambda r, be, nu, nv: (be[r], 0, 0)),
                  pl.BlockSpec((1, d_ff, D), lambda r, be, nu, nv: (be[r], 0, 0)),
                  pl.BlockSpec((1, 1, D), lambda r, be, nu, nv: (be[r], 0, 0))],
        out_specs=pl.BlockSpec((bm, half), lambda r, be, nu, nv: (r, 0)),
        scratch_shapes=[pltpu.VMEM((D, F2), MXU_DTYPE),
                        pltpu.VMEM((d_ff, D), MXU_DTYPE),
                        pltpu.VMEM((bm, d_ff), MXU_DTYPE)],
    )
    return pl.pallas_call(
        functools.partial(_ffn_kernel, d_ff=d_ff),
        grid_spec=grid_spec,
        out_shape=jax.ShapeDtypeStruct((cap, half), jnp.uint32),
        compiler_params=_cparams(("arbitrary",)),
    )(block_expert, n_used, n_valid, xs, w_gate_up, b_gate_up.reshape(E, 1, F2), w_down,
      b_down.reshape(E, 1, D))


def _gather_sc(dest, ybuf):
    cap, D = ybuf.shape
    T = dest.shape[1]
    chunk = SC_CHUNK
    n_chunks = T // chunk
    n_workers = SC_CORES * SC_SUBCORES
    assert T % chunk == 0 and n_chunks % n_workers == 0
    per_worker = n_chunks // n_workers
    idx = dest.reshape(TOP_K, n_chunks, chunk).transpose(1, 0, 2)
    mesh = plsc.VectorSubcoreMesh(core_axis_name="c", subcore_axis_name="s")

    @functools.partial(
        pl.kernel, mesh=mesh,
        out_type=jax.ShapeDtypeStruct((TOP_K, T, D), ybuf.dtype),
        scratch_types=[pltpu.VMEM((chunk, D), ybuf.dtype),
                       pltpu.VMEM((TOP_K, chunk), jnp.int32)])
    def gather_rows(y_hbm, idx_hbm, out_hbm, rows_v, idx_v):
        worker = lax.axis_index("s") * SC_CORES + lax.axis_index("c")

        def body(j, carry):
            c = worker * per_worker + j
            pltpu.sync_copy(idx_hbm.at[c], idx_v)
            for k in range(TOP_K):
                pltpu.sync_copy(y_hbm.at[idx_v.at[k]], rows_v)
                pltpu.sync_copy(rows_v, out_hbm.at[k, pl.ds(c * chunk, chunk)])
            return carry

        lax.fori_loop(0, per_worker, body, 0)

    return gather_rows(ybuf, idx)


def _combine_dense_kernel(yk_ref, gate_ref, x1_ref, lg_ref, lb_ref, *rest):
    o_ref = rest[-1]
    left = right = None
    for k in range(TOP_K):
        w = yk_ref[k]
        g = gate_ref[:, k:k + 1]
        hi = g * pltpu.bitcast(w & jnp.uint32(0xFFFF0000), jnp.float32)
        lo = g * pltpu.bitcast(w << 16, jnp.float32)
        left = hi if left is None else left + hi
        right = lo if right is None else right + lo
    y = jnp.concatenate([left, right], axis=1)
    o_ref[...] = _layer_norm(DEEPNORM_ALPHA * x1_ref[...] + y, lg_ref[...], lb_ref[...])


def _combine_dense(yk, first_tile, gates, x1, ln_g, ln_b, prev, tm):
    T, D = x1.shape
    n = yk.shape[1] // tm
    in_specs = [pl.BlockSpec((TOP_K, tm, D // 2), lambda i: (0, i, 0)),
                pl.BlockSpec((tm, LANES), lambda i: (i + first_tile, 0)),
                pl.BlockSpec((tm, D), lambda i: (i + first_tile, 0)),
                pl.BlockSpec((1, D), lambda i: (0, 0)),
                pl.BlockSpec((1, D), lambda i: (0, 0))]
    args = [yk, gates, x1, ln_g, ln_b]
    aliases = {}
    if prev is not None:
        in_specs.append(pl.BlockSpec(memory_space=pl.ANY))
        args.append(prev)
        aliases = {len(args) - 1: 0}
    return pl.pallas_call(
        _combine_dense_kernel,
        grid=(n,),
        in_specs=in_specs,
        out_specs=pl.BlockSpec((tm, D), lambda i: (i + first_tile, 0)),
        out_shape=jax.ShapeDtypeStruct((T, D), jnp.float32),
        input_output_aliases=aliases,
        compiler_params=_cparams(("parallel",)),
    )(*args)


def _tile(n, pref):
    t = min(n, pref)
    assert n % t == 0
    return t


def _relayout_w_in(w_in, d_model):
    sizes = (ATT_WIDTH, ATT_WIDTH, ATT_WIDTH, IDX_HEADS * IDX_HEAD_DIM, IDX_HEAD_DIM, IDX_HEADS,
             QK_M, QK_M, MLSTM_WIDTH, MLSTM_HEADS, MLSTM_HEADS, MLSTM_WIDTH, d_model, d_model)
    offs = [0]
    for s in sizes:
        offs.append(offs[-1] + s)
    seg = lambda k: w_in[:, offs[k]:offs[k + 1]]
    (q_a, k_a, v_a, q_i, k_i, w_i, q_m, k_m, v_m, i_m, f_m, o_m, g_a, g_m) = [seg(k) for k in range(14)]
    pad = jnp.zeros((w_in.shape[0], COLS_S - (IDX_HEAD_DIM + IDX_HEADS + 2 * MLSTM_HEADS)), w_in.dtype)
    cols = [q_a, k_a, q_i, k_i, w_i, i_m, f_m, pad, q_m, k_m, v_m, o_m, g_a, g_m, k_i, k_i, v_a]
    return jnp.concatenate(cols, axis=1).astype(MXU_DTYPE)


def _layer(x2, B, S, w_in, conv_w, conv_b, i_bias, f_bias, norm_g, w_branch_attn, w_branch_mlstm,
           w_out, ln1_g, ln1_b, w_router, b_router, w_gate_up, b_gate_up, w_down, b_down,
           ln2_g, ln2_b, rel_bias):
    T, D = x2.shape
    bf = MXU_DTYPE
    tq = _tile(S, 256)
    L = _tile(S, 256)
    tm = _tile(T, 256)

    oa, osm, om, og, okk, v_t = _project(x2, _relayout_w_in(w_in, D), _tile(T, 512), tq)
    bias_t = _bias_tiles(rel_bias, tq)
    y_att = _dsa_attention(oa, osm, okk, v_t, bias_t, B, S, tq)

    gate_bias = jnp.zeros((1, COLS_S), jnp.float32)
    gate_bias = gate_bias.at[0, S_IM:S_IM + MLSTM_HEADS].set(i_bias)
    gate_bias = gate_bias.at[0, S_FM:S_FM + MLSTM_HEADS].set(f_bias)
    y_m = _mlstm(om, osm, conv_w, conv_b.reshape(1, -1), gate_bias, norm_g.reshape(1, -1), B, S, L)

    wr = jnp.zeros((D, LANES), bf).at[:, :N_EXPERTS].set(w_router.astype(bf))
    br = jnp.full((1, LANES), NEG_BIG, jnp.float32).at[0, :N_EXPERTS].set(b_router)
    x1, xp, idx, gates, cnt = _merge(y_att, y_m, og, x2, w_branch_attn.astype(bf),
                                     w_branch_mlstm.astype(bf), w_out.astype(bf), ln1_g.reshape(1, D),
                                     ln1_b.reshape(1, D), wr, br, _tile(T, 512))

    bm = 1024
    counts = cnt[0, :N_EXPERTS].astype(jnp.int32)
    padded = ((counts + bm - 1) // bm) * bm
    pend = jnp.cumsum(padded)
    pstart = pend - padded
    cap = ((T * TOP_K + bm - 1) // bm) * bm + N_EXPERTS * bm
    n_blocks = cap // bm
    experts = jnp.arange(N_EXPERTS, dtype=jnp.int32)[:, None, None]
    dest = jnp.sum(jnp.where(idx[None, :TOP_K, :] == experts, pstart[:, None, None], 0), axis=0) \
        + idx[TOP_K:, :]
    block_row = jnp.arange(n_blocks, dtype=jnp.int32) * bm
    block_expert = jnp.minimum(jnp.sum((pend[None, :] <= block_row[:, None]).astype(jnp.int32), axis=1),
                               N_EXPERTS - 1)
    n_used = (pend[-1:] // bm).astype(jnp.int32)
    n_valid = jnp.clip((pstart + counts)[block_expert] - block_row, 0, bm).astype(jnp.int32)

    xs = _dispatch_sc(dest, xp, cap)
    ybuf = _expert_ffn(block_expert, n_used, n_valid, xs, w_gate_up, b_gate_up, w_down, b_down, bm)
    quantum = SC_CORES * SC_SUBCORES * SC_CHUNK
    parts = (1, 1, 2, 4) if T % (8 * quantum) == 0 else (1,)
    bounds = [0]
    for p in parts:
        bounds.append(bounds[-1] + p * T // sum(parts))
    slabs = [_gather_sc(dest[:, lo:hi], ybuf) for lo, hi in zip(bounds[:-1], bounds[1:])]
    out = None
    tc = _tile(T // sum(parts), 1024)
    for lo, yk in zip(bounds[:-1], slabs):
        out = _combine_dense(yk, lo // tc, gates, x1, ln2_g.reshape(1, D), ln2_b.reshape(1, D),
                             out, tc)
    return out


def kernel(x, w_in, conv_w, conv_b, mlstm_i_bias, mlstm_f_bias, mlstm_norm_g, w_branch_attn,
           w_branch_mlstm, w_out, ln1_g, ln1_b, w_router, b_router, w_gate_up, b_gate_up,
           w_down, b_down, ln2_g, ln2_b, rel_bias):
    B, S, D = x.shape
    x2 = x.reshape(B * S, D)
    for l in range(w_in.shape[0]):
        x2 = _layer(x2, B, S, w_in[l], conv_w[l], conv_b[l], mlstm_i_bias[l], mlstm_f_bias[l],
                    mlstm_norm_g[l], w_branch_attn[l], w_branch_mlstm[l], w_out[l], ln1_g[l], ln1_b[l],
                    w_router[l], b_router[l], w_gate_up[l], b_gate_up[l], w_down[l], b_down[l],
                    ln2_g[l], ln2_b[l], rel_bias)
    return x2.reshape(B, S, D)
```

```python
import functools
import math

import jax
import jax.numpy as jnp
from jax import lax
from jax.experimental import pallas as pl
from jax.experimental.pallas import tpu as pltpu
from jax.experimental.pallas import tpu_sc as plsc

ATT_HEADS = 8
ATT_HEAD_DIM = 64
ATT_WIDTH = ATT_HEADS * ATT_HEAD_DIM
IDX_HEADS = 8
IDX_HEAD_DIM = 64
TOPK_MAX = 256
MLSTM_HEADS = 4
MLSTM_QK_DIM = 64
MLSTM_V_DIM = 128
MLSTM_WIDTH = MLSTM_HEADS * MLSTM_V_DIM
CONV_WIDTH = 4
N_BUCKETS = 32
MAX_DISTANCE = 128
N_EXPERTS = 32
TOP_K = 4
SWIGLU_ALPHA = 1.702
SWIGLU_LIMIT = 7.0
LN_EPS = 1e-5
DEPTH = 1
DEEPNORM_ALPHA = (2 * DEPTH) ** 0.25

LANES = 128
SUBLANES = 8
VMEM_LIMIT_BYTES = 56 * 1024 * 1024
SC_CORES = 2
SC_SUBCORES = 16
SC_CHUNK = 128

MXU_DTYPE = jnp.bfloat16

INT_MIN = -(2 ** 31)
NEG_BIG = -1e30

QK_M = MLSTM_HEADS * MLSTM_QK_DIM
COLS_A = 2 * ATT_WIDTH + IDX_HEADS * IDX_HEAD_DIM
COLS_S = LANES
COLS_M = 2 * QK_M + 2 * MLSTM_WIDTH
COLS_K = 2 * IDX_HEAD_DIM
S_WI = IDX_HEAD_DIM
S_IM = S_WI + IDX_HEADS
S_FM = S_IM + MLSTM_HEADS


def _cparams(sem):
    return pltpu.CompilerParams(dimension_semantics=sem, vmem_limit_bytes=VMEM_LIMIT_BYTES)


def _proj_kernel(x_ref, w_ref, oa_ref, os_ref, om_ref, og_ref, ok_ref, ovt_ref, *, d_model):
    xb = x_ref[...].astype(MXU_DTYPE)
    step = 512

    def mm(lo, hi):
        return jnp.dot(xb, w_ref[:, lo:hi], preferred_element_type=jnp.float32)

    base = 0
    for j in range(0, COLS_A, step):
        oa_ref[:, j:j + step] = mm(base + j, base + j + step).astype(MXU_DTYPE)
    base += COLS_A
    os_ref[...] = mm(base, base + COLS_S)
    base += COLS_S
    for j in range(0, COLS_M, step):
        om_ref[:, j:j + step] = mm(base + j, base + j + step)
    base += COLS_M
    for j in range(0, 2 * d_model, step):
        og_ref[:, j:j + step] = mm(base + j, base + j + step)
    base += 2 * d_model
    ok_ref[...] = mm(base, base + COLS_K).astype(MXU_DTYPE)
    base += COLS_K
    v = mm(base, base + ATT_WIDTH)
    tq = ovt_ref.shape[2]
    for s in range(ovt_ref.shape[0]):
        ovt_ref[s] = v[s * tq:(s + 1) * tq, :].T.astype(MXU_DTYPE)


def _project(x2, w_p, tm, tq):
    T, D = x2.shape
    n_all = w_p.shape[1]
    assert tm % tq == 0
    return pl.pallas_call(
        functools.partial(_proj_kernel, d_model=D),
        grid=(T // tm,),
        in_specs=[pl.BlockSpec((tm, D), lambda i: (i, 0)),
                  pl.BlockSpec((D, n_all), lambda i: (0, 0))],
        out_specs=[pl.BlockSpec((tm, COLS_A), lambda i: (i, 0)),
                   pl.BlockSpec((tm, COLS_S), lambda i: (i, 0)),
                   pl.BlockSpec((tm, COLS_M), lambda i: (i, 0)),
                   pl.BlockSpec((tm, 2 * D), lambda i: (i, 0)),
                   pl.BlockSpec((tm, COLS_K), lambda i: (i, 0)),
                   pl.BlockSpec((tm // tq, ATT_WIDTH, tq), lambda i: (i, 0, 0))],
        out_shape=[jax.ShapeDtypeStruct((T, COLS_A), MXU_DTYPE),
                   jax.ShapeDtypeStruct((T, COLS_S), jnp.float32),
                   jax.ShapeDtypeStruct((T, COLS_M), jnp.float32),
                   jax.ShapeDtypeStruct((T, 2 * D), jnp.float32),
                   jax.ShapeDtypeStruct((T, COLS_K), MXU_DTYPE),
                   jax.ShapeDtypeStruct((T // tq, ATT_WIDTH, tq), MXU_DTYPE)],
        compiler_params=_cparams(("parallel",)),
    )(x2, w_p)


def _bias_kernel(rb_ref, o_ref, *, tq):
    h = pl.program_id(0)
    s = lax.broadcasted_iota(jnp.int32, (tq, tq), 0)
    t = lax.broadcasted_iota(jnp.int32, (tq, tq), 1)
    max_exact = N_BUCKETS // 2
    for d in range(3):
        n = jnp.maximum(t - s + d * tq, 0)
        n_f = jnp.maximum(n, 1).astype(jnp.float32)
        large = max_exact + (jnp.log(n_f / max_exact) / math.log(MAX_DISTANCE / max_exact)
                             * (N_BUCKETS - max_exact)).astype(jnp.int32)
        large = jnp.minimum(large, N_BUCKETS - 1)
        bucket = jnp.where(n < max_exact, n, large)
        acc = jnp.zeros((tq, tq), jnp.float32)
        for k in range(N_BUCKETS):
            acc = jnp.where(bucket == k, rb_ref[h, k], acc)
        o_ref[0, d] = acc


def _bias_tiles(rel_bias, tq):
    assert tq + 1 >= MAX_DISTANCE
    H = rel_bias.shape[0]
    return pl.pallas_call(
        functools.partial(_bias_kernel, tq=tq),
        grid=(H,),
        in_specs=[pl.BlockSpec(memory_space=pltpu.SMEM)],
        out_specs=pl.BlockSpec((1, 3, tq, tq), lambda h: (h, 0, 0, 0)),
        out_shape=jax.ShapeDtypeStruct((H, 3, tq, tq), jnp.float32),
        compiler_params=_cparams(("parallel",)),
    )(rel_bias)


def _dsa_kernel(qa_ref, ka_ref, vt_ref, qi_ref, kk_ref, wq_ref, bias_ref, o_ref,
                keys_ref, hi_ref, lo_ref, msk_ref, qz_ref, qiz_ref, s_ref, sb_ref, p_ref, acc_ref, m_ref,
                l_ref,
                *, tq, n_sel):
    i = pl.program_id(1)
    nch = i + 1
    t_pos = i * tq + lax.broadcasted_iota(jnp.int32, (1, tq), 1)
    s_loc = lax.broadcasted_iota(jnp.int32, (tq, 1), 0)
    hd = ATT_HEAD_DIM
    n_pairs = ATT_HEADS // 2
    assert 2 * hd == LANES and IDX_HEAD_DIM == hd and IDX_HEADS == ATT_HEADS

    top = lax.broadcasted_iota(jnp.int32, (LANES, tq), 0) < hd

    def pair_operand(blk):
        bt = blk.astype(jnp.float32).T
        return jnp.concatenate([jnp.where(top, bt, 0.0), jnp.where(top, 0.0, bt)],
                               axis=1).astype(MXU_DTYPE)

    for p in range(n_pairs):
        pair = slice(p * LANES, (p + 1) * LANES)
        qz_ref[p] = pair_operand(qa_ref[:, pair] * (hd ** -0.5))
        qiz_ref[p] = pair_operand(qi_ref[:, pair])

    ws = wq_ref[...].T[S_WI:S_WI + IDX_HEADS, :] * (IDX_HEADS ** -0.5)

    def score_tile(c):
        off = pl.multiple_of(c * tq, tq)
        kk = kk_ref[pl.ds(off, tq), :]
        sc = jnp.zeros((tq, tq), jnp.float32)
        for p in range(n_pairs):
            d = jnp.dot(kk, qiz_ref[p], preferred_element_type=jnp.float32)
            for j in range(2):
                h = 2 * p + j
                sc = sc + ws[h:h + 1, :] * jnp.maximum(d[:, j * tq:(j + 1) * tq], 0.0)
        sc = sc + 0.0
        b = pltpu.bitcast(sc, jnp.int32)
        sk = b ^ ((b >> 31) & jnp.int32(0x7FFFFFFF))
        sk = jnp.where(off + s_loc <= t_pos, sk, jnp.int32(INT_MIN))
        keys_ref[c] = sk
        hi_ref[c] = (sk >> 16).astype(jnp.int16)

    def score_pair(c2, carry):
        score_tile(2 * c2)
        score_tile(2 * c2 + 1)
        return carry

    def score_single(c, carry):
        score_tile(c)
        return carry

    lax.fori_loop(0, nch // 2, score_pair, 0)
    lax.fori_loop(2 * (nch // 2), nch, score_single, 0)

    pack = 2 * SUBLANES
    half_min = -(2 ** 15)

    def count16(ref, pred_fn):
        def tile_count(c):
            hit = jnp.where(pred_fn(ref[c]), jnp.int16(1), jnp.int16(0))
            parts = [hit[r * pack:(r + 1) * pack, :] for r in range(tq // pack)]
            while len(parts) > 1:
                parts = [a + b for a, b in zip(parts[::2], parts[1::2])]
            return parts[0]
        acc = lax.fori_loop(0, nch // 2,
                            lambda c2, a: a + (tile_count(2 * c2) + tile_count(2 * c2 + 1)),
                            jnp.zeros((pack, tq), jnp.int16))
        acc = lax.fori_loop(2 * (nch // 2), nch, lambda c, a: a + tile_count(c), acc)
        return jnp.sum(acc.astype(jnp.float32), axis=0, keepdims=True)

    def search16(ref, target):
        def bit_body(it, u):
            cand_u = u | lax.shift_left(jnp.int32(1), 15 - it)
            cand = (cand_u + half_min).astype(jnp.int16)
            return jnp.where(count16(ref, lambda x: x >= cand) >= target, cand_u, u)
        return lax.fori_loop(0, 16, bit_body, jnp.zeros((1, tq), jnp.int32))

    u_hi = search16(hi_ref, float(n_sel))
    thr_hi = (u_hi + half_min).astype(jnp.int16)
    need_lo = n_sel - count16(hi_ref, lambda x: x > thr_hi)

    def low_body(c, carry):
        lo = ((keys_ref[c] & 0xFFFF) + half_min).astype(jnp.int16)
        lo_ref[c] = jnp.where(hi_ref[c] == thr_hi, lo, jnp.int16(half_min))
        return carry

    lax.fori_loop(0, nch, low_body, 0)
    u_lo = search16(lo_ref, need_lo)
    thr_lo = (u_lo + half_min).astype(jnp.int16)
    thr = lax.shift_left(u_hi + half_min, 16) | u_lo
    need = jnp.where(thr == INT_MIN, 0.0, need_lo - count16(lo_ref, lambda x: x > thr_lo))
    n_eq = count16(lo_ref, lambda x: x == thr_lo)
    surplus = jnp.where((thr != INT_MIN) & (n_eq > need), 1.0, 0.0)
    has_surplus = jnp.max(surplus) > 0.0

    @pl.when(has_surplus)
    def _():
        tri = jnp.where(lax.broadcasted_iota(jnp.int32, (tq, tq), 0)
                        >= lax.broadcasted_iota(jnp.int32, (tq, tq), 1), 1.0, 0.0).astype(MXU_DTYPE)

        def mask_body(c, run):
            kc = keys_ref[c]
            eq = kc == thr
            eqf = jnp.where(eq, 1.0, 0.0)
            prefix = jnp.dot(tri, eqf.astype(MXU_DTYPE), preferred_element_type=jnp.float32) + run
            msk_ref[c] = jnp.where(kc > thr, 0.0,
                                   jnp.where(eq, jnp.where(prefix <= need, 0.0, NEG_BIG), NEG_BIG))
            return run + jnp.sum(eqf, axis=0, keepdims=True)

        lax.fori_loop(0, nch, mask_body, jnp.zeros((1, tq), jnp.float32))

    @pl.when(jnp.logical_not(has_surplus))
    def _():
        floor = jnp.maximum(thr, INT_MIN + 1)

        def mask_body(c, carry):
            msk_ref[c] = jnp.where(keys_ref[c] >= floor, 0.0, NEG_BIG)
            return carry

        lax.fori_loop(0, nch, mask_body, 0)

    m_ref[...] = jnp.full(m_ref.shape, NEG_BIG, jnp.float32)
    l_ref[...] = jnp.zeros(l_ref.shape, jnp.float32)
    acc_ref[...] = jnp.zeros(acc_ref.shape, jnp.float32)

    def logit_stage(cs, far, sr):
        tile_max = [None] * ATT_HEADS
        for n, c in enumerate(cs):
            off = pl.multiple_of(c * tq, tq)
            mk = msk_ref[c]
            for p in range(n_pairs):
                pair = slice(p * LANES, (p + 1) * LANES)
                s2 = jnp.dot(ka_ref[pl.ds(off, tq), pair], qz_ref[p],
                             preferred_element_type=jnp.float32)
                for j in range(2):
                    h = 2 * p + j
                    s = s2[:, j * tq:(j + 1) * tq] + mk
                    if not far:
                        s = s + bias_ref[h, i - c]
                    sr[n, h] = s
                    mx = jnp.max(s, axis=0, keepdims=True)
                    tile_max[h] = mx if tile_max[h] is None else jnp.maximum(tile_max[h], mx)
        return tile_max

    def value_stage(cs, far, sr, tile_max):
        alphas = []
        for h in range(ATT_HEADS):
            m_prev = m_ref[h:h + 1, :]
            if far:
                b_far = bias_ref[h, 2, 0:1, 0:1]
                m_new = jnp.maximum(m_prev, tile_max[h] + b_far)
                shift = m_new - b_far
            else:
                m_new = jnp.maximum(m_prev, tile_max[h])
                shift = m_new
            alpha = jnp.exp(m_prev - m_new)
            for n in range(len(cs)):
                pe = jnp.exp(sr[n, h] - shift)
                p_ref[n, h // 2, :, (h % 2) * tq:(h % 2 + 1) * tq] = pe.astype(MXU_DTYPE)
            m_ref[h:h + 1, :] = m_new
            alphas.append(alpha)
        ones_rows = jnp.ones((2 * SUBLANES, tq), MXU_DTYPE)
        for h in range(ATT_HEADS):
            pv = None
            for n, c in enumerate(cs):
                va = jnp.concatenate([vt_ref[c, h * hd:(h + 1) * hd, :], ones_rows], axis=0)
                d = jnp.dot(va, p_ref[n, h // 2, :, (h % 2) * tq:(h % 2 + 1) * tq],
                            preferred_element_type=jnp.float32)
                pv = d if pv is None else pv + d
            acc_ref[h] = alphas[h] * acc_ref[h] + pv[:hd]
            l_ref[h:h + 1, :] = alphas[h] * l_ref[h:h + 1, :] + pv[hd:hd + 1]

    def att_step(cs, far):
        value_stage(cs, far, s_ref, logit_stage(cs, far, s_ref))

    n_far = jnp.maximum(i - 1, 0)
    n_fp = n_far // 2

    tiles = lambda q: [2 * q, 2 * q + 1]

    @pl.when(n_fp > 0)
    def _():
        def two_pairs(m, max_a):
            max_b = logit_stage(tiles(2 * m + 1), True, sb_ref)
            value_stage(tiles(2 * m), True, s_ref, max_a)
            nxt_a = logit_stage(tiles(2 * m + 2), True, s_ref)
            value_stage(tiles(2 * m + 1), True, sb_ref, max_b)
            return nxt_a

        n_dbl = (n_fp - 1) // 2
        max_a = lax.fori_loop(0, n_dbl, two_pairs, logit_stage(tiles(0), True, s_ref))
        q = 2 * n_dbl

        @pl.when(n_fp - q == 2)
        def _():
            max_b = logit_stage(tiles(q + 1), True, sb_ref)
            value_stage(tiles(q), True, s_ref, max_a)
            value_stage(tiles(q + 1), True, sb_ref, max_b)

        @pl.when(n_fp - q == 1)
        def _():
            value_stage(tiles(q), True, s_ref, max_a)

    odd_far = n_far - 2 * n_fp

    @pl.when(odd_far == 1)
    def _():
        c = n_far - 1
        max_c = logit_stage([c], True, s_ref)
        max_a = logit_stage([i - 1], False, sb_ref)
        value_stage([c], True, s_ref, max_c)
        max_b = logit_stage([i], False, s_ref)
        value_stage([i - 1], False, sb_ref, max_a)
        value_stage([i], False, s_ref, max_b)

    @pl.when((i > 0) & (odd_far == 0))
    def _():
        max_a = logit_stage([i - 1], False, s_ref)
        max_b = logit_stage([i], False, sb_ref)
        value_stage([i - 1], False, s_ref, max_a)
        value_stage([i], False, sb_ref, max_b)

    @pl.when(i == 0)
    def _():
        att_step([i], False)
    y_t = jnp.concatenate([acc_ref[h] / l_ref[h:h + 1, :] for h in range(ATT_HEADS)], axis=0)
    o_ref[...] = y_t.T.astype(o_ref.dtype)


def _dsa_attention(oa, osm, okk, v_t, bias_t, B, S, tq):
    nq = S // tq
    n_sel = min(TOPK_MAX, S // 4)
    W = ATT_WIDTH
    assert v_t.shape == (B * nq, W, tq)
    return pl.pallas_call(
        functools.partial(_dsa_kernel, tq=tq, n_sel=n_sel),
        grid=(B, nq),
        in_specs=[pl.BlockSpec((tq, W), lambda b, i: (b * nq + i, 0)),
                  pl.BlockSpec((S, W), lambda b, i: (b, 1)),
                  pl.BlockSpec((nq, W, tq), lambda b, i: (b, 0, 0)),
                  pl.BlockSpec((tq, W), lambda b, i: (b * nq + i, 2)),
                  pl.BlockSpec((S, COLS_K), lambda b, i: (b, 0)),
                  pl.BlockSpec((tq, COLS_S), lambda b, i: (b * nq + i, 0)),
                  pl.BlockSpec((ATT_HEADS, 3, tq, tq), lambda b, i: (0, 0, 0, 0))],
        out_specs=pl.BlockSpec((tq, W), lambda b, i: (b * nq + i, 0)),
        out_shape=jax.ShapeDtypeStruct((B * S, W), MXU_DTYPE),
        scratch_shapes=[pltpu.VMEM((nq, tq, tq), jnp.int32),
                        pltpu.VMEM((nq, tq, tq), jnp.int16),
                        pltpu.VMEM((nq, tq, tq), jnp.int16),
                        pltpu.VMEM((nq, tq, tq), jnp.float32),
                        pltpu.VMEM((ATT_HEADS // 2, LANES, 2 * tq), MXU_DTYPE),
                        pltpu.VMEM((IDX_HEADS // 2, LANES, 2 * tq), MXU_DTYPE),
                        pltpu.VMEM((2, ATT_HEADS, tq, tq), jnp.float32),
                        pltpu.VMEM((2, ATT_HEADS, tq, tq), jnp.float32),
                        pltpu.VMEM((2, ATT_HEADS // 2, tq, 2 * tq), MXU_DTYPE),
                        pltpu.VMEM((ATT_HEADS, ATT_HEAD_DIM, tq), jnp.float32),
                        pltpu.VMEM((ATT_HEADS, tq), jnp.float32),
                        pltpu.VMEM((ATT_HEADS, tq), jnp.float32)],
        compiler_params=pltpu.CompilerParams(
            dimension_semantics=("parallel", "arbitrary"), vmem_limit_bytes=VMEM_LIMIT_BYTES),
    )(oa, oa, v_t, oa, okk, osm, bias_t)


def _mlstm_kernel(qk_ref, v_ref, og_ref, s_ref, cw_ref, cb_ref, gb_ref, ng_ref, y_ref,
                  xbuf, cst, mst, qk_s, in_s, p_s, kw_s, *, L):
    c = pl.program_id(1)
    H, DK, DV = MLSTM_HEADS, MLSTM_QK_DIM, MLSTM_V_DIM
    tail = SUBLANES

    @pl.when(c == 0)
    def _():
        xbuf[0:tail, :] = jnp.zeros((tail, 2 * QK_M), jnp.float32)
        cst[...] = jnp.zeros(cst.shape, jnp.float32)
        mst[...] = jnp.zeros(mst.shape, jnp.float32)

    @pl.when(c > 0)
    def _():
        xbuf[0:tail, :] = xbuf[L:L + tail, :]

    xbuf[tail:tail + L, :] = qk_ref[...]
    conv = cb_ref[...] + jnp.zeros((L, 2 * QK_M), jnp.float32)
    for j in range(CONV_WIDTH):
        conv = conv + cw_ref[j:j + 1, :] * xbuf[pl.ds(tail - (CONV_WIDTH - 1) + j, L), :]
    qk = conv * jax.nn.sigmoid(conv)
    assert 2 * DK == LANES and DV == LANES
    qT = qk[:, :QK_M].T
    kb = (qk[:, QK_M:] * (DK ** -0.5)).astype(MXU_DTYPE)
    vT = v_ref[...].T

    g = s_ref[...] + gb_ref[...]
    logf = jnp.minimum(g, 0.0) - jnp.log(1.0 + jnp.exp(-jnp.abs(g)))
    row = lax.broadcasted_iota(jnp.int32, (L, L), 0)
    col = lax.broadcasted_iota(jnp.int32, (L, L), 1)
    bcum = jnp.dot(jnp.where(row >= col, 1.0, 0.0), logf, precision=lax.Precision.HIGHEST,
                   preferred_element_type=jnp.float32)
    gT = g.T
    bT = bcum.T
    visible = row <= col
    top = lax.broadcasted_iota(jnp.int32, (LANES, L), 0) < DK
    one_row = jnp.where(lax.broadcasted_iota(jnp.int32, (LANES, L), 0) == 0, 1.0, 0.0)

    for h in range(H):
        pair = slice((h // 2) * LANES, (h // 2 + 1) * LANES)
        qp = qT[pair, :]
        qz = jnp.where(top, qp, 0.0) if h % 2 == 0 else jnp.where(top, 0.0, qp)
        qz = qz.astype(MXU_DTYPE)
        qk_s[h] = jnp.dot(kb[:, pair], qz, preferred_element_type=jnp.float32)
        in_s[h] = jnp.dot(cst[h].astype(MXU_DTYPE), qz, preferred_element_type=jnp.float32)

    stats = []
    for h in range(H):
        acol = g[:, S_IM + h:S_IM + h + 1] - bcum[:, S_FM + h:S_FM + h + 1]
        br = bT[S_FM + h:S_FM + h + 1, :]
        lir = gT[S_IM + h:S_IM + h + 1, :]
        m0 = mst[h:h + 1, 0:1]
        d = jnp.where(visible, br + acol, -jnp.inf)
        m_inter = br + m0
        m_t = jnp.maximum(m_inter, jnp.max(d, axis=0, keepdims=True))
        p = qk_s[h] * jnp.exp(d - m_t)
        p_s[h] = p.astype(MXU_DTYPE)
        b_end = br[:, L - 1:L]
        a = b_end - br + lir
        m_new = jnp.maximum(b_end + m0, jnp.max(a, axis=1, keepdims=True))
        vaug = jnp.concatenate([vT[h * DV:(h + 1) * DV, :], one_row], axis=0)
        kw_s[h] = (vaug * jnp.exp(a - m_new)).astype(MXU_DTYPE)
        stats.append((jnp.sum(p, axis=0, keepdims=True), jnp.exp(m_inter - m_t), m_t,
                      jnp.exp(b_end + m0 - m_new), m_new))

    for h in range(H):
        den_intra, sc, m_t, decay, m_new = stats[h]
        pair = slice((h // 2) * LANES, (h // 2 + 1) * LANES)
        inter = in_s[h]
        num = jnp.dot(vT[h * DV:(h + 1) * DV, :].astype(MXU_DTYPE), p_s[h],
                      preferred_element_type=jnp.float32) + sc * inter[:DV, :]
        den = den_intra + sc * inter[DV:DV + 1, :]
        hh = num / jnp.maximum(jnp.abs(den), jnp.exp(-m_t))
        cst[h] = decay * cst[h] + jnp.dot(kw_s[h], kb[:, pair], preferred_element_type=jnp.float32)
        mst[h:h + 1, :] = jnp.broadcast_to(m_new, (1, LANES))
        mu = jnp.mean(hh, axis=0, keepdims=True)
        hc = hh - mu
        var = jnp.mean(hc * hc, axis=0, keepdims=True)
        hn = (hc * lax.rsqrt(var + LN_EPS)).T * ng_ref[:, h * DV:(h + 1) * DV]
        y_ref[:, h * DV:(h + 1) * DV] = (hn * jax.nn.sigmoid(og_ref[:, h * DV:(h + 1) * DV])).astype(y_ref.dtype)


def _mlstm(om, osm, conv_w, conv_b, gate_bias, norm_g, B, S, L):
    nc = S // L
    W = MLSTM_WIDTH
    assert 2 * QK_M == W
    return pl.pallas_call(
        functools.partial(_mlstm_kernel, L=L),
        grid=(B, nc),
        in_specs=[pl.BlockSpec((L, W), lambda b, c: (b * nc + c, 0)),
                  pl.BlockSpec((L, W), lambda b, c: (b * nc + c, 1)),
                  pl.BlockSpec((L, W), lambda b, c: (b * nc + c, 2)),
                  pl.BlockSpec((L, COLS_S), lambda b, c: (b * nc + c, 0)),
                  pl.BlockSpec((CONV_WIDTH, W), lambda b, c: (0, 0)),
                  pl.BlockSpec((1, W), lambda b, c: (0, 0)),
                  pl.BlockSpec((1, COLS_S), lambda b, c: (0, 0)),
                  pl.BlockSpec((1, W), lambda b, c: (0, 0))],
        out_specs=pl.BlockSpec((L, W), lambda b, c: (b * nc + c, 0)),
        out_shape=jax.ShapeDtypeStruct((B * S, W), MXU_DTYPE),
        scratch_shapes=[pltpu.VMEM((L + 2 * SUBLANES, W), jnp.float32),
                        pltpu.VMEM((MLSTM_HEADS, 2 * MLSTM_V_DIM, LANES), jnp.float32),
                        pltpu.VMEM((SUBLANES, LANES), jnp.float32),
                        pltpu.VMEM((MLSTM_HEADS, L, L), jnp.float32),
                        pltpu.VMEM((MLSTM_HEADS, 2 * MLSTM_V_DIM, L), jnp.float32),
                        pltpu.VMEM((MLSTM_HEADS, L, L), MXU_DTYPE),
                        pltpu.VMEM((MLSTM_HEADS, 2 * MLSTM_V_DIM, L), MXU_DTYPE)],
        compiler_params=_cparams(("parallel", "arbitrary")),
    )(om, om, om, osm, conv_w, conv_b, gate_bias, norm_g)


def _layer_norm(z, g, b):
    mu = jnp.mean(z, axis=1, keepdims=True)
    zc = z - mu
    var = jnp.mean(zc * zc, axis=1, keepdims=True)
    return zc * lax.rsqrt(var + LN_EPS) * g + b


def _merge_kernel(ya_ref, ym_ref, g_ref, x_ref, wa_ref, wm_ref, wo_ref, lg_ref, lb_ref,
                  wr_ref, br_ref, x1_ref, xp_ref, idx_ref, gate_ref, cnt_ref, carry_ref, *, d_model):
    D = d_model
    mix = (jax.nn.sigmoid(g_ref[:, :D]) * jnp.dot(ya_ref[...], wa_ref[...], preferred_element_type=jnp.float32)
           + jax.nn.sigmoid(g_ref[:, D:]) * jnp.dot(ym_ref[...], wm_ref[...], preferred_element_type=jnp.float32))
    y = jnp.dot(mix.astype(MXU_DTYPE), wo_ref[...], preferred_element_type=jnp.float32)
    x1 = _layer_norm(DEEPNORM_ALPHA * x_ref[...] + y, lg_ref[...], lb_ref[...])
    x1_ref[...] = x1
    xb = x1.astype(MXU_DTYPE)
    bits = pltpu.bitcast(x1.astype(jnp.bfloat16).astype(jnp.float32), jnp.uint32)
    xp_ref[...] = (bits[:, :D // 2] & jnp.uint32(0xFFFF0000)) | (bits[:, D // 2:] >> 16)

    logits = jnp.dot(xb, wr_ref[...], preferred_element_type=jnp.float32) + br_ref[...]
    tm = logits.shape[0]
    lane = lax.broadcasted_iota(jnp.int32, (tm, LANES), 1)
    lane_f = lane.astype(jnp.float32)
    vals, idxs = [], []
    for _ in range(TOP_K):
        mx = jnp.max(logits, axis=1, keepdims=True)
        ix = jnp.min(jnp.where(logits == mx, lane_f, float(LANES)), axis=1, keepdims=True)
        vals.append(mx)
        idxs.append(ix)
        logits = jnp.where(lane_f == ix, -jnp.inf, logits)
    es = [jnp.exp(v - vals[0]) for v in vals]
    tot = es[0]
    for e in es[1:]:
        tot = tot + e
    @pl.when(pl.program_id(0) == 0)
    def _():
        carry_ref[...] = jnp.zeros(carry_ref.shape, jnp.float32)

    hots = [lane_f == ix for ix in idxs]
    c = jnp.zeros((tm, LANES), jnp.float32)
    for hot in hots:
        c = c + jnp.where(hot, 1.0, 0.0)
    before = jnp.where(lax.broadcasted_iota(jnp.int32, (tm, tm), 0)
                       > lax.broadcasted_iota(jnp.int32, (tm, tm), 1), 1.0, 0.0).astype(MXU_DTYPE)
    carry = carry_ref[0:1, :]
    prior = jnp.dot(before, c.astype(MXU_DTYPE), preferred_element_type=jnp.float32) + carry
    total = carry + jnp.sum(c, axis=0, keepdims=True)
    carry_ref[...] = jnp.broadcast_to(total, carry_ref.shape)
    cnt_ref[...] = jnp.broadcast_to(total, cnt_ref.shape)

    idx_out = jnp.zeros((tm, LANES), jnp.float32)
    gate_out = jnp.zeros((tm, LANES), jnp.float32)
    for k in range(TOP_K):
        idx_out = jnp.where(lane == k, idxs[k], idx_out)
        rank_k = jnp.sum(jnp.where(hots[k], prior, 0.0), axis=1, keepdims=True)
        idx_out = jnp.where(lane == TOP_K + k, rank_k, idx_out)
        gate_out = jnp.where(lane == k, es[k] / tot, gate_out)
    idx_ref[...] = idx_out.T[:2 * TOP_K, :].astype(jnp.int32)
    gate_ref[...] = gate_out


def _merge(y_att, y_m, og, x2, wa, wm, wo, ln_g, ln_b, wr, br, tm):
    T, D = x2.shape
    full = lambda shape: pl.BlockSpec(shape, lambda i: (0, 0))
    return pl.pallas_call(
        functools.partial(_merge_kernel, d_model=D),
        grid=(T // tm,),
        in_specs=[pl.BlockSpec((tm, ATT_WIDTH), lambda i: (i, 0)),
                  pl.BlockSpec((tm, MLSTM_WIDTH), lambda i: (i, 0)),
                  pl.BlockSpec((tm, 2 * D), lambda i: (i, 0)),
                  pl.BlockSpec((tm, D), lambda i: (i, 0)),
                  full(wa.shape), full(wm.shape), full(wo.shape),
                  full((1, D)), full((1, D)), full(wr.shape), full((1, LANES))],
        out_specs=[pl.BlockSpec((tm, D), lambda i: (i, 0)),
                   pl.BlockSpec((tm, D // 2), lambda i: (i, 0)),
                   pl.BlockSpec((2 * TOP_K, tm), lambda i: (0, i)),
                   pl.BlockSpec((tm, LANES), lambda i: (i, 0)),
                   pl.BlockSpec((SUBLANES, LANES), lambda i: (0, 0))],
        out_shape=[jax.ShapeDtypeStruct((T, D), jnp.float32),
                   jax.ShapeDtypeStruct((T, D // 2), jnp.uint32),
                   jax.ShapeDtypeStruct((2 * TOP_K, T), jnp.int32),
                   jax.ShapeDtypeStruct((T, LANES), jnp.float32),
                   jax.ShapeDtypeStruct((SUBLANES, LANES), jnp.float32)],
        scratch_shapes=[pltpu.VMEM((SUBLANES, LANES), jnp.float32)],
        compiler_params=_cparams(("arbitrary",)),
    )(y_att, y_m, og, x2, wa, wm, wo, ln_g, ln_b, wr, br)


def _dispatch_sc(dest, xp, cap):
    T, W = xp.shape
    n_chunks = T // SC_CHUNK
    n_workers = SC_CORES * SC_SUBCORES
    assert T % SC_CHUNK == 0 and n_chunks % n_workers == 0
    per_worker = n_chunks // n_workers
    idx = dest.reshape(TOP_K, n_chunks, SC_CHUNK).transpose(1, 0, 2)
    mesh = plsc.VectorSubcoreMesh(core_axis_name="c", subcore_axis_name="s")

    @functools.partial(
        pl.kernel, mesh=mesh,
        out_type=jax.ShapeDtypeStruct((cap, W), xp.dtype),
        scratch_types=[pltpu.VMEM((SC_CHUNK, W), xp.dtype),
                       pltpu.VMEM((TOP_K, SC_CHUNK), jnp.int32)])
    def scatter_rows(x_hbm, idx_hbm, xs_hbm, rows_v, idx_v):
        worker = lax.axis_index("s") * SC_CORES + lax.axis_index("c")

        def body(j, carry):
            c = worker * per_worker + j
            pltpu.sync_copy(x_hbm.at[pl.ds(c * SC_CHUNK, SC_CHUNK)], rows_v)
            pltpu.sync_copy(idx_hbm.at[c], idx_v)
            for k in range(TOP_K):
                pltpu.sync_copy(rows_v, xs_hbm.at[idx_v.at[k]])
            return carry

        lax.fori_loop(0, per_worker, body, 0)

    return scatter_rows(xp, idx)


def _ffn_kernel(be_ref, nu_ref, nv_ref, xs_ref, wgu_ref, bgu_ref, wd_ref, bd_ref, y_ref, wgu_b,
                wd_b, act_s, *, d_ff):
    r = pl.program_id(0)
    e = be_ref[r]
    prev = be_ref[jnp.maximum(r - 1, 0)]

    @pl.when((r == 0) | (e != prev))
    def _():
        wgu_b[...] = wgu_ref[0].astype(MXU_DTYPE)
        wd_b[...] = wd_ref[0].astype(MXU_DTYPE)

    @pl.when(r < nu_ref[0])
    def _():
        live = lax.broadcasted_iota(jnp.int32, (xs_ref.shape[0], 1), 0) < nv_ref[r]
        w = jnp.where(live, xs_ref[...], jnp.uint32(0))
        half = w.shape[1]
        x_hi = pltpu.bitcast(w & jnp.uint32(0xFFFF0000), jnp.float32).astype(MXU_DTYPE)
        x_lo = pltpu.bitcast(w << 16, jnp.float32).astype(MXU_DTYPE)
        x = jnp.concatenate([x_hi, x_lo], axis=1)
        step = 512
        for j in range(0, d_ff, step):
            def gu(lo):
                return (jnp.dot(x, wgu_b[:, lo:lo + step], preferred_element_type=jnp.float32)
                        + bgu_ref[0, :, lo:lo + step])
            gate = jnp.minimum(gu(j), SWIGLU_LIMIT)
            up = jnp.clip(gu(d_ff + j), -SWIGLU_LIMIT, SWIGLU_LIMIT)
            act = (up + 1.0) * (gate * jax.nn.sigmoid(SWIGLU_ALPHA * gate))
            act_s[:, j:j + step] = act.astype(MXU_DTYPE)
        y = jnp.dot(act_s[...], wd_b[...], preferred_element_type=jnp.float32) + bd_ref[0]
        bits = pltpu.bitcast(y.astype(jnp.bfloat16).astype(jnp.float32), jnp.uint32)
        y_ref[...] = (bits[:, :half] & jnp.uint32(0xFFFF0000)) | (bits[:, half:] >> 16)

    @pl.when(r >= nu_ref[0])
    def _():
        y_ref[...] = jnp.zeros(y_ref.shape, y_ref.dtype)


def _expert_ffn(block_expert, n_used, n_valid, xs, w_gate_up, b_gate_up, w_down, b_down, bm):
    cap, half = xs.shape
    E, D, F2 = w_gate_up.shape
    d_ff = F2 // 2
    grid_spec = pltpu.PrefetchScalarGridSpec(
        num_scalar_prefetch=3,
        grid=(cap // bm,),
        in_specs=[pl.BlockSpec((bm, half), lambda r, be, nu, nv: (jnp.minimum(r, nu[0] - 1), 0)),
                  pl.BlockSpec((1, D, F2), lambda r, be, nu, nv: (be[r], 0, 0)),
                  pl.BlockSpec((1, 1, F2), lambda r, be, nu, nv: (be[r], 0, 0)),
                  pl.BlockSpec((1, d_ff, D), lambda r, be, nu, nv: (be[r], 0, 0)),
                  pl.BlockSpec((1, 1, D), lambda r, be, nu, nv: (be[r], 0, 0))],
        out_specs=pl.BlockSpec((bm, half), lambda r, be, nu, nv: (r, 0)),
        scratch_shapes=[pltpu.VMEM((D, F2), MXU_DTYPE),
                        pltpu.VMEM((d_ff, D), MXU_DTYPE),
                        pltpu.VMEM((bm, d_ff), MXU_DTYPE)],
    )
    return pl.pallas_call(
        functools.partial(_ffn_kernel, d_ff=d_ff),
        grid_spec=grid_spec,
        out_shape=jax.ShapeDtypeStruct((cap, half), jnp.uint32),
        compiler_params=_cparams(("arbitrary",)),
    )(block_expert, n_used, n_valid, xs, w_gate_up, b_gate_up.reshape(E, 1, F2), w_down,
      b_down.reshape(E, 1, D))


def _gather_sc(dest, ybuf):
    cap, D = ybuf.shape
    T = dest.shape[1]
    chunk = SC_CHUNK
    n_chunks = T // chunk
    n_workers = SC_CORES * SC_SUBCORES
    assert T % chunk == 0 and n_chunks % n_workers == 0
    per_worker = n_chunks // n_workers
    idx = dest.reshape(TOP_K, n_chunks, chunk).transpose(1, 0, 2)
    mesh = plsc.VectorSubcoreMesh(core_axis_name="c", subcore_axis_name="s")

    @functools.partial(
        pl.kernel, mesh=mesh,
        out_type=jax.ShapeDtypeStruct((TOP_K, T, D), ybuf.dtype),
        scratch_types=[pltpu.VMEM((chunk, D), ybuf.dtype),
                       pltpu.VMEM((TOP_K, chunk), jnp.int32)])
    def gather_rows(y_hbm, idx_hbm, out_hbm, rows_v, idx_v):
        worker = lax.axis_index("s") * SC_CORES + lax.axis_index("c")

        def body(j, carry):
            c = worker * per_worker + j
            pltpu.sync_copy(idx_hbm.at[c], idx_v)
            for k in range(TOP_K):
                pltpu.sync_copy(y_hbm.at[idx_v.at[k]], rows_v)
                pltpu.sync_copy(rows_v, out_hbm.at[k, pl.ds(c * chunk, chunk)])
            return carry

        lax.fori_loop(0, per_worker, body, 0)

    return gather_rows(ybuf, idx)


def _combine_dense_kernel(yk_ref, gate_ref, x1_ref, lg_ref, lb_ref, *rest):
    o_ref = rest[-1]
    left = right = None
    for k in range(TOP_K):
        w = yk_ref[k]
        g = gate_ref[:, k:k + 1]
        hi = g * pltpu.bitcast(w & jnp.uint32(0xFFFF0000), jnp.float32)
        lo = g * pltpu.bitcast(w << 16, jnp.float32)
        left = hi if left is None else left + hi
        right = lo if right is None else right + lo
    y = jnp.concatenate([left, right], axis=1)
    o_ref[...] = _layer_norm(DEEPNORM_ALPHA * x1_ref[...] + y, lg_ref[...], lb_ref[...])


def _combine_dense(yk, first_tile, gates, x1, ln_g, ln_b, prev, tm):
    T, D = x1.shape
    n = yk.shape[1] // tm
    in_specs = [pl.BlockSpec((TOP_K, tm, D // 2), lambda i: (0, i, 0)),
                pl.BlockSpec((tm, LANES), lambda i: (i + first_tile, 0)),
                pl.BlockSpec((tm, D), lambda i: (i + first_tile, 0)),
                pl.BlockSpec((1, D), lambda i: (0, 0)),
                pl.BlockSpec((1, D), lambda i: (0, 0))]
    args = [yk, gates, x1, ln_g, ln_b]
    aliases = {}
    if prev is not None:
        in_specs.append(pl.BlockSpec(memory_space=pl.ANY))
        args.append(prev)
        aliases = {len(args) - 1: 0}
    return pl.pallas_call(
        _combine_dense_kernel,
        grid=(n,),
        in_specs=in_specs,
        out_specs=pl.BlockSpec((tm, D), lambda i: (i + first_tile, 0)),
        out_shape=jax.ShapeDtypeStruct((T, D), jnp.float32),
        input_output_aliases=aliases,
        compiler_params=_cparams(("parallel",)),
    )(*args)


def _tile(n, pref):
    t = min(n, pref)
    assert n % t == 0
    return t


def _relayout_w_in(w_in, d_model):
    sizes = (ATT_WIDTH, ATT_WIDTH, ATT_WIDTH, IDX_HEADS * IDX_HEAD_DIM, IDX_HEAD_DIM, IDX_HEADS,
             QK_M, QK_M, MLSTM_WIDTH, MLSTM_HEADS, MLSTM_HEADS, MLSTM_WIDTH, d_model, d_model)
    offs = [0]
    for s in sizes:
        offs.append(offs[-1] + s)
    seg = lambda k: w_in[:, offs[k]:offs[k + 1]]
    (q_a, k_a, v_a, q_i, k_i, w_i, q_m, k_m, v_m, i_m, f_m, o_m, g_a, g_m) = [seg(k) for k in range(14)]
    pad = jnp.zeros((w_in.shape[0], COLS_S - (IDX_HEAD_DIM + IDX_HEADS + 2 * MLSTM_HEADS)), w_in.dtype)
    cols = [q_a, k_a, q_i, k_i, w_i, i_m, f_m, pad, q_m, k_m, v_m, o_m, g_a, g_m, k_i, k_i, v_a]
    return jnp.concatenate(cols, axis=1).astype(MXU_DTYPE)


def _layer(x2, B, S, w_in, conv_w, conv_b, i_bias, f_bias, norm_g, w_branch_attn, w_branch_mlstm,
           w_out, ln1_g, ln1_b, w_router, b_router, w_gate_up, b_gate_up, w_down, b_down,
           ln2_g, ln2_b, rel_bias):
    T, D = x2.shape
    bf = MXU_DTYPE
    tq = _tile(S, 256)
    L = _tile(S, 256)
    tm = _tile(T, 256)

    oa, osm, om, og, okk, v_t = _project(x2, _relayout_w_in(w_in, D), _tile(T, 512), tq)
    bias_t = _bias_tiles(rel_bias, tq)
    y_att = _dsa_attention(oa, osm, okk, v_t, bias_t, B, S, tq)

    gate_bias = jnp.zeros((1, COLS_S), jnp.float32)
    gate_bias = gate_bias.at[0, S_IM:S_IM + MLSTM_HEADS].set(i_bias)
    gate_bias = gate_bias.at[0, S_FM:S_FM + MLSTM_HEADS].set(f_bias)
    y_m = _mlstm(om, osm, conv_w, conv_b.reshape(1, -1), gate_bias, norm_g.reshape(1, -1), B, S, L)

    wr = jnp.zeros((D, LANES), bf).at[:, :N_EXPERTS].set(w_router.astype(bf))
    br = jnp.full((1, LANES), NEG_BIG, jnp.float32).at[0, :N_EXPERTS].set(b_router)
    x1, xp, idx, gates, cnt = _merge(y_att, y_m, og, x2, w_branch_attn.astype(bf),
                                     w_branch_mlstm.astype(bf), w_out.astype(bf), ln1_g.reshape(1, D),
                                     ln1_b.reshape(1, D), wr, br, _tile(T, 512))

    bm = 1024
    counts = cnt[0, :N_EXPERTS].astype(jnp.int32)
    padded = ((counts + bm - 1) // bm) * bm
    pend = jnp.cumsum(padded)
    pstart = pend - padded
    cap = ((T * TOP_K + bm - 1) // bm) * bm + N_EXPERTS * bm
    n_blocks = cap // bm
    experts = jnp.arange(N_EXPERTS, dtype=jnp.int32)[:, None, None]
    dest = jnp.sum(jnp.where(idx[None, :TOP_K, :] == experts, pstart[:, None, None], 0), axis=0) \
        + idx[TOP_K:, :]
    block_row = jnp.arange(n_blocks, dtype=jnp.int32) * bm
    block_expert = jnp.minimum(jnp.sum((pend[None, :] <= block_row[:, None]).astype(jnp.int32), axis=1),
                               N_EXPERTS - 1)
    n_used = (pend[-1:] // bm).astype(jnp.int32)
    n_valid = jnp.clip((pstart + counts)[block_expert] - block_row, 0, bm).astype(jnp.int32)

    xs = _dispatch_sc(dest, xp, cap)
    ybuf = _expert_ffn(block_expert, n_used, n_valid, xs, w_gate_up, b_gate_up, w_down, b_down, bm)
    quantum = SC_CORES * SC_SUBCORES * SC_CHUNK
    parts = (1, 1, 2, 4) if T % (8 * quantum) == 0 else (1,)
    bounds = [0]
    for p in parts:
        bounds.append(bounds[-1] + p * T // sum(parts))
    slabs = [_gather_sc(dest[:, lo:hi], ybuf) for lo, hi in zip(bounds[:-1], bounds[1:])]
    out = None
    tc = _tile(T // sum(parts), 1024)
    for lo, yk in zip(bounds[:-1], slabs):
        out = _combine_dense(yk, lo // tc, gates, x1, ln2_g.reshape(1, D), ln2_b.reshape(1, D),
                             out, tc)
    return out


def kernel(x, w_in, conv_w, conv_b, mlstm_i_bias, mlstm_f_bias, mlstm_norm_g, w_branch_attn,
           w_branch_mlstm, w_out, ln1_g, ln1_b, w_router, b_router, w_gate_up, b_gate_up,
           w_down, b_down, ln2_g, ln2_b, rel_bias):
    B, S, D = x.shape
    x2 = x.reshape(B * S, D)
    for l in range(w_in.shape[0]):
        x2 = _layer(x2, B, S, w_in[l], conv_w[l], conv_b[l], mlstm_i_bias[l], mlstm_f_bias[l],
                    mlstm_norm_g[l], w_branch_attn[l], w_branch_mlstm[l], w_out[l], ln1_g[l], ln1_b[l],
                    w_router[l], b_router[l], w_gate_up[l], b_gate_up[l], w_down[l], b_down[l],
                    ln2_g[l], ln2_b[l], rel_bias)
    return x2.reshape(B, S, D)
```
